```python
import math
import jax
import jax.numpy as jnp
from jax import lax
import numpy as np

D_MODEL = 1024
BATCH = 8
SEQ = 8192
DEPTH = 2

CTX_LEN = 256
GRID_W = 64
EPS = 1e-6
F32 = jnp.float32

HY_W = 512
HY_SHORT = 3
HY_EMB = 33
HY_BANDS = (HY_EMB - 1) // 2
HY_FILT_HID = 64
HY_MOD_SHIFT = 0.05
HY_DECAY_TARGET = 0.01
HY_FAST_PCT = 0.3
HY_SLOW_PCT = 1.5

GDN_HEADS = 4
GDN_DK = 128
GDN_DV = 128
GDN_SHORT = 3
GDN_CHUNK = 64

MLA_HEADS = 8
MLA_NOPE = 64
MLA_ROPE = 32
MLA_V = 64
MLA_Q_LORA = 768
MLA_KV_LORA = 256
ROPE_THETA = 10000.0
Q_BLOCK = 128

N_EXPERTS = 16
N_GROUPS = 4
EXPERTS_PER_GROUP = N_EXPERTS // N_GROUPS
TOP_K = 2
GROUP_SCORE_TOPK = 2
EXPERT_FF = 512

N_BRANCH = 3
IN_SIZES = (3 * HY_W, GDN_HEADS * (2 * GDN_DK + GDN_DV), GDN_HEADS * GDN_DV, 2 * GDN_HEADS, 2 * GDN_HEADS, MLA_Q_LORA, MLA_KV_LORA, MLA_ROPE, N_BRANCH * D_MODEL)

kernel_name = 'hybrid_hyena_gdn_mla_moe_diffusion_trunk'


def rms_norm(x, g):
    xf = x.astype(F32)
    y = xf * lax.rsqrt(jnp.mean(xf * xf, axis=-1, keepdims=True) + EPS)
    return (y * g.astype(F32)).astype(x.dtype)


def l2_normalize(x):
    xf = x.astype(F32)
    return xf * lax.rsqrt(jnp.sum(xf * xf, axis=-1, keepdims=True) + EPS)


def centred_conv(x, w):
    k_w = w.shape[0]
    r = k_w // 2
    L = x.shape[1]
    xp = jnp.pad(x, ((0, 0), (r, r), (0, 0)))
    y = xp[:, 0:L] * w[0]
    for j in range(1, k_w):
        y = y + xp[:, j:j + L] * w[j]
    return y


def split_in(p):
    parts = []
    off = 0
    for n in IN_SIZES:
        parts.append(p[..., off:off + n])
        off += n
    return parts


def hyena_filters(L, lp):
    t = jnp.linspace(0.0, 1.0, L, dtype=F32)[:, None]
    w = (2.0 * math.pi / L) * jnp.arange(L, dtype=F32)[:, None]
    f = jnp.linspace(1e-4, HY_BANDS - 1, HY_BANDS, dtype=F32)[None, :]
    z = jnp.concatenate([t, jnp.cos(f * w), -jnp.sin(f * w)], axis=-1)
    freq = lp['hy_f_freq'].astype(F32)
    hdn = jnp.sin(freq * (z @ lp['hy_f_w1'].astype(F32) + lp['hy_f_b1'].astype(F32)))
    hdn = jnp.sin(freq * (hdn @ lp['hy_f_w2'].astype(F32) + lp['hy_f_b2'].astype(F32)))
    h = hdn @ lp['hy_f_w3'].astype(F32)
    window = jnp.exp(-t * jnp.abs(lp['hy_decay'].astype(F32))) + HY_MOD_SHIFT
    return h * window


def bidir_long_conv(u, filt, bias):
    B, L, C = u.shape
    kbuf = jnp.concatenate([filt[:, :C], jnp.zeros((1, C), F32), filt[:0:-1, C:]], axis=0)
    uf = jnp.fft.rfft(u.astype(F32), n=2 * L, axis=1)
    kf = jnp.fft.rfft(kbuf, axis=0)
    y = jnp.fft.irfft(uf * kf[None], n=2 * L, axis=1)[:, :L]
    return (y + u.astype(F32) * bias.astype(F32)).astype(u.dtype)


def hyena_mixer(p, lp):
    filt = hyena_filters(p.shape[1], lp)
    u = centred_conv(p, lp['hy_conv_w']) + lp['hy_conv_b']
    x0, x1, v = jnp.split(u, 3, axis=-1)
    return x0 * bidir_long_conv(x1 * v, filt, lp['hy_bias'])


def gdn_features(qkv, a, b, lp):
    B, L, _ = qkv.shape
    qkv = jax.nn.silu(centred_conv(qkv, lp['gdn_conv_w']))
    nk = GDN_HEADS * GDN_DK
    q = l2_normalize(qkv[..., :nk].reshape(B, L, GDN_HEADS, GDN_DK)) * (GDN_DK ** -0.5)
    k = l2_normalize(qkv[..., nk:2 * nk].reshape(B, L, GDN_HEADS, GDN_DK))
    v = qkv[..., 2 * nk:].reshape(B, L, GDN_HEADS, GDN_DV).astype(F32)
    a = a.astype(F32).reshape(B, L, 2, GDN_HEADS)
    g = -jnp.exp(lp['gdn_a_log'].astype(F32)) * jax.nn.softplus(a + lp['gdn_dt_bias'].astype(F32))
    beta = jax.nn.sigmoid(b.astype(F32).reshape(B, L, 2, GDN_HEADS))
    return q, k, v, g, beta


def gated_delta_chunked(q, k, v, g, beta, s0):
    B, L, H, DK = q.shape
    DV = v.shape[-1]
    C = GDN_CHUNK
    N = L // C

    def chunks5(a):
        return a.astype(F32).reshape(B, N, C, H, a.shape[-1]).transpose(1, 0, 3, 2, 4)

    def chunks4(a):
        return a.astype(F32).reshape(B, N, C, H).transpose(1, 0, 3, 2)

    qc, kc, vc = chunks5(q), chunks5(k), chunks5(v)
    gc = jnp.cumsum(chunks4(g), axis=-1)
    bc = chunks4(beta)
    causal = jnp.tril(jnp.ones((C, C), dtype=bool))
    strict = jnp.tril(jnp.ones((C, C), dtype=bool), -1)
    gamma = jnp.exp(jnp.where(causal, gc[..., :, None] - gc[..., None, :], -jnp.inf))
    kb = kc * bc[..., None]
    m = jnp.where(strict, jnp.einsum('nbhid,nbhjd->nbhij', kb, kc) * gamma, 0.0)
    eye = jnp.eye(C, dtype=F32)
    t_inv = lax.linalg.triangular_solve(eye + m, jnp.broadcast_to(eye, m.shape), left_side=True, lower=True, unit_diagonal=True)
    u = t_inv @ (vc * bc[..., None])
    w = t_inv @ (kb * jnp.exp(gc)[..., None])
    a_intra = jnp.einsum('nbhid,nbhjd->nbhij', qc, kc) * gamma
    q_dec = qc * jnp.exp(gc)[..., None]
    k_dec = kc * jnp.exp(gc[..., -1:] - gc)[..., None]
    g_end = jnp.exp(gc[..., -1])

    def step(s, xs):
        u_n, w_n, a_n, qd_n, kd_n, ge_n = xs
        v_new = u_n - w_n @ s
        o_n = qd_n @ s + a_n @ v_new
        s = s * ge_n[..., None, None] + jnp.swapaxes(kd_n, -1, -2) @ v_new
        return s, o_n

    s_fin, o = lax.scan(step, s0.astype(F32), (u, w, a_intra, q_dec, k_dec, g_end))
    o = o.transpose(1, 0, 3, 2, 4).reshape(B, L, H, DV)
    return o, s_fin


def gdn_bidir(feats, s0_fwd, s0_bwd):
    q, k, v, g, beta = feats
    o_f, s_f = gated_delta_chunked(q, k, v, g[:, :, 0], beta[:, :, 0], s0_fwd)
    rev = lambda a: jnp.flip(a, axis=1)
    o_b, s_b = gated_delta_chunked(rev(q), rev(k), rev(v), rev(g[:, :, 1]), rev(beta[:, :, 1]), s0_bwd)
    return o_f + rev(o_b), s_f, s_b


def gdn_output(o, z, lp):
    B, L, H, DV = o.shape
    on = o * lax.rsqrt(jnp.mean(o * o, axis=-1, keepdims=True) + EPS) * lp['gdn_norm_g'].astype(F32)
    y = on * jax.nn.silu(z.astype(F32)).reshape(B, L, H, DV)
    return y.reshape(B, L, H * DV).astype(z.dtype) @ lp['gdn_out']


def rope_2d(x, row, col):
    nf = MLA_ROPE // 4
    half = MLA_ROPE // 2
    inv_freq = ROPE_THETA ** (-jnp.arange(nf, dtype=F32) / nf)
    xf = x.astype(F32)

    def rotate(xa, pos):
        ang = pos.astype(F32)[:, None] * inv_freq[None, :]
        cos = jnp.cos(ang)[None, :, None, :]
        sin = jnp.sin(ang)[None, :, None, :]
        x1, x2 = xa[..., :nf], xa[..., nf:]
        return jnp.concatenate([x1 * cos - x2 * sin, x1 * sin + x2 * cos], axis=-1)

    return jnp.concatenate([rotate(xf[..., :half], row), rotate(xf[..., half:], col)], axis=-1).astype(x.dtype)


def mla_queries(cq, lp):
    B, L, _ = cq.shape
    q = (rms_norm(cq, lp['mla_q_norm_g']) @ lp['mla_w_uq']).reshape(B, L, MLA_HEADS, MLA_NOPE + MLA_ROPE)
    return q[..., :MLA_NOPE], q[..., MLA_NOPE:]


def mla_keys_values(ckv, lp):
    B, L, _ = ckv.shape
    kv = (rms_norm(ckv, lp['mla_kv_norm_g']) @ lp['mla_w_ukv']).reshape(B, L, MLA_HEADS, MLA_NOPE + MLA_V)
    return kv[..., :MLA_NOPE], kv[..., MLA_NOPE:]


def mla_attend(qn, qr, kn, kr, v):
    B, S, H, _ = qn.shape
    nb = S // Q_BLOCK
    scale = (MLA_NOPE + MLA_ROPE) ** -0.5

    def to_blocks(a):
        return jnp.swapaxes(a.reshape((B, nb, Q_BLOCK) + a.shape[2:]), 0, 1)

    def one_block(blk):
        qn_b, qr_b = blk
        s = jnp.einsum('bqhd,bkhd->bhqk', qn_b, kn, preferred_element_type=F32)
        s = s + jnp.einsum('bqhr,bkr->bhqk', qr_b, kr, preferred_element_type=F32)
        p = jax.nn.softmax(s * scale, axis=-1)
        return jnp.einsum('bhqk,bkhd->bqhd', p.astype(v.dtype), v)

    o = lax.map(one_block, (to_blocks(qn), to_blocks(qr)))
    return jnp.swapaxes(o, 0, 1).reshape(B, S, H * MLA_V)


def merge_branches(gate_raw, y_hy, y_gdn, y_mla, w_out):
    g_hy, g_gdn, g_mla = jnp.split(jax.nn.sigmoid(gate_raw.astype(F32)), N_BRANCH, axis=-1)
    merged = g_hy * y_hy + g_gdn * y_gdn + g_mla * y_mla
    return merged.astype(w_out.dtype) @ w_out


def moe_ffn(h, router_w, router_b, lp):
    T, D = h.shape
    scores = jax.nn.sigmoid(jnp.dot(h, router_w, preferred_element_type=F32))
    sel = scores + router_b.astype(F32)
    grouped = sel.reshape(T, N_GROUPS, EXPERTS_PER_GROUP)
    group_score = jnp.sum(lax.top_k(grouped, GROUP_SCORE_TOPK)[0], axis=-1)
    grp = jnp.argmax(group_score, axis=-1)
    group_mask = jnp.arange(N_GROUPS)[None, :] == grp[:, None]
    masked = jnp.where(group_mask[:, :, None], grouped, -jnp.inf).reshape(T, N_EXPERTS)
    _, idx = lax.top_k(masked, TOP_K)
    wts = jnp.take_along_axis(scores, idx, axis=-1)
    wts = wts / jnp.sum(wts, axis=-1, keepdims=True)
    gate = jnp.sum(jax.nn.one_hot(idx, N_EXPERTS, dtype=F32) * wts[..., None], axis=1)
    out = jnp.zeros((T, D), F32)
    for e in range(N_EXPERTS):
        he = jax.nn.silu(h @ lp['moe_w1'][e]) * (h @ lp['moe_w3'][e])
        out = out + gate[:, e:e + 1] * (he @ lp['moe_w2'][e])
    return out.astype(h.dtype)


def trunk_layer(x, cx, mod, mod_c, row, col, lp, router_w, router_b, update_ctx):
    B, S, D = x.shape
    sh1, sc1, gt1, sh2, sc2, gt2 = jnp.split(mod[:, None, :], 6, axis=-1)
    csh1, csc1, cgt1, csh2, csc2, cgt2 = jnp.split(mod_c, 6, axis=-1)
    h = rms_norm(x, lp['norm1_g']) * (1.0 + sc1) + sh1
    hc = rms_norm(cx, lp['norm1_g']) * (1.0 + csc1) + csh1
    hy_l, qkv_l, z_l, a_l, b_l, cq_l, ckv_l, kr_l, gate_l = split_in(h @ lp['w_in'])
    hy_c, qkv_c, z_c, a_c, b_c, cq_c, ckv_c, kr_c, gate_c = split_in(hc @ lp['w_in'])

    s_zero = jnp.zeros((B, GDN_HEADS, GDN_DK, GDN_DV), F32)
    o_c, s_f, s_b = gdn_bidir(gdn_features(qkv_c, a_c, b_c, lp), s_zero, s_zero)
    o_l, _, _ = gdn_bidir(gdn_features(qkv_l, a_l, b_l, lp), s_f, s_b)
    y_gdn = gdn_output(o_l, z_l, lp)

    kn_c, v_c = mla_keys_values(ckv_c, lp)
    kn_l, v_l = mla_keys_values(ckv_l, lp)
    kr_lat = rope_2d(kr_l[:, :, None, :], row, col)[:, :, 0, :]
    qn_l, qr_l = mla_queries(cq_l, lp)
    qr_l = rope_2d(qr_l, row, col)
    attn_l = mla_attend(qn_l, qr_l, jnp.concatenate([kn_l, kn_c], axis=1), jnp.concatenate([kr_lat, kr_c], axis=1), jnp.concatenate([v_l, v_c], axis=1))
    y_mla = attn_l @ lp['mla_out']

    y_hy = hyena_mixer(hy_l, lp) @ lp['hy_out']

    mix = merge_branches(gate_l, y_hy, y_gdn, y_mla, lp['w_out'])
    x = x + (gt1 * mix).astype(x.dtype)
    h2 = rms_norm(x, lp['norm2_g']) * (1.0 + sc2) + sh2
    x = x + (gt2 * moe_ffn(h2.reshape(B * S, D), router_w, router_b, lp).reshape(B, S, D)).astype(x.dtype)

    if update_ctx:
        Bc, Lc, _ = cx.shape
        y_hy_c = hyena_mixer(hy_c, lp) @ lp['hy_out']
        y_gdn_c = gdn_output(o_c, z_c, lp)
        qn_c, qr_c = mla_queries(cq_c, lp)
        y_mla_c = mla_attend(qn_c, qr_c, kn_c, kr_c, v_c) @ lp['mla_out']
        mix_c = merge_branches(gate_c, y_hy_c, y_gdn_c, y_mla_c, lp['w_out'])
        cx = cx + (cgt1 * mix_c).astype(cx.dtype)
        hc2 = rms_norm(cx, lp['norm2_g']) * (1.0 + csc2) + csh2
        cx = cx + (cgt2 * moe_ffn(hc2.reshape(Bc * Lc, D), router_w, router_b, lp).reshape(Bc, Lc, D)).astype(cx.dtype)
    return x, cx


def setup_inputs(seed: int = 0) -> dict:
    key = jax.random.key(seed)
    ks = iter(jax.random.split(key, 48))
    D = D_MODEL
    n_in = sum(IN_SIZES)

    def nrm(shape, scale):
        return scale * jax.random.normal(next(ks), shape, F32)

    def gain(shape):
        return 1.0 + 0.05 * jax.random.normal(next(ks), shape, F32)

    dt = jnp.exp(jax.random.uniform(next(ks), (DEPTH, 2, GDN_HEADS), F32, math.log(1e-3), math.log(1e-1)))
    decay_lo = -math.log(HY_DECAY_TARGET) / HY_SLOW_PCT
    decay_hi = -math.log(HY_DECAY_TARGET) / HY_FAST_PCT
    return {
        'x': nrm((BATCH, SEQ, D), 1.0),
        'c': nrm((BATCH, D), 1.0),
        'ctx': nrm((BATCH, CTX_LEN, D), 1.0),
        'c_ctx': nrm((D,), 1.0),
        'w_ada': nrm((DEPTH, D, 6 * D), 0.5 * D ** -0.5),
        'b_ada': nrm((DEPTH, 6 * D), 0.02),
        'norm1_g': gain((DEPTH, D)),
        'norm2_g': gain((DEPTH, D)),
        'w_in': nrm((DEPTH, D, n_in), D ** -0.5),
        'hy_conv_w': nrm((DEPTH, HY_SHORT, 3 * HY_W), HY_SHORT ** -0.5),
        'hy_conv_b': nrm((DEPTH, 3 * HY_W), 0.02),
        'hy_f_w1': nrm((DEPTH, HY_EMB, HY_FILT_HID), HY_EMB ** -0.5),
        'hy_f_b1': nrm((DEPTH, HY_FILT_HID), 0.1),
        'hy_f_w2': nrm((DEPTH, HY_FILT_HID, HY_FILT_HID), HY_FILT_HID ** -0.5),
        'hy_f_b2': nrm((DEPTH, HY_FILT_HID), 0.1),
        'hy_f_w3': nrm((DEPTH, HY_FILT_HID, 2 * HY_W), 0.1 * HY_FILT_HID ** -0.5),
        'hy_f_freq': gain((DEPTH, HY_FILT_HID)),
        'hy_decay': jax.random.uniform(next(ks), (DEPTH, 2 * HY_W), F32, decay_lo, decay_hi),
        'hy_bias': nrm((DEPTH, HY_W), 0.5),
        'hy_out': nrm((DEPTH, HY_W, D), HY_W ** -0.5),
        'gdn_conv_w': nrm((DEPTH, GDN_SHORT, GDN_HEADS * (2 * GDN_DK + GDN_DV)), GDN_SHORT ** -0.5),
        'gdn_a_log': jnp.log(jax.random.uniform(next(ks), (DEPTH, 2, GDN_HEADS), F32, 1.0, 16.0)),
        'gdn_dt_bias': dt + jnp.log(-jnp.expm1(-dt)),
        'gdn_norm_g': gain((DEPTH, GDN_DV)),
        'gdn_out': nrm((DEPTH, GDN_HEADS * GDN_DV, D), (GDN_HEADS * GDN_DV) ** -0.5),
        'mla_q_norm_g': gain((DEPTH, MLA_Q_LORA)),
        'mla_w_uq': nrm((DEPTH, MLA_Q_LORA, MLA_HEADS * (MLA_NOPE + MLA_ROPE)), MLA_Q_LORA ** -0.5),
        'mla_kv_norm_g': gain((DEPTH, MLA_KV_LORA)),
        'mla_w_ukv': nrm((DEPTH, MLA_KV_LORA, MLA_HEADS * (MLA_NOPE + MLA_V)), MLA_KV_LORA ** -0.5),
        'mla_out': nrm((DEPTH, MLA_HEADS * MLA_V, D), (MLA_HEADS * MLA_V) ** -0.5),
        'w_out': nrm((DEPTH, D, D), D ** -0.5),
        'moe_w1': nrm((DEPTH, N_EXPERTS, D, EXPERT_FF), D ** -0.5),
        'moe_w3': nrm((DEPTH, N_EXPERTS, D, EXPERT_FF), D ** -0.5),
        'moe_w2': nrm((DEPTH, N_EXPERTS, EXPERT_FF, D), EXPERT_FF ** -0.5),
        'router_w': nrm((D, N_EXPERTS), D ** -0.5),
        'router_b': nrm((N_EXPERTS,), 0.01),
        'final_norm_g': gain((D,)),
    }


def reference(x, c, ctx, c_ctx, w_ada, b_ada, norm1_g, norm2_g, w_in, hy_conv_w, hy_conv_b, hy_f_w1, hy_f_b1, hy_f_w2, hy_f_b2, hy_f_w3, hy_f_freq, hy_decay, hy_bias, hy_out, gdn_conv_w, gdn_a_log, gdn_dt_bias, gdn_norm_g, gdn_out, mla_q_norm_g, mla_w_uq, mla_kv_norm_g, mla_w_ukv, mla_out, w_out, moe_w1, moe_w3, moe_w2, router_w, router_b, final_norm_g):
    S = x.shape[1]
    rows = S // GRID_W
    row = jnp.repeat(jnp.arange(rows, dtype=jnp.int32), GRID_W)
    col = jnp.tile(jnp.arange(GRID_W, dtype=jnp.int32), rows)
    cx = ctx
    for l in range(DEPTH):
        lp = {
            'norm1_g': norm1_g[l], 'norm2_g': norm2_g[l], 'w_in': w_in[l],
            'hy_conv_w': hy_conv_w[l], 'hy_conv_b': hy_conv_b[l],
            'hy_f_w1': hy_f_w1[l], 'hy_f_b1': hy_f_b1[l], 'hy_f_w2': hy_f_w2[l], 'hy_f_b2': hy_f_b2[l],
            'hy_f_w3': hy_f_w3[l], 'hy_f_freq': hy_f_freq[l], 'hy_decay': hy_decay[l], 'hy_bias': hy_bias[l],
            'hy_out': hy_out[l],
            'gdn_conv_w': gdn_conv_w[l], 'gdn_a_log': gdn_a_log[l], 'gdn_dt_bias': gdn_dt_bias[l],
            'gdn_norm_g': gdn_norm_g[l], 'gdn_out': gdn_out[l],
            'mla_q_norm_g': mla_q_norm_g[l], 'mla_w_uq': mla_w_uq[l], 'mla_kv_norm_g': mla_kv_norm_g[l],
            'mla_w_ukv': mla_w_ukv[l], 'mla_out': mla_out[l],
            'w_out': w_out[l], 'moe_w1': moe_w1[l], 'moe_w3': moe_w3[l], 'moe_w2': moe_w2[l],
        }
        mod = jax.nn.silu(c) @ w_ada[l] + b_ada[l]
        mod_c = jax.nn.silu(c_ctx) @ w_ada[l] + b_ada[l]
        x, cx = trunk_layer(x, cx, mod, mod_c, row, col, lp, router_w, router_b, l < DEPTH - 1)
    return rms_norm(x, final_norm_g)
```

```python
import functools
import math

import jax
import jax.numpy as jnp
from jax import lax
from jax.experimental import pallas as pl
from jax.experimental.pallas import tpu as pltpu

F32 = jnp.float32
BF16 = jnp.bfloat16
HI = lax.Precision.HIGHEST
EPS = 1e-6

GRID_W = 64
HY_W = 512
HY_EMB = 33
HY_BANDS = (HY_EMB - 1) // 2
HY_MOD_SHIFT = 0.05
GDN_HEADS = 4
GDN_DK = 128
GDN_DV = 128
GDN_CHUNK = 64
MLA_HEADS = 8
MLA_NOPE = 64
MLA_ROPE = 32
MLA_V = 64
MLA_Q_LORA = 768
MLA_KV_LORA = 256
ROPE_THETA = 10000.0
N_EXPERTS = 16
N_GROUPS = 4
EXPERTS_PER_GROUP = N_EXPERTS // N_GROUPS
EXPERT_FF = 512
LANES = 128
SUBLANES = 8
VMEM_LIMIT = 56 * 1024 * 1024


def _cparams(*sem):
    return pltpu.CompilerParams(dimension_semantics=sem, vmem_limit_bytes=VMEM_LIMIT)


def _silu(x):
    return x * jax.nn.sigmoid(x)


def _ada_kernel(c_ref, w_ref, b_ref, o_ref):
    a = _silu(c_ref[...])
    o_ref[...] = jnp.dot(a, w_ref[...], precision=HI, preferred_element_type=F32) + b_ref[...]


def ada_mod(cc, w, b):
    R, D = cc.shape
    N = w.shape[1]
    tn = 1536
    return pl.pallas_call(
        _ada_kernel,
        grid=(N // tn,),
        in_specs=[pl.BlockSpec((R, D), lambda j: (0, 0)),
                  pl.BlockSpec((D, tn), lambda j: (0, j)),
                  pl.BlockSpec((1, tn), lambda j: (0, j))],
        out_specs=pl.BlockSpec((R, tn), lambda j: (0, j)),
        out_shape=jax.ShapeDtypeStruct((R, N), F32),
        compiler_params=_cparams("parallel"),
        name="ada_mod",
    )(cc, w, b)


def _inproj_kernel(x_ref, g_ref, sc_ref, sh_ref, w_ref, o_ref):
    x = x_ref[...]
    y = x * lax.rsqrt(jnp.mean(x * x, axis=-1, keepdims=True) + EPS) * g_ref[...]
    h = y * (1.0 + sc_ref[...]) + sh_ref[...]
    o_ref[...] = jnp.dot(h.astype(BF16), w_ref[...], preferred_element_type=F32).astype(o_ref.dtype)


def _mod_spec(m, D):
    if m.shape[0] == 1:
        return pl.BlockSpec((None, 1, D), lambda b, i: (0, 0, 0))
    return pl.BlockSpec((None, 1, D), lambda b, i: (b, 0, 0))


def in_proj(x, g, sc, sh, w, tm):
    B, S, D = x.shape
    N = w.shape[1]
    tm = min(tm, S)
    return pl.pallas_call(
        _inproj_kernel,
        grid=(B, S // tm),
        in_specs=[pl.BlockSpec((None, tm, D), lambda b, i: (b, i, 0)),
                  pl.BlockSpec((1, D), lambda b, i: (0, 0)),
                  _mod_spec(sc, D), _mod_spec(sh, D),
                  pl.BlockSpec((D, N), lambda b, i: (0, 0))],
        out_specs=pl.BlockSpec((None, tm, N), lambda b, i: (b, i, 0)),
        out_shape=jax.ShapeDtypeStruct((B, S, N), F32),
        compiler_params=_cparams("parallel", "parallel"),
        name="in_proj",
    )(x, g, sc, sh, w)


def _halo_specs(tm, S, C, col_block=0):
    nb8 = tm // SUBLANES
    last8 = S // SUBLANES - 1
    main = pl.BlockSpec((None, tm, C), lambda b, i: (b, i, col_block))
    prev = pl.BlockSpec((None, SUBLANES, C), lambda b, i: (b, jnp.maximum(i * nb8 - 1, 0), col_block))
    nxt = pl.BlockSpec((None, SUBLANES, C), lambda b, i: (b, jnp.minimum((i + 1) * nb8, last8), col_block))
    return main, prev, nxt


def _conv3(x, prev8, next8, w, first, last):
    tm = x.shape[0]
    row = lax.broadcasted_iota(jnp.int32, x.shape, 0)
    p_row = jnp.where(first, 0.0, prev8[SUBLANES - 1:SUBLANES, :])
    n_row = jnp.where(last, 0.0, next8[0:1, :])
    x_prev = jnp.where(row == 0, p_row, pltpu.roll(x, 1, 0))
    x_next = jnp.where(row == tm - 1, n_row, pltpu.roll(x, tm - 1, 0))
    return x_prev * w[0:1, :] + x * w[1:2, :] + x_next * w[2:3, :]


def _hy_pre_kernel(p_ref, pp_ref, pn_ref, w_ref, b_ref, x0_ref, u_ref):
    i = pl.program_id(1)
    y = _conv3(p_ref[...], pp_ref[...], pn_ref[...], w_ref[...], i == 0, i == pl.num_programs(1) - 1)
    y = y + b_ref[...]
    x0_ref[...] = y[:, :HY_W]
    u_ref[...] = y[:, HY_W:2 * HY_W] * y[:, 2 * HY_W:]


def hy_pre(p_hy, conv_w, conv_b, tm):
    B, S, C = p_hy.shape
    tm = min(tm, S)
    main, prev, nxt = _halo_specs(tm, S, C)
    o_spec = pl.BlockSpec((None, tm, HY_W), lambda b, i: (b, i, 0))
    return pl.pallas_call(
        _hy_pre_kernel,
        grid=(B, S // tm),
        in_specs=[main, prev, nxt,
                  pl.BlockSpec((3, C), lambda b, i: (0, 0)),
                  pl.BlockSpec((1, C), lambda b, i: (0, 0))],
        out_specs=[o_spec, o_spec],
        out_shape=[jax.ShapeDtypeStruct((B, S, HY_W), F32)] * 2,
        compiler_params=_cparams("parallel", "parallel"),
        name="hy_pre",
    )(p_hy, p_hy, p_hy, conv_w, conv_b)


def _hy_filter_kernel(z_ref, w1_ref, b1_ref, w2_ref, b2_ref, w3_ref, fq_ref, dc_ref, o_ref):
    fq = fq_ref[...]
    h = jnp.sin(fq * (jnp.dot(z_ref[...], w1_ref[...], precision=HI, preferred_element_type=F32) + b1_ref[...]))
    h = jnp.sin(fq * (jnp.dot(h, w2_ref[...], precision=HI, preferred_element_type=F32) + b2_ref[...]))
    h = jnp.dot(h, w3_ref[...], precision=HI, preferred_element_type=F32)
    window = jnp.exp(-z_ref[:, 0:1] * jnp.abs(dc_ref[...])) + HY_MOD_SHIFT
    o_ref[...] = h * window


def hy_filter(L, w1, b1, w2, b2, w3, freq, decay):
    t = jnp.linspace(0.0, 1.0, L, dtype=F32)[:, None]
    w = (2.0 * math.pi / L) * jnp.arange(L, dtype=F32)[:, None]
    f = jnp.linspace(1e-4, HY_BANDS - 1, HY_BANDS, dtype=F32)[None, :]
    z = jnp.concatenate([t, jnp.cos(f * w), -jnp.sin(f * w)], axis=-1)
    emb_pad = LANES - HY_EMB
    z = jnp.pad(z, ((0, 0), (0, emb_pad)))
    w1 = jnp.pad(w1, ((0, emb_pad), (0, 0)))
    hid = w1.shape[1]
    tl = min(L, 512)
    full = lambda a: pl.BlockSpec(a.shape, lambda i: (0, 0))
    b1, b2, freq, decay = b1[None, :], b2[None, :], freq[None, :], decay[None, :]
    return pl.pallas_call(
        _hy_filter_kernel,
        grid=(L // tl,),
        in_specs=[pl.BlockSpec((tl, LANES), lambda i: (i, 0)),
                  full(w1), full(b1), full(w2), full(b2), full(w3), full(freq), full(decay)],
        out_specs=pl.BlockSpec((tl, 2 * HY_W), lambda i: (i, 0)),
        out_shape=jax.ShapeDtypeStruct((L, 2 * HY_W), F32),
        compiler_params=_cparams("parallel"),
        name="hy_filter",
    )(z, w1, b1, w2, b2, w3, freq, decay)


def _dft_tables(N1, N2):
    N = N1 * N2
    two_pi = 2.0 * math.pi

    def cs(num, den):
        ang = (two_pi / den) * (num % den).astype(F32)
        return jnp.cos(ang), jnp.sin(ang)

    a1 = jnp.arange(N1, dtype=jnp.int32)
    a2 = jnp.arange(N2, dtype=jnp.int32)
    c1, s1 = cs(a1[:, None] * a1[None, :], N1)
    c2, s2 = cs(a2[:, None] * a2[None, :], N2)
    ct, st = cs(a2[:, None] * a1[None, :], N)

    def stack(re, im):
        return jnp.concatenate([jnp.concatenate([re, -im], axis=-1),
                                jnp.concatenate([im, re], axis=-1)], axis=-2)

    tr = ct[:, :, None] * c1[None] - st[:, :, None] * s1[None]
    ti = -(ct[:, :, None] * s1[None] + st[:, :, None] * c1[None])
    h = N1 // 2
    m1_data = stack(tr[:, :, :h], ti[:, :, :h])
    m1_real = jnp.concatenate([tr, ti], axis=-2)
    m2 = stack(c2, -s2)
    ctk, stk = ct.T, st.T
    gr = ctk[:, :, None] * c2.T[None] - stk[:, :, None] * s2.T[None]
    gi = ctk[:, :, None] * s2.T[None] + stk[:, :, None] * c2.T[None]
    m2inv = stack(gr, gi)
    er, ei = c1.T[:h] / N, s1.T[:h] / N
    m3 = stack(er, ei)
    return m1_data, m1_real, m2, m2inv, m3


def _hy_pass_a_kernel(u_ref, m_ref, o_ref):
    n1 = o_ref.shape[1]
    for j in range(SUBLANES):
        xj = jnp.concatenate([u_ref[0, :, j, :], u_ref[1, :, j, :]], axis=0)
        a = jnp.dot(m_ref[j], xj, precision=HI, preferred_element_type=F32)
        o_ref[0, :, j, :] = a[:n1]
        o_ref[1, :, j, :] = a[n1:]


def _hy_pass_a(u4, m1, n_pairs):
    _, h, N2, C = u4.shape
    N1 = 2 * h
    return pl.pallas_call(
        _hy_pass_a_kernel,
        grid=(n_pairs, N2 // SUBLANES),
        in_specs=[pl.BlockSpec((2, h, SUBLANES, C), lambda p, j: (p, 0, j, 0)),
                  pl.BlockSpec((SUBLANES, 2 * N1, N1), lambda p, j: (j, 0, 0))],
        out_specs=pl.BlockSpec((2, N1, SUBLANES, C), lambda p, j: (0, 0, j, p)),
        out_shape=jax.ShapeDtypeStruct((2, N1, N2, n_pairs * C), F32),
        compiler_params=_cparams("parallel", "parallel"),
        name="hy_pass_a",
    )(u4, m1)


def _hy_spec_kernel(a_ref, m2_ref, o_ref):
    n2 = a_ref.shape[2]
    for j in range(SUBLANES):
        a = jnp.concatenate([a_ref[0, j], a_ref[1, j]], axis=0)
        x = jnp.dot(m2_ref[...], a, precision=HI, preferred_element_type=F32)
        o_ref[0, j] = x[:n2]
        o_ref[1, j] = x[n2:]


def _hy_spectrum(a, m2):
    _, N1, N2, C = a.shape
    spec = pl.BlockSpec((2, SUBLANES, N2, C), lambda k: (0, k, 0, 0))
    return pl.pallas_call(
        _hy_spec_kernel,
        grid=(N1 // SUBLANES,),
        in_specs=[spec, pl.BlockSpec((2 * N2, 2 * N2), lambda k: (0, 0))],
        out_specs=spec,
        out_shape=jax.ShapeDtypeStruct(a.shape, F32),
        compiler_params=_cparams("parallel"),
        name="hy_spectrum",
    )(a, m2)


def _hy_pass_b_kernel(a_ref, k_ref, m2_ref, mi_ref, o_ref):
    n2 = a_ref.shape[2]
    for j in range(SUBLANES):
        a = jnp.concatenate([a_ref[0, j], a_ref[1, j]], axis=0)
        x = jnp.dot(m2_ref[...], a, precision=HI, preferred_element_type=F32)
        xr, xi = x[:n2], x[n2:]
        kr, ki = k_ref[0, j], k_ref[1, j]
        y = jnp.concatenate([xr * kr - xi * ki, xr * ki + xi * kr], axis=0)
        b = jnp.dot(mi_ref[j], y, precision=HI, preferred_element_type=F32)
        o_ref[0, :, j, :] = b[:n2]
        o_ref[1, :, j, :] = b[n2:]


def _hy_pass_b(a, kf, m2, m2inv, n_pairs):
    _, N1, N2, PC = a.shape
    C = PC // n_pairs
    return pl.pallas_call(
        _hy_pass_b_kernel,
        grid=(n_pairs, N1 // SUBLANES),
        in_specs=[pl.BlockSpec((2, SUBLANES, N2, C), lambda p, k: (0, k, 0, p)),
                  pl.BlockSpec((2, SUBLANES, N2, C), lambda p, k: (0, k, 0, 0)),
                  pl.BlockSpec((2 * N2, 2 * N2), lambda p, k: (0, 0)),
                  pl.BlockSpec((SUBLANES, 2 * N2, 2 * N2), lambda p, k: (k, 0, 0))],
        out_specs=pl.BlockSpec((2, N2, SUBLANES, C), lambda p, k: (0, 0, k, p)),
        out_shape=jax.ShapeDtypeStruct((2, N2, N1, PC), F32),
        compiler_params=_cparams("parallel", "parallel"),
        name="hy_pass_b",
    )(a, kf, m2, m2inv)


def _hy_pass_c_kernel(b_ref, u_ref, x0_ref, bias_ref, m3_ref, o_ref):
    h = o_ref.shape[1]
    bias = bias_ref[...]
    for j in range(SUBLANES):
        bb = jnp.concatenate([b_ref[0, j], b_ref[1, j]], axis=0)
        y = jnp.dot(m3_ref[...], bb, precision=HI, preferred_element_type=F32)
        for r in range(2):
            o_ref[r, :, j, :] = x0_ref[r, :, j, :] * (y[r * h:(r + 1) * h] + bias * u_ref[r, :, j, :])


def _hy_pass_c(bq, u4, x04, bias, m3, n_pairs):
    _, N2, N1, PC = bq.shape
    C = PC // n_pairs
    h = N1 // 2
    io = pl.BlockSpec((2, h, SUBLANES, C), lambda p, j: (p, 0, j, 0))
    return pl.pallas_call(
        _hy_pass_c_kernel,
        grid=(n_pairs, N2 // SUBLANES),
        in_specs=[pl.BlockSpec((2, SUBLANES, N1, C), lambda p, j: (0, j, 0, p)),
                  io, io,
                  pl.BlockSpec((1, C), lambda p, j: (0, 0)),
                  pl.BlockSpec((N1, 2 * N1), lambda p, j: (0, 0))],
        out_specs=io,
        out_shape=jax.ShapeDtypeStruct(u4.shape, F32),
        compiler_params=_cparams("parallel", "parallel"),
        name="hy_pass_c",
    )(bq, u4, x04, bias, m3)


def hyena_long_conv(x0, u, filt, bias):
    B, L, C = u.shape
    N2 = min(128, L // 32)
    N1 = 2 * L // N2
    n_pairs = B // 2
    m1_data, m1_real, m2, m2inv, m3 = _dft_tables(N1, N2)
    kbuf = jnp.concatenate([filt[:, :C], jnp.zeros((1, C), F32), filt[:0:-1, C:]], axis=0)
    kf = _hy_spectrum(_hy_pass_a(kbuf.reshape(2, N1 // 2, N2, C), m1_real, 1), m2)
    u4 = u.reshape(B, N1 // 2, N2, C)
    a = _hy_pass_a(u4, m1_data, n_pairs)
    bq = _hy_pass_b(a, kf, m2, m2inv, n_pairs)
    y = _hy_pass_c(bq, u4, x0.reshape(u4.shape), bias[None, :], m3, n_pairs)
    return y.reshape(B, L, C)


GDN_QKV = GDN_HEADS * (2 * GDN_DK + GDN_DV)


def _gdn_pre_kernel(p_ref, pp_ref, pn_ref, w_ref, alog_ref, dtb_ref, q_ref, k_ref, v_ref, gb_ref):
    i = pl.program_id(1)
    C = GDN_QKV
    y = _conv3(p_ref[:, :C], pp_ref[:, :C], pn_ref[:, :C], w_ref[...], i == 0, i == pl.num_programs(1) - 1)
    y = _silu(y)
    nk = GDN_HEADS * GDN_DK
    for h in range(GDN_HEADS):
        sl = slice(h * GDN_DK, (h + 1) * GDN_DK)
        qh = y[:, sl]
        kh = y[:, nk + h * GDN_DK:nk + (h + 1) * GDN_DK]
        q_ref[:, sl] = qh * (lax.rsqrt(jnp.sum(qh * qh, axis=-1, keepdims=True) + EPS) * (GDN_DK ** -0.5))
        k_ref[:, sl] = kh * lax.rsqrt(jnp.sum(kh * kh, axis=-1, keepdims=True) + EPS)
    v_ref[...] = y[:, 2 * nk:]
    s = p_ref[:, C:]
    lane = lax.broadcasted_iota(jnp.int32, s.shape, 1)
    xa = s + dtb_ref[...]
    softplus = jnp.maximum(xa, 0.0) + jnp.log1p(jnp.exp(-jnp.abs(xa)))
    g = -jnp.exp(alog_ref[...]) * softplus
    gb_ref[...] = jnp.where(lane < 2 * GDN_HEADS, g, jnp.where(lane < 4 * GDN_HEADS, jax.nn.sigmoid(s), 0.0))


def gdn_pre(p_gdn, conv_w, a_log, dt_bias, tm):
    B, S, C = p_gdn.shape
    tm = min(tm, S)
    main, prev, nxt = _halo_specs(tm, S, C)
    pad = LANES - 2 * GDN_HEADS
    alog = jnp.pad(a_log.reshape(1, -1), ((0, 0), (0, pad)))
    dtb = jnp.pad(dt_bias.reshape(1, -1), ((0, 0), (0, pad)))
    nv = GDN_HEADS * GDN_DV
    o_spec = pl.BlockSpec((None, tm, nv), lambda b, i: (b, i, 0))
    return pl.pallas_call(
        _gdn_pre_kernel,
        grid=(B, S // tm),
        in_specs=[main, prev, nxt,
                  pl.BlockSpec((3, GDN_QKV), lambda b, i: (0, 0)),
                  pl.BlockSpec((1, LANES), lambda b, i: (0, 0)),
                  pl.BlockSpec((1, LANES), lambda b, i: (0, 0))],
        out_specs=[o_spec, o_spec, o_spec, pl.BlockSpec((None, tm, LANES), lambda b, i: (b, i, 0))],
        out_shape=[jax.ShapeDtypeStruct((B, S, nv), F32)] * 3 + [jax.ShapeDtypeStruct((B, S, LANES), F32)],
        compiler_params=_cparams("parallel", "parallel"),
        name="gdn_pre",
    )(p_gdn, p_gdn, p_gdn, conv_w, alog, dtb)


def _bdot(a, b):
    return jnp.dot(a.astype(BF16), b.astype(BF16), preferred_element_type=F32)


def _bdot_nt(a, b):
    return lax.dot_general(a.astype(BF16), b.astype(BF16), (((1,), (1,)), ((), ())), preferred_element_type=F32)


def _bdot_tn(a, b):
    return lax.dot_general(a.astype(BF16), b.astype(BF16), (((0,), (0,)), ((), ())), preferred_element_type=F32)


def _gdn_scan_kernel(qf_ref, kf_ref, vf_ref, gf_ref, qb_ref, kb_ref, vb_ref, gb_ref, of_ref, ob_ref, s_ref):
    C = GDN_CHUNK
    H = GDN_HEADS

    @pl.when(pl.program_id(1) == 0)
    def _():
        s_ref[...] = jnp.zeros_like(s_ref)

    ri = lax.broadcasted_iota(jnp.int32, (C, C), 0)
    ci = lax.broadcasted_iota(jnp.int32, (C, C), 1)
    eye = (ri == ci).astype(F32)
    for d, (q_ref, k_ref, v_ref, g_ref, o_ref) in enumerate(
            ((qf_ref, kf_ref, vf_ref, gf_ref, of_ref), (qb_ref, kb_ref, vb_ref, gb_ref, ob_ref))):
        incl = (ci <= ri) if d == 0 else (ci >= ri)
        strict = (ci < ri) if d == 0 else (ci > ri)
        gbv = g_ref[...]
        gc_all = jnp.dot(incl.astype(F32), gbv, precision=HI, preferred_element_type=F32)
        gc_t = gc_all.T
        end = C - 1 if d == 0 else 0
        for h in range(H):
            c = d * H + h
            sl = slice(h * GDN_DK, (h + 1) * GDN_DK)
            q, k, v = q_ref[:, sl], k_ref[:, sl], v_ref[:, sl]
            beta = gbv[:, 2 * H + c:2 * H + c + 1]
            gc_c = gc_all[:, c:c + 1]
            gc_r = gc_t[c:c + 1, :]
            g_tot = gc_all[end:end + 1, c:c + 1]
            gamma = jnp.where(incl, jnp.exp(jnp.where(incl, gc_c - gc_r, 0.0)), 0.0)
            e_c = jnp.exp(gc_c)
            kb = k * beta
            m = jnp.where(strict, _bdot_nt(kb, k) * gamma, 0.0)
            t = eye - m
            pw = m
            for _ in range(5):
                pw = _bdot(pw, pw)
                t = t + _bdot(t, pw)
            uw = _bdot(t, jnp.concatenate([v * beta, kb * e_c], axis=1))
            u, w = uw[:, :GDN_DV], uw[:, GDN_DV:]
            a_intra = _bdot_nt(q, k) * gamma
            s = s_ref[c]
            ws = _bdot(jnp.concatenate([w, q * e_c], axis=0), s)
            v_new = u - ws[:C]
            o_ref[:, sl] = ws[C:] + _bdot(a_intra, v_new)
            k_dec = k * jnp.exp(g_tot - gc_c)
            s_ref[c] = s * jnp.exp(g_tot) + _bdot_tn(k_dec, v_new)


def gdn_scan(q, k, v, gb, n_ctx_chunks):
    B, Lt, NV = q.shape
    C = GDN_CHUNK
    n = Lt // C

    def fwd(b, i):
        return (b, i, 0)

    def bwd(b, i):
        return (b, jnp.where(i < n_ctx_chunks, n_ctx_chunks - 1 - i, n - 1 - (i - n_ctx_chunks)), 0)

    def specs(imap):
        return [pl.BlockSpec((None, C, NV), imap)] * 3 + [pl.BlockSpec((None, C, LANES), imap)]

    return pl.pallas_call(
        _gdn_scan_kernel,
        grid=(B, n),
        in_specs=specs(fwd) + specs(bwd),
        out_specs=[pl.BlockSpec((None, C, NV), fwd), pl.BlockSpec((None, C, NV), bwd)],
        out_shape=[jax.ShapeDtypeStruct((B, Lt, NV), F32)] * 2,
        scratch_shapes=[pltpu.VMEM((2 * GDN_HEADS, GDN_DK, GDN_DV), F32)],
        compiler_params=_cparams("parallel", "arbitrary"),
        name="gdn_scan",
    )(q, k, v, gb, q, k, v, gb)


MLA_QK = MLA_NOPE + MLA_ROPE
MLA_KVIN = MLA_KV_LORA + LANES


def _rope_partner(w_rope):
    nf = MLA_ROPE // 4
    parts = []
    for half in range(2):
        a = w_rope[..., half * 2 * nf:half * 2 * nf + nf]
        b = w_rope[..., half * 2 * nf + nf:(half + 1) * 2 * nf]
        parts += [-b, a]
    return jnp.concatenate(parts, axis=-1)


def _head_pad(nope, rope):
    pad = jnp.zeros(nope.shape[:-1] + (LANES - MLA_QK,), nope.dtype)
    out = jnp.concatenate([nope, rope, pad], axis=-1)
    return out.reshape(out.shape[:-2] + (MLA_HEADS * LANES,))


def mla_weights(w_uq, w_ukv):
    wq = w_uq.reshape(MLA_Q_LORA, MLA_HEADS, MLA_QK)
    qn, qr = wq[..., :MLA_NOPE], wq[..., MLA_NOPE:]
    wq2 = jnp.concatenate([_head_pad(qn, qr), _head_pad(jnp.zeros_like(qn), _rope_partner(qr))], axis=-1)
    wkv = w_ukv.reshape(MLA_KV_LORA, MLA_HEADS, MLA_NOPE + MLA_V)
    kn, vv = wkv[..., :MLA_NOPE], wkv[..., MLA_NOPE:]
    eye = jnp.broadcast_to(jnp.eye(MLA_ROPE, dtype=F32)[:, None, :], (MLA_ROPE, MLA_HEADS, MLA_ROPE))
    z_kn = jnp.zeros((MLA_ROPE, MLA_HEADS, MLA_NOPE), F32)
    top = jnp.concatenate([_head_pad(kn, jnp.zeros((MLA_KV_LORA, MLA_HEADS, MLA_ROPE), F32)),
                           _head_pad(jnp.zeros_like(kn), jnp.zeros((MLA_KV_LORA, MLA_HEADS, MLA_ROPE), F32)),
                           vv.reshape(MLA_KV_LORA, MLA_HEADS * MLA_V)], axis=-1)
    mid = jnp.concatenate([_head_pad(z_kn, eye), _head_pad(z_kn, _rope_partner(eye)),
                           jnp.zeros((MLA_ROPE, MLA_HEADS * MLA_V), F32)], axis=-1)
    bot = jnp.zeros((MLA_KVIN - MLA_KV_LORA - MLA_ROPE, top.shape[1]), F32)
    return wq2.astype(BF16), jnp.concatenate([top, mid, bot], axis=0).astype(BF16)


def rope_tables(row, col):
    nf = MLA_ROPE // 4
    inv_freq = ROPE_THETA ** (-jnp.arange(nf, dtype=F32) / nf)
    ang = jnp.concatenate([row.astype(F32)[:, None] * inv_freq[None, :]] * 2
                          + [col.astype(F32)[:, None] * inv_freq[None, :]] * 2, axis=-1)
    n = ang.shape[0]
    pad = jnp.zeros((n, LANES - MLA_QK), F32)
    cos = jnp.concatenate([jnp.ones((n, MLA_NOPE), F32), jnp.cos(ang), pad], axis=-1)
    sin = jnp.concatenate([jnp.zeros((n, MLA_NOPE), F32), jnp.sin(ang), pad], axis=-1)
    return cos, sin


def _mla_proj_kernel(p_ref, gq_ref, gkv_ref, wq_ref, wk_ref, cos_ref, sin_ref, q_ref, k_ref, v_ref):
    HL = MLA_HEADS * LANES
    cos = jnp.concatenate([cos_ref[...]] * MLA_HEADS, axis=1)
    sin = jnp.concatenate([sin_ref[...]] * MLA_HEADS, axis=1)
    cq = p_ref[:, :MLA_Q_LORA]
    cqn = cq * lax.rsqrt(jnp.mean(cq * cq, axis=-1, keepdims=True) + EPS) * gq_ref[...]
    qq = jnp.dot(cqn.astype(BF16), wq_ref[...], preferred_element_type=F32)
    q_ref[...] = ((qq[:, :HL] * cos + qq[:, HL:] * sin) * (MLA_QK ** -0.5)).astype(q_ref.dtype)
    ck = p_ref[:, MLA_Q_LORA:]
    lane = lax.broadcasted_iota(jnp.int32, ck.shape, 1)
    is_kv = lane < MLA_KV_LORA
    ms = jnp.sum(jnp.where(is_kv, ck * ck, 0.0), axis=-1, keepdims=True) * (1.0 / MLA_KV_LORA)
    ckn = jnp.where(is_kv, ck * lax.rsqrt(ms + EPS) * gkv_ref[...], ck)
    kk = jnp.dot(ckn.astype(BF16), wk_ref[...], preferred_element_type=F32)
    k_ref[...] = (kk[:, :HL] * cos + kk[:, HL:2 * HL] * sin).astype(k_ref.dtype)
    v_ref[...] = kk[:, 2 * HL:].astype(v_ref.dtype)


def mla_proj(p_mla, gq, gkv, wq2, wk2, cos, sin, tm):
    B, S, C = p_mla.shape
    tm = min(tm, S)
    HL = MLA_HEADS * LANES
    gkv = jnp.pad(gkv, ((0, 0), (0, MLA_KVIN - MLA_KV_LORA)))
    row = lambda n: pl.BlockSpec((None, tm, n), lambda b, i: (b, i, 0))
    full = lambda a: pl.BlockSpec(a.shape, lambda b, i: (0, 0))
    tab = pl.BlockSpec((tm, LANES), lambda b, i: (i, 0))
    return pl.pallas_call(
        _mla_proj_kernel,
        grid=(B, S // tm),
        in_specs=[row(C), full(gq), full(gkv), full(wq2), full(wk2), tab, tab],
        out_specs=[row(HL), row(HL), row(MLA_HEADS * MLA_V)],
        out_shape=[jax.ShapeDtypeStruct((B, S, HL), BF16), jax.ShapeDtypeStruct((B, S, HL), BF16),
                   jax.ShapeDtypeStruct((B, S, MLA_HEADS * MLA_V), BF16)],
        compiler_params=_cparams("parallel", "parallel"),
        name="mla_proj",
    )(p_mla, gq, gkv, wq2, wk2, cos, sin)


def _mla_attn_kernel(q_ref, k_ref, v_ref, o_ref, *, tk):
    tq = q_ref.shape[0]
    Tk = k_ref.shape[0]
    n_full, rem = Tk // tk, Tk % tk
    outs = []
    for h in range(2):
        hs = slice(h * LANES, (h + 1) * LANES)
        q = q_ref[:, hs]

        def step(carry, start, size):
            m, l, acc = carry
            s = lax.dot_general(q, k_ref[pl.ds(start, size), hs], (((1,), (1,)), ((), ())),
                                preferred_element_type=F32)
            m_new = jnp.maximum(m, jnp.max(s, axis=-1, keepdims=True))
            alpha = jnp.exp(m - m_new)
            p = jnp.exp(s - m_new)
            l = l * alpha + jnp.sum(p, axis=-1, keepdims=True)
            acc = acc * alpha + jnp.dot(p.astype(BF16), v_ref[pl.ds(start, size), :], preferred_element_type=F32)
            return m_new, l, acc

        carry = (jnp.full((tq, 1), -jnp.inf, F32), jnp.zeros((tq, 1), F32), jnp.zeros((tq, LANES), F32))
        if n_full:
            carry = lax.fori_loop(0, n_full, lambda c, cr: step(cr, pl.multiple_of(c * tk, tk), tk), carry)
        if rem:
            carry = step(carry, n_full * tk, rem)
        _, l, acc = carry
        outs.append(acc / l)
    lane = lax.broadcasted_iota(jnp.int32, (tq, LANES), 1)
    o_ref[...] = jnp.where(lane < MLA_V, outs[0], outs[1]).astype(o_ref.dtype)


def mla_attn(q, k, v, tq, tk):
    B, S, _ = q.shape
    Tk = k.shape[1]
    tq = min(tq, S)
    return pl.pallas_call(
        functools.partial(_mla_attn_kernel, tk=tk),
        grid=(B, MLA_HEADS // 2, S // tq),
        in_specs=[pl.BlockSpec((None, tq, 2 * LANES), lambda b, h, i: (b, i, h)),
                  pl.BlockSpec((None, Tk, 2 * LANES), lambda b, h, i: (b, 0, h)),
                  pl.BlockSpec((None, Tk, LANES), lambda b, h, i: (b, 0, h))],
        out_specs=pl.BlockSpec((None, tq, LANES), lambda b, h, i: (b, i, h)),
        out_shape=jax.ShapeDtypeStruct((B, S, MLA_HEADS * MLA_V), BF16),
        compiler_params=_cparams("parallel", "parallel", "arbitrary"),
        name="mla_attn",
    )(q, k, v)


def _route(logits, rb):
    lane = lax.broadcasted_iota(jnp.int32, logits.shape, 1)
    neg = -jnp.inf
    scores = jax.nn.sigmoid(logits)
    sel = scores + rb

    def top2(masked):
        m1 = jnp.max(masked, axis=-1, keepdims=True)
        i1 = jnp.min(jnp.where(masked == m1, lane, LANES), axis=-1, keepdims=True)
        rest = jnp.where(lane == i1, neg, masked)
        m2 = jnp.max(rest, axis=-1, keepdims=True)
        i2 = jnp.min(jnp.where(rest == m2, lane, LANES), axis=-1, keepdims=True)
        return m1, i1, m2, i2

    best = None
    for gi in range(N_GROUPS):
        in_g = jnp.logical_and(lane >= gi * EXPERTS_PER_GROUP, lane < (gi + 1) * EXPERTS_PER_GROUP)
        m1, _, m2, _ = top2(jnp.where(in_g, sel, neg))
        gs = m1 + m2
        if best is None:
            best, grp = gs, jnp.zeros_like(gs, dtype=jnp.int32)
        else:
            better = gs > best
            grp = jnp.where(better, gi, grp)
            best = jnp.where(better, gs, best)
    lo = grp * EXPERTS_PER_GROUP
    in_grp = jnp.logical_and(lane >= lo, lane < lo + EXPERTS_PER_GROUP)
    _, i1, _, i2 = top2(jnp.where(in_grp, sel, neg))
    picked = jnp.where(jnp.logical_or(lane == i1, lane == i2), scores, 0.0)
    return picked / jnp.sum(picked, axis=-1, keepdims=True)


def _merge_kernel(hyv_ref, of_ref, ob_ref, pg_ref, at_ref, x_ref, gt1_ref, sc2_ref, sh2_ref, gng_ref, n2g_ref,
                  whb_ref, wgd_ref, wml_ref, wo_ref, rw_ref, rb_ref, xo_ref, h2_ref, gate_ref):
    nv = GDN_HEADS * GDN_DV
    D = x_ref.shape[1]
    o = of_ref[...] + ob_ref[...]
    z = pg_ref[:, :nv]
    ys = []
    for h in range(GDN_HEADS):
        sl = slice(h * GDN_DV, (h + 1) * GDN_DV)
        oh = o[:, sl]
        on = oh * lax.rsqrt(jnp.mean(oh * oh, axis=-1, keepdims=True) + EPS) * gng_ref[...]
        ys.append(on * _silu(z[:, sl]))
    y_gdn = jnp.dot(jnp.concatenate(ys, axis=1).astype(BF16), wgd_ref[...], preferred_element_type=F32)
    y_hy = jnp.dot(hyv_ref[...].astype(BF16), whb_ref[...], preferred_element_type=F32)
    y_mla = jnp.dot(at_ref[...], wml_ref[...], preferred_element_type=F32)
    merged = (jax.nn.sigmoid(pg_ref[:, nv:nv + D]) * y_hy
              + jax.nn.sigmoid(pg_ref[:, nv + D:nv + 2 * D]) * y_gdn
              + jax.nn.sigmoid(pg_ref[:, nv + 2 * D:]) * y_mla)
    mix = jnp.dot(merged.astype(BF16), wo_ref[...], preferred_element_type=F32)
    xn = x_ref[...] + gt1_ref[...] * mix
    xo_ref[...] = xn
    y2 = xn * lax.rsqrt(jnp.mean(xn * xn, axis=-1, keepdims=True) + EPS) * n2g_ref[...]
    h2 = y2 * (1.0 + sc2_ref[...]) + sh2_ref[...]
    h2_ref[...] = h2.astype(h2_ref.dtype)
    logits = jnp.dot(h2, rw_ref[...], precision=HI, preferred_element_type=F32)
    gate_ref[...] = _route(logits, rb_ref[...])


def merge_out(hyv, o_f, o_b, pg, attn, x, gt1, sc2, sh2, gdn_norm_g, norm2_g, w_hy, w_gdn, w_mla, w_out,
              router_w, router_b, tm):
    B, S, D = x.shape
    tm = min(tm, S)
    row = lambda n: pl.BlockSpec((None, tm, n), lambda b, i: (b, i, 0))
    full = lambda a: pl.BlockSpec(a.shape, lambda b, i: (0, 0))
    nv = GDN_HEADS * GDN_DV
    return pl.pallas_call(
        _merge_kernel,
        grid=(B, S // tm),
        in_specs=[row(HY_W), row(nv), row(nv), row(pg.shape[2]), row(MLA_HEADS * MLA_V), row(D),
                  _mod_spec(gt1, D), _mod_spec(sc2, D), _mod_spec(sh2, D),
                  full(gdn_norm_g), full(norm2_g), full(w_hy), full(w_gdn), full(w_mla), full(w_out),
                  full(router_w), full(router_b)],
        out_specs=[row(D), row(D), row(LANES)],
        out_shape=[jax.ShapeDtypeStruct((B, S, D), F32), jax.ShapeDtypeStruct((B, S, D), BF16),
                   jax.ShapeDtypeStruct((B, S, LANES), F32)],
        compiler_params=_cparams("parallel", "parallel"),
        name="merge_out",
    )(hyv, o_f, o_b, pg, attn, x, gt1, sc2, sh2, gdn_norm_g, norm2_g, w_hy, w_gdn, w_mla, w_out,
      router_w, router_b)


def _moe_kernel(h_ref, gate_ref, w1_ref, w3_ref, w2_ref, x_ref, gt2_ref, fg_ref, o_ref, acc_ref, *, final_norm):
    e = pl.program_id(1)

    @pl.when(e == 0)
    def _():
        acc_ref[...] = jnp.zeros_like(acc_ref)

    h = h_ref[...]
    he = _silu(jnp.dot(h, w1_ref[...], preferred_element_type=F32)) * jnp.dot(h, w3_ref[...], preferred_element_type=F32)
    gate = gate_ref[...]
    lane = lax.broadcasted_iota(jnp.int32, gate.shape, 1)
    ge = jnp.sum(jnp.where(lane == e, gate, 0.0), axis=-1, keepdims=True)
    acc_ref[...] += ge * jnp.dot(he.astype(BF16), w2_ref[...], preferred_element_type=F32)

    @pl.when(e == pl.num_programs(1) - 1)
    def _():
        xn = x_ref[...] + gt2_ref[...] * acc_ref[...]
        if final_norm:
            xn = xn * lax.rsqrt(jnp.mean(xn * xn, axis=-1, keepdims=True) + EPS) * fg_ref[...]
        o_ref[...] = xn


def moe(h2, gate, w1, w3, w2, x, gt2, final_g, S, tm, final_norm):
    T, D = x.shape
    E, _, FF = w1.shape
    tm = min(tm, S)
    per_b = S // tm
    if gt2.shape[0] == 1:
        gt_spec = pl.BlockSpec((None, 1, D), lambda i, e: (0, 0, 0))
    else:
        gt_spec = pl.BlockSpec((None, 1, D), lambda i, e: (i // per_b, 0, 0))
    return pl.pallas_call(
        functools.partial(_moe_kernel, final_norm=final_norm),
        grid=(T // tm, E),
        in_specs=[pl.BlockSpec((tm, D), lambda i, e: (i, 0)),
                  pl.BlockSpec((tm, LANES), lambda i, e: (i, 0)),
                  pl.BlockSpec((None, D, FF), lambda i, e: (e, 0, 0)),
                  pl.BlockSpec((None, D, FF), lambda i, e: (e, 0, 0)),
                  pl.BlockSpec((None, FF, D), lambda i, e: (e, 0, 0)),
                  pl.BlockSpec((tm, D), lambda i, e: (i, 0)),
                  gt_spec,
                  pl.BlockSpec((1, D), lambda i, e: (0, 0))],
        out_specs=pl.BlockSpec((tm, D), lambda i, e: (i, 0)),
        out_shape=jax.ShapeDtypeStruct((T, D), F32),
        scratch_shapes=[pltpu.VMEM((tm, D), F32)],
        compiler_params=_cparams("parallel", "arbitrary"),
        name="moe",
    )(h2, gate, w1, w3, w2, x, gt2, final_g)


IN_SIZES = (3 * HY_W, GDN_QKV, GDN_HEADS * GDN_DV, 2 * GDN_HEADS, 2 * GDN_HEADS, MLA_Q_LORA, MLA_KV_LORA, MLA_ROPE)


def _split_w_in(w_in):
    D = w_in.shape[0]
    parts, off = [], 0
    for n in IN_SIZES:
        parts.append(w_in[:, off:off + n])
        off += n
    hy, qkv, z, a, b, cq, ckv, kr = parts
    gate = w_in[:, off:]
    zeros = lambda n: jnp.zeros((D, n), w_in.dtype)
    w_hy = hy
    w_gdn = jnp.concatenate([qkv, a, b, zeros(LANES - 4 * GDN_HEADS)], axis=1)
    w_mla = jnp.concatenate([cq, ckv, kr, zeros(LANES - MLA_ROPE)], axis=1)
    w_gate = jnp.concatenate([z, gate], axis=1)
    return [w.astype(BF16) for w in (w_hy, w_gdn, w_mla, w_gate)]


def _layer(x, cx, mod, mod_c, lw, tabs, router_w, router_b, final_g, update_ctx, last):
    B, S, D = x.shape
    Lc = cx.shape[1]
    sh1, sc1, gt1, sh2, sc2, gt2 = [m[:, None, :] for m in jnp.split(mod, 6, axis=-1)]
    csh1, csc1, cgt1, csh2, csc2, cgt2 = [m[:, None, :] for m in jnp.split(mod_c, 6, axis=-1)]
    n1g = lw['norm1_g'][None, :]
    w_hy, w_gdn, w_mla, w_gate = _split_w_in(lw['w_in'])

    def project(xx, sc, sh, tm):
        return [in_proj(xx, n1g, sc, sh, w, tm) for w in (w_hy, w_gdn, w_mla, w_gate)]

    p_hy, p_gdn, p_mla, p_gate = project(x, sc1, sh1, 512)
    c_hy, c_gdn, c_mla, c_gate = project(cx, csc1, csh1, 256)

    feats_c = gdn_pre(c_gdn, lw['gdn_conv_w'], lw['gdn_a_log'], lw['gdn_dt_bias'], 256)
    feats_l = gdn_pre(p_gdn, lw['gdn_conv_w'], lw['gdn_a_log'], lw['gdn_dt_bias'], 512)
    o_f, o_b = gdn_scan(*[jnp.concatenate([c, l], axis=1) for c, l in zip(feats_c, feats_l)], Lc // GDN_CHUNK)

    wq2, wk2 = mla_weights(lw['mla_w_uq'], lw['mla_w_ukv'])
    gq, gkv = lw['mla_q_norm_g'][None, :], lw['mla_kv_norm_g'][None, :]
    cos_l, sin_l, cos_c, sin_c = tabs
    q_l, k_l, v_l = mla_proj(p_mla, gq, gkv, wq2, wk2, cos_l, sin_l, 512)
    q_c, k_c, v_c = mla_proj(c_mla, gq, gkv, wq2, wk2, cos_c, sin_c, 256)
    attn_l = mla_attn(q_l, jnp.concatenate([k_l, k_c], axis=1), jnp.concatenate([v_l, v_c], axis=1), 512, 1024)

    def hyena(p, L):
        filt = hy_filter(L, lw['hy_f_w1'], lw['hy_f_b1'], lw['hy_f_w2'], lw['hy_f_b2'], lw['hy_f_w3'],
                         lw['hy_f_freq'], lw['hy_decay'])
        x0, u = hy_pre(p, lw['hy_conv_w'], lw['hy_conv_b'][None, :], 512)
        return hyena_long_conv(x0, u, filt, lw['hy_bias'])

    hyv_l = hyena(p_hy, S)

    wb = lambda name: lw[name].astype(BF16)
    rw = jnp.pad(router_w, ((0, 0), (0, LANES - N_EXPERTS)))
    rb = jnp.pad(router_b[None, :], ((0, 0), (0, LANES - N_EXPERTS)))
    n2g, gng = lw['norm2_g'][None, :], lw['gdn_norm_g'][None, :]
    w1, w3, w2 = wb('moe_w1'), wb('moe_w3'), wb('moe_w2')

    def finish(xx, hyv, of_, ob_, pg, attn, gt1_, sc2_, sh2_, gt2_, tm, tm_moe, fin):
        Bx, Sx, _ = xx.shape
        xn, h2, gate = merge_out(hyv, of_, ob_, pg, attn, xx, gt1_, sc2_, sh2_, gng, n2g, wb('hy_out'), wb('gdn_out'),
                                 wb('mla_out'), wb('w_out'), rw, rb, tm)
        out = moe(h2.reshape(Bx * Sx, D), gate.reshape(Bx * Sx, LANES), w1, w3, w2, xn.reshape(Bx * Sx, D),
                  gt2_, final_g, Sx, tm_moe, fin)
        return out.reshape(Bx, Sx, D)

    x_new = finish(x, hyv_l, o_f[:, Lc:], o_b[:, Lc:], p_gate, attn_l, gt1, sc2, sh2, gt2, 256, 1024, last)
    if update_ctx:
        hyv_c = hyena(c_hy, Lc)
        attn_c = mla_attn(q_c, k_c, v_c, 256, 1024)
        cx = finish(cx, hyv_c, o_f[:, :Lc], o_b[:, :Lc], c_gate, attn_c, cgt1, csc2, csh2, cgt2, 256, 256, False)
    return x_new, cx


def kernel(x, c, ctx, c_ctx, w_ada, b_ada, norm1_g, norm2_g, w_in, hy_conv_w, hy_conv_b, hy_f_w1, hy_f_b1, hy_f_w2, hy_f_b2, hy_f_w3, hy_f_freq, hy_decay, hy_bias, hy_out, gdn_conv_w, gdn_a_log, gdn_dt_bias, gdn_norm_g, gdn_out, mla_q_norm_g, mla_w_uq, mla_kv_norm_g, mla_w_ukv, mla_out, w_out, moe_w1, moe_w3, moe_w2, router_w, router_b, final_norm_g):
    per_layer = dict(norm1_g=norm1_g, norm2_g=norm2_g, w_in=w_in, hy_conv_w=hy_conv_w, hy_conv_b=hy_conv_b,
                     hy_f_w1=hy_f_w1, hy_f_b1=hy_f_b1, hy_f_w2=hy_f_w2, hy_f_b2=hy_f_b2, hy_f_w3=hy_f_w3,
                     hy_f_freq=hy_f_freq, hy_decay=hy_decay, hy_bias=hy_bias, hy_out=hy_out,
                     gdn_conv_w=gdn_conv_w, gdn_a_log=gdn_a_log, gdn_dt_bias=gdn_dt_bias, gdn_norm_g=gdn_norm_g,
                     gdn_out=gdn_out, mla_q_norm_g=mla_q_norm_g, mla_w_uq=mla_w_uq, mla_kv_norm_g=mla_kv_norm_g,
                     mla_w_ukv=mla_w_ukv, mla_out=mla_out, w_out=w_out, moe_w1=moe_w1, moe_w3=moe_w3, moe_w2=moe_w2)
    B, S, D = x.shape
    Lc = ctx.shape[1]
    depth = w_ada.shape[0]
    rows = S // GRID_W
    row = jnp.repeat(jnp.arange(rows, dtype=jnp.int32), GRID_W)
    col = jnp.tile(jnp.arange(GRID_W, dtype=jnp.int32), rows)
    cos_l, sin_l = rope_tables(row, col)
    zero = jnp.zeros((Lc,), jnp.int32)
    cos_c, sin_c = rope_tables(zero, zero)
    tabs = (cos_l, sin_l, cos_c, sin_c)
    cc = jnp.concatenate([c, c_ctx[None, :], jnp.zeros((2 * SUBLANES - B - 1, D), F32)], axis=0)
    final_g = final_norm_g[None, :]
    cx = ctx
    for l in range(depth):
        lw = {k: v[l] for k, v in per_layer.items()}
        mods = ada_mod(cc, w_ada[l], b_ada[l][None, :])
        x, cx = _layer(x, cx, mods[:B], mods[B:B + 1], lw, tabs, router_w, router_b, final_g,
                       l < depth - 1, l == depth - 1)
    return x
```

```python
import functools
import math

import jax
import jax.numpy as jnp
from jax import lax
from jax.experimental import pallas as pl
from jax.experimental.pallas import tpu as pltpu

F32 = jnp.float32
BF16 = jnp.bfloat16
HI = lax.Precision.HIGHEST
EPS = 1e-6

GRID_W = 64
HY_W = 512
HY_EMB = 33
HY_BANDS = (HY_EMB - 1) // 2
HY_MOD_SHIFT = 0.05
GDN_HEADS = 4
GDN_DK = 128
GDN_DV = 128
GDN_CHUNK = 64
MLA_HEADS = 8
MLA_NOPE = 64
MLA_ROPE = 32
MLA_V = 64
MLA_Q_LORA = 768
MLA_KV_LORA = 256
ROPE_THETA = 10000.0
N_EXPERTS = 16
N_GROUPS = 4
EXPERTS_PER_GROUP = N_EXPERTS // N_GROUPS
EXPERT_FF = 512
LANES = 128
SUBLANES = 8
VMEM_LIMIT = 56 * 1024 * 1024


def _cparams(*sem):
    return pltpu.CompilerParams(dimension_semantics=sem, vmem_limit_bytes=VMEM_LIMIT)


def _silu(x):
    return x * jax.nn.sigmoid(x)


def _ada_kernel(c_ref, w_ref, b_ref, o_ref):
    a = _silu(c_ref[...])
    o_ref[...] = jnp.dot(a, w_ref[...], precision=HI, preferred_element_type=F32) + b_ref[...]


def ada_mod(cc, w, b):
    R, D = cc.shape
    N = w.shape[1]
    tn = 1536
    return pl.pallas_call(
        _ada_kernel,
        grid=(N // tn,),
        in_specs=[pl.BlockSpec((R, D), lambda j: (0, 0)),
                  pl.BlockSpec((D, tn), lambda j: (0, j)),
                  pl.BlockSpec((1, tn), lambda j: (0, j))],
        out_specs=pl.BlockSpec((R, tn), lambda j: (0, j)),
        out_shape=jax.ShapeDtypeStruct((R, N), F32),
        compiler_params=_cparams("parallel"),
        name="ada_mod",
    )(cc, w, b)


def _inproj_kernel(x_ref, g_ref, sc_ref, sh_ref, w_ref, o_ref):
    x = x_ref[...]
    y = x * lax.rsqrt(jnp.mean(x * x, axis=-1, keepdims=True) + EPS) * g_ref[...]
    h = y * (1.0 + sc_ref[...]) + sh_ref[...]
    o_ref[...] = jnp.dot(h.astype(BF16), w_ref[...], preferred_element_type=F32).astype(o_ref.dtype)


def _mod_spec(m, D):
    if m.shape[0] == 1:
        return pl.BlockSpec((None, 1, D), lambda b, i: (0, 0, 0))
    return pl.BlockSpec((None, 1, D), lambda b, i: (b, 0, 0))


def in_proj(x, g, sc, sh, w, tm):
    B, S, D = x.shape
    N = w.shape[1]
    tm = min(tm, S)
    return pl.pallas_call(
        _inproj_kernel,
        grid=(B, S // tm),
        in_specs=[pl.BlockSpec((None, tm, D), lambda b, i: (b, i, 0)),
                  pl.BlockSpec((1, D), lambda b, i: (0, 0)),
                  _mod_spec(sc, D), _mod_spec(sh, D),
                  pl.BlockSpec((D, N), lambda b, i: (0, 0))],
        out_specs=pl.BlockSpec((None, tm, N), lambda b, i: (b, i, 0)),
        out_shape=jax.ShapeDtypeStruct((B, S, N), F32),
        compiler_params=_cparams("parallel", "parallel"),
        name="in_proj",
    )(x, g, sc, sh, w)


def _halo_specs(tm, S, C, col_block=0):
    nb8 = tm // SUBLANES
    last8 = S // SUBLANES - 1
    main = pl.BlockSpec((None, tm, C), lambda b, i: (b, i, col_block))
    prev = pl.BlockSpec((None, SUBLANES, C), lambda b, i: (b, jnp.maximum(i * nb8 - 1, 0), col_block))
    nxt = pl.BlockSpec((None, SUBLANES, C), lambda b, i: (b, jnp.minimum((i + 1) * nb8, last8), col_block))
    return main, prev, nxt


def _conv3(x, prev8, next8, w, first, last):
    tm = x.shape[0]
    row = lax.broadcasted_iota(jnp.int32, x.shape, 0)
    p_row = jnp.where(first, 0.0, prev8[SUBLANES - 1:SUBLANES, :])
    n_row = jnp.where(last, 0.0, next8[0:1, :])
    x_prev = jnp.where(row == 0, p_row, pltpu.roll(x, 1, 0))
    x_next = jnp.where(row == tm - 1, n_row, pltpu.roll(x, tm - 1, 0))
    return x_prev * w[0:1, :] + x * w[1:2, :] + x_next * w[2:3, :]


def _hy_pre_kernel(p_ref, pp_ref, pn_ref, w_ref, b_ref, x0_ref, u_ref):
    i = pl.program_id(1)
    y = _conv3(p_ref[...], pp_ref[...], pn_ref[...], w_ref[...], i == 0, i == pl.num_programs(1) - 1)
    y = y + b_ref[...]
    x0_ref[...] = y[:, :HY_W]
    u_ref[...] = y[:, HY_W:2 * HY_W] * y[:, 2 * HY_W:]


def hy_pre(p_hy, conv_w, conv_b, tm):
    B, S, C = p_hy.shape
    tm = min(tm, S)
    main, prev, nxt = _halo_specs(tm, S, C)
    o_spec = pl.BlockSpec((None, tm, HY_W), lambda b, i: (b, i, 0))
    return pl.pallas_call(
        _hy_pre_kernel,
        grid=(B, S // tm),
        in_specs=[main, prev, nxt,
                  pl.BlockSpec((3, C), lambda b, i: (0, 0)),
                  pl.BlockSpec((1, C), lambda b, i: (0, 0))],
        out_specs=[o_spec, o_spec],
        out_shape=[jax.ShapeDtypeStruct((B, S, HY_W), F32)] * 2,
        compiler_params=_cparams("parallel", "parallel"),
        name="hy_pre",
    )(p_hy, p_hy, p_hy, conv_w, conv_b)


def _hy_filter_kernel(z_ref, w1_ref, b1_ref, w2_ref, b2_ref, w3_ref, fq_ref, dc_ref, o_ref):
    fq = fq_ref[...]
    h = jnp.sin(fq * (jnp.dot(z_ref[...], w1_ref[...], precision=HI, preferred_element_type=F32) + b1_ref[...]))
    h = jnp.sin(fq * (jnp.dot(h, w2_ref[...], precision=HI, preferred_element_type=F32) + b2_ref[...]))
    h = jnp.dot(h, w3_ref[...], precision=HI, preferred_element_type=F32)
    window = jnp.exp(-z_ref[:, 0:1] * jnp.abs(dc_ref[...])) + HY_MOD_SHIFT
    o_ref[...] = h * window


def hy_filter(L, w1, b1, w2, b2, w3, freq, decay):
    t = jnp.linspace(0.0, 1.0, L, dtype=F32)[:, None]
    w = (2.0 * math.pi / L) * jnp.arange(L, dtype=F32)[:, None]
    f = jnp.linspace(1e-4, HY_BANDS - 1, HY_BANDS, dtype=F32)[None, :]
    z = jnp.concatenate([t, jnp.cos(f * w), -jnp.sin(f * w)], axis=-1)
    emb_pad = LANES - HY_EMB
    z = jnp.pad(z, ((0, 0), (0, emb_pad)))
    w1 = jnp.pad(w1, ((0, emb_pad), (0, 0)))
    hid = w1.shape[1]
    tl = min(L, 512)
    full = lambda a: pl.BlockSpec(a.shape, lambda i: (0, 0))
    b1, b2, freq, decay = b1[None, :], b2[None, :], freq[None, :], decay[None, :]
    return pl.pallas_call(
        _hy_filter_kernel,
        grid=(L // tl,),
        in_specs=[pl.BlockSpec((tl, LANES), lambda i: (i, 0)),
                  full(w1), full(b1), full(w2), full(b2), full(w3), full(freq), full(decay)],
        out_specs=pl.BlockSpec((tl, 2 * HY_W), lambda i: (i, 0)),
        out_shape=jax.ShapeDtypeStruct((L, 2 * HY_W), F32),
        compiler_params=_cparams("parallel"),
        name="hy_filter",
    )(z, w1, b1, w2, b2, w3, freq, decay)


def _dft_tables(N1, N2):
    N = N1 * N2
    two_pi = 2.0 * math.pi

    def cs(num, den):
        ang = (two_pi / den) * (num % den).astype(F32)
        return jnp.cos(ang), jnp.sin(ang)

    a1 = jnp.arange(N1, dtype=jnp.int32)
    a2 = jnp.arange(N2, dtype=jnp.int32)
    c1, s1 = cs(a1[:, None] * a1[None, :], N1)
    c2, s2 = cs(a2[:, None] * a2[None, :], N2)
    ct, st = cs(a2[:, None] * a1[None, :], N)

    def stack(re, im):
        return jnp.concatenate([jnp.concatenate([re, -im], axis=-1),
                                jnp.concatenate([im, re], axis=-1)], axis=-2)

    tr = ct[:, :, None] * c1[None] - st[:, :, None] * s1[None]
    ti = -(ct[:, :, None] * s1[None] + st[:, :, None] * c1[None])
    h = N1 // 2
    m1_data = stack(tr[:, :, :h], ti[:, :, :h])
    m1_real = jnp.concatenate([tr, ti], axis=-2)
    m2 = stack(c2, -s2)
    ctk, stk = ct.T, st.T
    gr = ctk[:, :, None] * c2.T[None] - stk[:, :, None] * s2.T[None]
    gi = ctk[:, :, None] * s2.T[None] + stk[:, :, None] * c2.T[None]
    m2inv = stack(gr, gi)
    er, ei = c1.T[:h] / N, s1.T[:h] / N
    m3 = stack(er, ei)
    return m1_data, m1_real, m2, m2inv, m3


def _hy_pass_a_kernel(u_ref, m_ref, o_ref):
    n1 = o_ref.shape[1]
    for j in range(SUBLANES):
        xj = jnp.concatenate([u_ref[0, :, j, :], u_ref[1, :, j, :]], axis=0)
        a = jnp.dot(m_ref[j], xj, precision=HI, preferred_element_type=F32)
        o_ref[0, :, j, :] = a[:n1]
        o_ref[1, :, j, :] = a[n1:]


def _hy_pass_a(u4, m1, n_pairs):
    _, h, N2, C = u4.shape
    N1 = 2 * h
    return pl.pallas_call(
        _hy_pass_a_kernel,
        grid=(n_pairs, N2 // SUBLANES),
        in_specs=[pl.BlockSpec((2, h, SUBLANES, C), lambda p, j: (p, 0, j, 0)),
                  pl.BlockSpec((SUBLANES, 2 * N1, N1), lambda p, j: (j, 0, 0))],
        out_specs=pl.BlockSpec((2, N1, SUBLANES, C), lambda p, j: (0, 0, j, p)),
        out_shape=jax.ShapeDtypeStruct((2, N1, N2, n_pairs * C), F32),
        compiler_params=_cparams("parallel", "parallel"),
        name="hy_pass_a",
    )(u4, m1)


def _hy_spec_kernel(a_ref, m2_ref, o_ref):
    n2 = a_ref.shape[2]
    for j in range(SUBLANES):
        a = jnp.concatenate([a_ref[0, j], a_ref[1, j]], axis=0)
        x = jnp.dot(m2_ref[...], a, precision=HI, preferred_element_type=F32)
        o_ref[0, j] = x[:n2]
        o_ref[1, j] = x[n2:]


def _hy_spectrum(a, m2):
    _, N1, N2, C = a.shape
    spec = pl.BlockSpec((2, SUBLANES, N2, C), lambda k: (0, k, 0, 0))
    return pl.pallas_call(
        _hy_spec_kernel,
        grid=(N1 // SUBLANES,),
        in_specs=[spec, pl.BlockSpec((2 * N2, 2 * N2), lambda k: (0, 0))],
        out_specs=spec,
        out_shape=jax.ShapeDtypeStruct(a.shape, F32),
        compiler_params=_cparams("parallel"),
        name="hy_spectrum",
    )(a, m2)


def _hy_pass_b_kernel(a_ref, k_ref, m2_ref, mi_ref, o_ref):
    n2 = a_ref.shape[2]
    for j in range(SUBLANES):
        a = jnp.concatenate([a_ref[0, j], a_ref[1, j]], axis=0)
        x = jnp.dot(m2_ref[...], a, precision=HI, preferred_element_type=F32)
        xr, xi = x[:n2], x[n2:]
        kr, ki = k_ref[0, j], k_ref[1, j]
        y = jnp.concatenate([xr * kr - xi * ki, xr * ki + xi * kr], axis=0)
        b = jnp.dot(mi_ref[j], y, precision=HI, preferred_element_type=F32)
        o_ref[0, :, j, :] = b[:n2]
        o_ref[1, :, j, :] = b[n2:]


def _hy_pass_b(a, kf, m2, m2inv, n_pairs):
    _, N1, N2, PC = a.shape
    C = PC // n_pairs
    return pl.pallas_call(
        _hy_pass_b_kernel,
        grid=(n_pairs, N1 // SUBLANES),
        in_specs=[pl.BlockSpec((2, SUBLANES, N2, C), lambda p, k: (0, k, 0, p)),
                  pl.BlockSpec((2, SUBLANES, N2, C), lambda p, k: (0, k, 0, 0)),
                  pl.BlockSpec((2 * N2, 2 * N2), lambda p, k: (0, 0)),
                  pl.BlockSpec((SUBLANES, 2 * N2, 2 * N2), lambda p, k: (k, 0, 0))],
        out_specs=pl.BlockSpec((2, N2, SUBLANES, C), lambda p, k: (0, 0, k, p)),
        out_shape=jax.ShapeDtypeStruct((2, N2, N1, PC), F32),
        compiler_params=_cparams("parallel", "parallel"),
        name="hy_pass_b",
    )(a, kf, m2, m2inv)


def _hy_pass_c_kernel(b_ref, u_ref, x0_ref, bias_ref, m3_ref, o_ref):
    h = o_ref.shape[1]
    bias = bias_ref[...]
    for j in range(SUBLANES):
        bb = jnp.concatenate([b_ref[0, j], b_ref[1, j]], axis=0)
        y = jnp.dot(m3_ref[...], bb, precision=HI, preferred_element_type=F32)
        for r in range(2):
            o_ref[r, :, j, :] = x0_ref[r, :, j, :] * (y[r * h:(r + 1) * h] + bias * u_ref[r, :, j, :])


def _hy_pass_c(bq, u4, x04, bias, m3, n_pairs):
    _, N2, N1, PC = bq.shape
    C = PC // n_pairs
    h = N1 // 2
    io = pl.BlockSpec((2, h, SUBLANES, C), lambda p, j: (p, 0, j, 0))
    return pl.pallas_call(
        _hy_pass_c_kernel,
        grid=(n_pairs, N2 // SUBLANES),
        in_specs=[pl.BlockSpec((2, SUBLANES, N1, C), lambda p, j: (0, j, 0, p)),
                  io, io,
                  pl.BlockSpec((1, C), lambda p, j: (0, 0)),
                  pl.BlockSpec((N1, 2 * N1), lambda p, j: (0, 0))],
        out_specs=io,
        out_shape=jax.ShapeDtypeStruct(u4.shape, F32),
        compiler_params=_cparams("parallel", "parallel"),
        name="hy_pass_c",
    )(bq, u4, x04, bias, m3)


def hyena_long_conv(x0, u, filt, bias):
    B, L, C = u.shape
    N2 = min(128, L // 32)
    N1 = 2 * L // N2
    n_pairs = B // 2
    m1_data, m1_real, m2, m2inv, m3 = _dft_tables(N1, N2)
    kbuf = jnp.concatenate([filt[:, :C], jnp.zeros((1, C), F32), filt[:0:-1, C:]], axis=0)
    kf = _hy_spectrum(_hy_pass_a(kbuf.reshape(2, N1 // 2, N2, C), m1_real, 1), m2)
    u4 = u.reshape(B, N1 // 2, N2, C)
    a = _hy_pass_a(u4, m1_data, n_pairs)
    bq = _hy_pass_b(a, kf, m2, m2inv, n_pairs)
    y = _hy_pass_c(bq, u4, x0.reshape(u4.shape), bias[None, :], m3, n_pairs)
    return y.reshape(B, L, C)


GDN_QKV = GDN_HEADS * (2 * GDN_DK + GDN_DV)


def _gdn_pre_kernel(p_ref, pp_ref, pn_ref, w_ref, alog_ref, dtb_ref, q_ref, k_ref, v_ref, gb_ref):
    i = pl.program_id(1)
    C = GDN_QKV
    y = _conv3(p_ref[:, :C], pp_ref[:, :C], pn_ref[:, :C], w_ref[...], i == 0, i == pl.num_programs(1) - 1)
    y = _silu(y)
    nk = GDN_HEADS * GDN_DK
    for h in range(GDN_HEADS):
        sl = slice(h * GDN_DK, (h + 1) * GDN_DK)
        qh = y[:, sl]
        kh = y[:, nk + h * GDN_DK:nk + (h + 1) * GDN_DK]
        q_ref[:, sl] = qh * (lax.rsqrt(jnp.sum(qh * qh, axis=-1, keepdims=True) + EPS) * (GDN_DK ** -0.5))
        k_ref[:, sl] = kh * lax.rsqrt(jnp.sum(kh * kh, axis=-1, keepdims=True) + EPS)
    v_ref[...] = y[:, 2 * nk:]
    s = p_ref[:, C:]
    lane = lax.broadcasted_iota(jnp.int32, s.shape, 1)
    xa = s + dtb_ref[...]
    softplus = jnp.maximum(xa, 0.0) + jnp.log1p(jnp.exp(-jnp.abs(xa)))
    g = -jnp.exp(alog_ref[...]) * softplus
    gb_ref[...] = jnp.where(lane < 2 * GDN_HEADS, g, jnp.where(lane < 4 * GDN_HEADS, jax.nn.sigmoid(s), 0.0))


def gdn_pre(p_gdn, conv_w, a_log, dt_bias, tm):
    B, S, C = p_gdn.shape
    tm = min(tm, S)
    main, prev, nxt = _halo_specs(tm, S, C)
    pad = LANES - 2 * GDN_HEADS
    alog = jnp.pad(a_log.reshape(1, -1), ((0, 0), (0, pad)))
    dtb = jnp.pad(dt_bias.reshape(1, -1), ((0, 0), (0, pad)))
    nv = GDN_HEADS * GDN_DV
    o_spec = pl.BlockSpec((None, tm, nv), lambda b, i: (b, i, 0))
    return pl.pallas_call(
        _gdn_pre_kernel,
        grid=(B, S // tm),
        in_specs=[main, prev, nxt,
                  pl.BlockSpec((3, GDN_QKV), lambda b, i: (0, 0)),
                  pl.BlockSpec((1, LANES), lambda b, i: (0, 0)),
                  pl.BlockSpec((1, LANES), lambda b, i: (0, 0))],
        out_specs=[o_spec, o_spec, o_spec, pl.BlockSpec((None, tm, LANES), lambda b, i: (b, i, 0))],
        out_shape=[jax.ShapeDtypeStruct((B, S, nv), F32)] * 3 + [jax.ShapeDtypeStruct((B, S, LANES), F32)],
        compiler_params=_cparams("parallel", "parallel"),
        name="gdn_pre",
    )(p_gdn, p_gdn, p_gdn, conv_w, alog, dtb)


def _bdot(a, b):
    return jnp.dot(a.astype(BF16), b.astype(BF16), preferred_element_type=F32)


def _bdot_nt(a, b):
    return lax.dot_general(a.astype(BF16), b.astype(BF16), (((1,), (1,)), ((), ())), preferred_element_type=F32)


def _bdot_tn(a, b):
    return lax.dot_general(a.astype(BF16), b.astype(BF16), (((0,), (0,)), ((), ())), preferred_element_type=F32)


GDN_STEP_CHUNKS = 4


def _gdn_scan_kernel(qf_ref, kf_ref, vf_ref, gf_ref, qb_ref, kb_ref, vb_ref, gb_ref, of_ref, ob_ref, s_ref):
    C = GDN_CHUNK
    H = GDN_HEADS
    G = qf_ref.shape[0] // C
    in_refs = ((qf_ref, kf_ref, vf_ref, gf_ref), (qb_ref, kb_ref, vb_ref, gb_ref))
    o_refs = (of_ref, ob_ref)

    @pl.when(pl.program_id(1) == 0)
    def _():
        s_ref[...] = jnp.zeros_like(s_ref)

    ri = lax.broadcasted_iota(jnp.int32, (C, C), 0)
    ci = lax.broadcasted_iota(jnp.int32, (C, C), 1)
    eye = (ri == ci).astype(F32)
    incl = ((ci <= ri), (ci >= ri))
    strict = ((ci < ri), (ci > ri))
    rows = lambda g: slice(g * C, (g + 1) * C)
    cols = lambda h: slice(h * GDN_DK, (h + 1) * GDN_DK)
    chains = [(d, g, h) for d in range(2) for g in range(G) for h in range(H)]

    gbv = {(d, g): in_refs[d][3][rows(g), :] for d in range(2) for g in range(G)}
    gc_all = {dg: jnp.dot(incl[dg[0]].astype(F32), gbv[dg], precision=HI, preferred_element_type=F32)
              for dg in gbv}
    gc_t = {dg: gc_all[dg].T for dg in gbv}

    kk, gamma, rhs, qe, kdec, gend = {}, {}, {}, {}, {}, {}
    for ch in chains:
        d, g, h = ch
        c = d * H + h
        q, k, v = (in_refs[d][n][rows(g), cols(h)] for n in range(3))
        end = C - 1 if d == 0 else 0
        beta = gbv[d, g][:, 2 * H + c:2 * H + c + 1]
        gc_c = gc_all[d, g][:, c:c + 1]
        gc_r = gc_t[d, g][c:c + 1, :]
        g_tot = gc_all[d, g][end:end + 1, c:c + 1]
        gamma[ch] = jnp.where(incl[d], jnp.exp(jnp.where(incl[d], gc_c - gc_r, 0.0)), 0.0)
        e_c = jnp.exp(gc_c)
        kb = k * beta
        kk[ch] = _bdot_nt(jnp.concatenate([kb, q], axis=0), k)
        rhs[ch] = jnp.concatenate([v * beta, kb * e_c], axis=1).astype(BF16)
        qe[ch] = q * e_c
        kdec[ch] = (k * jnp.exp(g_tot - gc_c)).astype(BF16)
        gend[ch] = jnp.exp(g_tot)
    m = {ch: jnp.where(strict[ch[0]], kk[ch][:C] * gamma[ch], 0.0) for ch in chains}
    a_intra = {ch: (kk[ch][C:] * gamma[ch]).astype(BF16) for ch in chains}
    t = {ch: eye - m[ch] for ch in chains}
    pw = m
    for _ in range(5):
        pw = {ch: _bdot(pw[ch], pw[ch]) for ch in chains}
        t = {ch: t[ch] + _bdot(t[ch], pw[ch]) for ch in chains}
    uw = {ch: jnp.dot(t[ch].astype(BF16), rhs[ch], preferred_element_type=F32) for ch in chains}
    wq = {ch: jnp.concatenate([uw[ch][:, GDN_DV:], qe[ch]], axis=0).astype(BF16) for ch in chains}

    heads = [(d, h) for d in range(2) for h in range(H)]
    s = {dh: s_ref[dh[0] * H + dh[1]] for dh in heads}
    for j in range(G):
        at = lambda dh: (dh[0], j if dh[0] == 0 else G - 1 - j, dh[1])
        ws = {dh: jnp.dot(wq[at(dh)], s[dh].astype(BF16), preferred_element_type=F32) for dh in heads}
        v_new = {dh: uw[at(dh)][:, :GDN_DV] - ws[dh][:C] for dh in heads}
        for dh in heads:
            d, g, h = at(dh)
            o_refs[d][rows(g), cols(h)] = ws[dh][C:] + jnp.dot(a_intra[at(dh)], v_new[dh].astype(BF16),
                                                               preferred_element_type=F32)
        s = {dh: s[dh] * gend[at(dh)] + _bdot_tn(kdec[at(dh)], v_new[dh]) for dh in heads}
    for dh in heads:
        s_ref[dh[0] * H + dh[1]] = s[dh]


def gdn_scan(q, k, v, gb, n_ctx_chunks):
    B, Lt, NV = q.shape
    R = GDN_CHUNK * GDN_STEP_CHUNKS
    n = Lt // R
    n_ctx = n_ctx_chunks // GDN_STEP_CHUNKS
    assert n * R == Lt and n_ctx * GDN_STEP_CHUNKS == n_ctx_chunks

    def fwd(b, i):
        return (b, i, 0)

    def bwd(b, i):
        return (b, jnp.where(i < n_ctx, n_ctx - 1 - i, n - 1 - (i - n_ctx)), 0)

    def specs(imap):
        return [pl.BlockSpec((None, R, NV), imap)] * 3 + [pl.BlockSpec((None, R, LANES), imap)]

    return pl.pallas_call(
        _gdn_scan_kernel,
        grid=(B, n),
        in_specs=specs(fwd) + specs(bwd),
        out_specs=[pl.BlockSpec((None, R, NV), fwd), pl.BlockSpec((None, R, NV), bwd)],
        out_shape=[jax.ShapeDtypeStruct((B, Lt, NV), F32)] * 2,
        scratch_shapes=[pltpu.VMEM((2 * GDN_HEADS, GDN_DK, GDN_DV), F32)],
        compiler_params=_cparams("parallel", "arbitrary"),
        name="gdn_scan",
    )(q, k, v, gb, q, k, v, gb)


MLA_QK = MLA_NOPE + MLA_ROPE
MLA_KVIN = MLA_KV_LORA + LANES


def _rope_partner(w_rope):
    nf = MLA_ROPE // 4
    parts = []
    for half in range(2):
        a = w_rope[..., half * 2 * nf:half * 2 * nf + nf]
        b = w_rope[..., half * 2 * nf + nf:(half + 1) * 2 * nf]
        parts += [-b, a]
    return jnp.concatenate(parts, axis=-1)


def _head_pad(nope, rope):
    pad = jnp.zeros(nope.shape[:-1] + (LANES - MLA_QK,), nope.dtype)
    out = jnp.concatenate([nope, rope, pad], axis=-1)
    return out.reshape(out.shape[:-2] + (MLA_HEADS * LANES,))


def mla_weights(w_uq, w_ukv):
    wq = w_uq.reshape(MLA_Q_LORA, MLA_HEADS, MLA_QK)
    qn, qr = wq[..., :MLA_NOPE], wq[..., MLA_NOPE:]
    wq2 = jnp.concatenate([_head_pad(qn, qr), _head_pad(jnp.zeros_like(qn), _rope_partner(qr))], axis=-1)
    wkv = w_ukv.reshape(MLA_KV_LORA, MLA_HEADS, MLA_NOPE + MLA_V)
    kn, vv = wkv[..., :MLA_NOPE], wkv[..., MLA_NOPE:]
    eye = jnp.broadcast_to(jnp.eye(MLA_ROPE, dtype=F32)[:, None, :], (MLA_ROPE, MLA_HEADS, MLA_ROPE))
    z_kn = jnp.zeros((MLA_ROPE, MLA_HEADS, MLA_NOPE), F32)
    z_rope = jnp.zeros((MLA_KV_LORA, MLA_HEADS, MLA_ROPE), F32)
    top = jnp.concatenate([_head_pad(kn, z_rope), _head_pad(jnp.zeros_like(kn), z_rope), _head_pad(vv, z_rope)],
                          axis=-1)
    mid = jnp.concatenate([_head_pad(z_kn, eye), _head_pad(z_kn, _rope_partner(eye)),
                           jnp.zeros((MLA_ROPE, MLA_HEADS * LANES), F32)], axis=-1)
    bot = jnp.zeros((MLA_KVIN - MLA_KV_LORA - MLA_ROPE, top.shape[1]), F32)
    return wq2.astype(BF16), jnp.concatenate([top, mid, bot], axis=0).astype(BF16)


def rope_tables(row, col):
    nf = MLA_ROPE // 4
    inv_freq = ROPE_THETA ** (-jnp.arange(nf, dtype=F32) / nf)
    ang = jnp.concatenate([row.astype(F32)[:, None] * inv_freq[None, :]] * 2
                          + [col.astype(F32)[:, None] * inv_freq[None, :]] * 2, axis=-1)
    n = ang.shape[0]
    pad = jnp.zeros((n, LANES - MLA_QK), F32)
    cos = jnp.concatenate([jnp.ones((n, MLA_NOPE), F32), jnp.cos(ang), pad], axis=-1)
    sin = jnp.concatenate([jnp.zeros((n, MLA_NOPE), F32), jnp.sin(ang), pad], axis=-1)
    return cos, sin


def _mla_proj_kernel(p_ref, gq_ref, gkv_ref, wq_ref, wk_ref, cos_ref, sin_ref, q_ref, k_ref, v_ref):
    HL = MLA_HEADS * LANES
    cos = jnp.concatenate([cos_ref[...]] * MLA_HEADS, axis=1)
    sin = jnp.concatenate([sin_ref[...]] * MLA_HEADS, axis=1)
    cq = p_ref[:, :MLA_Q_LORA]
    cqn = cq * lax.rsqrt(jnp.mean(cq * cq, axis=-1, keepdims=True) + EPS) * gq_ref[...]
    qq = jnp.dot(cqn.astype(BF16), wq_ref[...], preferred_element_type=F32)
    q_ref[...] = ((qq[:, :HL] * cos + qq[:, HL:] * sin) * (MLA_QK ** -0.5 * math.log2(math.e))).astype(q_ref.dtype)
    ck = p_ref[:, MLA_Q_LORA:]
    lane = lax.broadcasted_iota(jnp.int32, ck.shape, 1)
    is_kv = lane < MLA_KV_LORA
    ms = jnp.sum(jnp.where(is_kv, ck * ck, 0.0), axis=-1, keepdims=True) * (1.0 / MLA_KV_LORA)
    ckn = jnp.where(is_kv, ck * lax.rsqrt(ms + EPS) * gkv_ref[...], ck)
    kk = jnp.dot(ckn.astype(BF16), wk_ref[...], preferred_element_type=F32)
    k_ref[...] = (kk[:, :HL] * cos + kk[:, HL:2 * HL] * sin).astype(k_ref.dtype)
    vv = kk[:, 2 * HL:]
    vlane = lax.broadcasted_iota(jnp.int32, vv.shape, 1)
    v_ref[...] = jnp.where(vlane % LANES == MLA_V, 1.0, vv).astype(v_ref.dtype)


def mla_proj(p_mla, gq, gkv, wq2, wk2, cos, sin, tm):
    B, S, C = p_mla.shape
    tm = min(tm, S)
    HL = MLA_HEADS * LANES
    gkv = jnp.pad(gkv, ((0, 0), (0, MLA_KVIN - MLA_KV_LORA)))
    row = lambda n: pl.BlockSpec((None, tm, n), lambda b, i: (b, i, 0))
    full = lambda a: pl.BlockSpec(a.shape, lambda b, i: (0, 0))
    tab = pl.BlockSpec((tm, LANES), lambda b, i: (i, 0))
    return pl.pallas_call(
        _mla_proj_kernel,
        grid=(B, S // tm),
        in_specs=[row(C), full(gq), full(gkv), full(wq2), full(wk2), tab, tab],
        out_specs=[row(HL)] * 3,
        out_shape=[jax.ShapeDtypeStruct((B, S, HL), BF16)] * 3,
        compiler_params=_cparams("parallel", "parallel"),
        name="mla_proj",
    )(p_mla, gq, gkv, wq2, wk2, cos, sin)


ATT_SLAB = 32


def _mla_attn_kernel(q_ref, k_ref, v_ref, o_ref, *, tk):
    tq = q_ref.shape[0]
    Tk = k_ref.shape[0]
    n_full, rem = Tk // tk, Tk % tk
    heads = (slice(0, LANES), slice(LANES, 2 * LANES))
    qs = [q_ref[:, hs] for hs in heads]

    def step(carry, start, size):
        ss = [lax.dot_general(qs[h], k_ref[pl.ds(start, size), heads[h]], (((1,), (1,)), ((), ())),
                              preferred_element_type=F32) for h in range(2)]
        out = []
        for h in range(2):
            m, acc = carry[h]
            m_new = jnp.maximum(m, jnp.max(ss[h], axis=-1, keepdims=True))
            p = jnp.concatenate([jnp.exp2(ss[h][r:r + ATT_SLAB] - m_new[r:r + ATT_SLAB]).astype(BF16)
                                 for r in range(0, tq, ATT_SLAB)], axis=0)
            acc = acc * jnp.exp2(m - m_new) + jnp.dot(p, v_ref[pl.ds(start, size), heads[h]],
                                                      preferred_element_type=F32)
            out.append((m_new, acc))
        return tuple(out)

    carry = tuple((jnp.full((tq, 1), -jnp.inf, F32), jnp.zeros((tq, LANES), F32)) for _ in range(2))
    if n_full:
        carry = lax.fori_loop(0, n_full, lambda c, cr: step(cr, pl.multiple_of(c * tk, tk), tk), carry)
    if rem:
        carry = step(carry, n_full * tk, rem)
    o0, o1 = [acc / acc[:, MLA_V:MLA_V + 1] for _, acc in carry]
    lane = lax.broadcasted_iota(jnp.int32, (tq, LANES), 1)
    o_ref[...] = jnp.where(lane < MLA_V, o0, pltpu.roll(o1, MLA_V, 1)).astype(o_ref.dtype)


def mla_attn(q, k, v, tq, tk):
    B, S, _ = q.shape
    Tk = k.shape[1]
    tq = min(tq, S)
    return pl.pallas_call(
        functools.partial(_mla_attn_kernel, tk=tk),
        grid=(B, MLA_HEADS // 2, S // tq),
        in_specs=[pl.BlockSpec((None, tq, 2 * LANES), lambda b, h, i: (b, i, h)),
                  pl.BlockSpec((None, Tk, 2 * LANES), lambda b, h, i: (b, 0, h)),
                  pl.BlockSpec((None, Tk, 2 * LANES), lambda b, h, i: (b, 0, h))],
        out_specs=pl.BlockSpec((None, tq, LANES), lambda b, h, i: (b, i, h)),
        out_shape=jax.ShapeDtypeStruct((B, S, MLA_HEADS * MLA_V), BF16),
        compiler_params=_cparams("parallel", "parallel", "arbitrary"),
        name="mla_attn",
    )(q, k, v)


def _route(logits, rb):
    lane = lax.broadcasted_iota(jnp.int32, logits.shape, 1)
    neg = -jnp.inf
    scores = jax.nn.sigmoid(logits)
    sel = scores + rb

    def top2(masked):
        m1 = jnp.max(masked, axis=-1, keepdims=True)
        i1 = jnp.min(jnp.where(masked == m1, lane, LANES), axis=-1, keepdims=True)
        rest = jnp.where(lane == i1, neg, masked)
        m2 = jnp.max(rest, axis=-1, keepdims=True)
        i2 = jnp.min(jnp.where(rest == m2, lane, LANES), axis=-1, keepdims=True)
        return m1, i1, m2, i2

    best = None
    for gi in range(N_GROUPS):
        in_g = jnp.logical_and(lane >= gi * EXPERTS_PER_GROUP, lane < (gi + 1) * EXPERTS_PER_GROUP)
        m1, _, m2, _ = top2(jnp.where(in_g, sel, neg))
        gs = m1 + m2
        if best is None:
            best, grp = gs, jnp.zeros_like(gs, dtype=jnp.int32)
        else:
            better = gs > best
            grp = jnp.where(better, gi, grp)
            best = jnp.where(better, gs, best)
    lo = grp * EXPERTS_PER_GROUP
    in_grp = jnp.logical_and(lane >= lo, lane < lo + EXPERTS_PER_GROUP)
    _, i1, _, i2 = top2(jnp.where(in_grp, sel, neg))
    picked = jnp.where(jnp.logical_or(lane == i1, lane == i2), scores, 0.0)
    return picked / jnp.sum(picked, axis=-1, keepdims=True)


def _merge_kernel(hyv_ref, of_ref, ob_ref, pg_ref, at_ref, x_ref, gt1_ref, sc2_ref, sh2_ref, gng_ref, n2g_ref,
                  whb_ref, wgd_ref, wml_ref, wo_ref, rw_ref, rb_ref, xo_ref, h2_ref, gate_ref):
    nv = GDN_HEADS * GDN_DV
    D = x_ref.shape[1]
    o = of_ref[...] + ob_ref[...]
    z = pg_ref[:, :nv]
    ys = []
    for h in range(GDN_HEADS):
        sl = slice(h * GDN_DV, (h + 1) * GDN_DV)
        oh = o[:, sl]
        on = oh * lax.rsqrt(jnp.mean(oh * oh, axis=-1, keepdims=True) + EPS) * gng_ref[...]
        ys.append(on * _silu(z[:, sl]))
    y_gdn = jnp.dot(jnp.concatenate(ys, axis=1).astype(BF16), wgd_ref[...], preferred_element_type=F32)
    y_hy = jnp.dot(hyv_ref[...].astype(BF16), whb_ref[...], preferred_element_type=F32)
    y_mla = jnp.dot(at_ref[...], wml_ref[...], preferred_element_type=F32)
    merged = (jax.nn.sigmoid(pg_ref[:, nv:nv + D]) * y_hy
              + jax.nn.sigmoid(pg_ref[:, nv + D:nv + 2 * D]) * y_gdn
              + jax.nn.sigmoid(pg_ref[:, nv + 2 * D:]) * y_mla)
    mix = jnp.dot(merged.astype(BF16), wo_ref[...], preferred_element_type=F32)
    xn = x_ref[...] + gt1_ref[...] * mix
    xo_ref[...] = xn
    y2 = xn * lax.rsqrt(jnp.mean(xn * xn, axis=-1, keepdims=True) + EPS) * n2g_ref[...]
    h2 = y2 * (1.0 + sc2_ref[...]) + sh2_ref[...]
    h2_ref[...] = h2.astype(h2_ref.dtype)
    logits = jnp.dot(h2, rw_ref[...], precision=HI, preferred_element_type=F32)
    gate_ref[...] = _route(logits, rb_ref[...])


def merge_out(hyv, o_f, o_b, pg, attn, x, gt1, sc2, sh2, gdn_norm_g, norm2_g, w_hy, w_gdn, w_mla, w_out,
              router_w, router_b, tm):
    B, S, D = x.shape
    tm = min(tm, S)
    row = lambda n: pl.BlockSpec((None, tm, n), lambda b, i: (b, i, 0))
    full = lambda a: pl.BlockSpec(a.shape, lambda b, i: (0, 0))
    nv = GDN_HEADS * GDN_DV
    return pl.pallas_call(
        _merge_kernel,
        grid=(B, S // tm),
        in_specs=[row(HY_W), row(nv), row(nv), row(pg.shape[2]), row(MLA_HEADS * MLA_V), row(D),
                  _mod_spec(gt1, D), _mod_spec(sc2, D), _mod_spec(sh2, D),
                  full(gdn_norm_g), full(norm2_g), full(w_hy), full(w_gdn), full(w_mla), full(w_out),
                  full(router_w), full(router_b)],
        out_specs=[row(D), row(D), row(LANES)],
        out_shape=[jax.ShapeDtypeStruct((B, S, D), F32), jax.ShapeDtypeStruct((B, S, D), BF16),
                   jax.ShapeDtypeStruct((B, S, LANES), F32)],
        compiler_params=_cparams("parallel", "parallel"),
        name="merge_out",
    )(hyv, o_f, o_b, pg, attn, x, gt1, sc2, sh2, gdn_norm_g, norm2_g, w_hy, w_gdn, w_mla, w_out,
      router_w, router_b)


def _moe_kernel(h_ref, gate_ref, w1_ref, w3_ref, w2_ref, x_ref, gt2_ref, fg_ref, o_ref, acc_ref, *, final_norm):
    e = pl.program_id(1)

    @pl.when(e == 0)
    def _():
        acc_ref[...] = jnp.zeros_like(acc_ref)

    h = h_ref[...]
    he = _silu(jnp.dot(h, w1_ref[...], preferred_element_type=F32)) * jnp.dot(h, w3_ref[...], preferred_element_type=F32)
    gate = gate_ref[...]
    lane = lax.broadcasted_iota(jnp.int32, gate.shape, 1)
    ge = jnp.sum(jnp.where(lane == e, gate, 0.0), axis=-1, keepdims=True)
    acc_ref[...] += ge * jnp.dot(he.astype(BF16), w2_ref[...], preferred_element_type=F32)

    @pl.when(e == pl.num_programs(1) - 1)
    def _():
        xn = x_ref[...] + gt2_ref[...] * acc_ref[...]
        if final_norm:
            xn = xn * lax.rsqrt(jnp.mean(xn * xn, axis=-1, keepdims=True) + EPS) * fg_ref[...]
        o_ref[...] = xn


def moe(h2, gate, w1, w3, w2, x, gt2, final_g, S, tm, final_norm):
    T, D = x.shape
    E, _, FF = w1.shape
    tm = min(tm, S)
    per_b = S // tm
    if gt2.shape[0] == 1:
        gt_spec = pl.BlockSpec((None, 1, D), lambda i, e: (0, 0, 0))
    else:
        gt_spec = pl.BlockSpec((None, 1, D), lambda i, e: (i // per_b, 0, 0))
    return pl.pallas_call(
        functools.partial(_moe_kernel, final_norm=final_norm),
        grid=(T // tm, E),
        in_specs=[pl.BlockSpec((tm, D), lambda i, e: (i, 0)),
                  pl.BlockSpec((tm, LANES), lambda i, e: (i, 0)),
                  pl.BlockSpec((None, D, FF), lambda i, e: (e, 0, 0)),
                  pl.BlockSpec((None, D, FF), lambda i, e: (e, 0, 0)),
                  pl.BlockSpec((None, FF, D), lambda i, e: (e, 0, 0)),
                  pl.BlockSpec((tm, D), lambda i, e: (i, 0)),
                  gt_spec,
                  pl.BlockSpec((1, D), lambda i, e: (0, 0))],
        out_specs=pl.BlockSpec((tm, D), lambda i, e: (i, 0)),
        out_shape=jax.ShapeDtypeStruct((T, D), F32),
        scratch_shapes=[pltpu.VMEM((tm, D), F32)],
        compiler_params=_cparams("parallel", "arbitrary"),
        name="moe",
    )(h2, gate, w1, w3, w2, x, gt2, final_g)


IN_SIZES = (3 * HY_W, GDN_QKV, GDN_HEADS * GDN_DV, 2 * GDN_HEADS, 2 * GDN_HEADS, MLA_Q_LORA, MLA_KV_LORA, MLA_ROPE)


def _split_w_in(w_in):
    D = w_in.shape[0]
    parts, off = [], 0
    for n in IN_SIZES:
        parts.append(w_in[:, off:off + n])
        off += n
    hy, qkv, z, a, b, cq, ckv, kr = parts
    gate = w_in[:, off:]
    zeros = lambda n: jnp.zeros((D, n), w_in.dtype)
    w_hy = hy
    w_gdn = jnp.concatenate([qkv, a, b, zeros(LANES - 4 * GDN_HEADS)], axis=1)
    w_mla = jnp.concatenate([cq, ckv, kr, zeros(LANES - MLA_ROPE)], axis=1)
    w_gate = jnp.concatenate([z, gate], axis=1)
    return [w.astype(BF16) for w in (w_hy, w_gdn, w_mla, w_gate)]


def _layer(x, cx, mod, mod_c, lw, tabs, router_w, router_b, final_g, update_ctx, last):
    B, S, D = x.shape
    Lc = cx.shape[1]
    sh1, sc1, gt1, sh2, sc2, gt2 = [m[:, None, :] for m in jnp.split(mod, 6, axis=-1)]
    csh1, csc1, cgt1, csh2, csc2, cgt2 = [m[:, None, :] for m in jnp.split(mod_c, 6, axis=-1)]
    n1g = lw['norm1_g'][None, :]
    w_hy, w_gdn, w_mla, w_gate = _split_w_in(lw['w_in'])

    def project(xx, sc, sh, tm):
        return [in_proj(xx, n1g, sc, sh, w, tm) for w in (w_hy, w_gdn, w_mla, w_gate)]

    p_hy, p_gdn, p_mla, p_gate = project(x, sc1, sh1, 512)
    c_hy, c_gdn, c_mla, c_gate = project(cx, csc1, csh1, 256)

    feats_c = gdn_pre(c_gdn, lw['gdn_conv_w'], lw['gdn_a_log'], lw['gdn_dt_bias'], 256)
    feats_l = gdn_pre(p_gdn, lw['gdn_conv_w'], lw['gdn_a_log'], lw['gdn_dt_bias'], 512)
    o_f, o_b = gdn_scan(*[jnp.concatenate([c, l], axis=1) for c, l in zip(feats_c, feats_l)], Lc // GDN_CHUNK)

    wq2, wk2 = mla_weights(lw['mla_w_uq'], lw['mla_w_ukv'])
    gq, gkv = lw['mla_q_norm_g'][None, :], lw['mla_kv_norm_g'][None, :]
    cos_l, sin_l, cos_c, sin_c = tabs
    q_l, k_l, v_l = mla_proj(p_mla, gq, gkv, wq2, wk2, cos_l, sin_l, 512)
    q_c, k_c, v_c = mla_proj(c_mla, gq, gkv, wq2, wk2, cos_c, sin_c, 256)
    attn_l = mla_attn(q_l, jnp.concatenate([k_l, k_c], axis=1), jnp.concatenate([v_l, v_c], axis=1), 512, 1024)

    def hyena(p, L):
        filt = hy_filter(L, lw['hy_f_w1'], lw['hy_f_b1'], lw['hy_f_w2'], lw['hy_f_b2'], lw['hy_f_w3'],
                         lw['hy_f_freq'], lw['hy_decay'])
        x0, u = hy_pre(p, lw['hy_conv_w'], lw['hy_conv_b'][None, :], 512)
        return hyena_long_conv(x0, u, filt, lw['hy_bias'])

    hyv_l = hyena(p_hy, S)

    wb = lambda name: lw[name].astype(BF16)
    rw = jnp.pad(router_w, ((0, 0), (0, LANES - N_EXPERTS)))
    rb = jnp.pad(router_b[None, :], ((0, 0), (0, LANES - N_EXPERTS)))
    n2g, gng = lw['norm2_g'][None, :], lw['gdn_norm_g'][None, :]
    w1, w3, w2 = wb('moe_w1'), wb('moe_w3'), wb('moe_w2')

    def finish(xx, hyv, of_, ob_, pg, attn, gt1_, sc2_, sh2_, gt2_, tm, tm_moe, fin):
        Bx, Sx, _ = xx.shape
        xn, h2, gate = merge_out(hyv, of_, ob_, pg, attn, xx, gt1_, sc2_, sh2_, gng, n2g, wb('hy_out'), wb('gdn_out'),
                                 wb('mla_out'), wb('w_out'), rw, rb, tm)
        out = moe(h2.reshape(Bx * Sx, D), gate.reshape(Bx * Sx, LANES), w1, w3, w2, xn.reshape(Bx * Sx, D),
                  gt2_, final_g, Sx, tm_moe, fin)
        return out.reshape(Bx, Sx, D)

    x_new = finish(x, hyv_l, o_f[:, Lc:], o_b[:, Lc:], p_gate, attn_l, gt1, sc2, sh2, gt2, 256, 1024, last)
    if update_ctx:
        hyv_c = hyena(c_hy, Lc)
        attn_c = mla_attn(q_c, k_c, v_c, 256, 1024)
        cx = finish(cx, hyv_c, o_f[:, :Lc], o_b[:, :Lc], c_gate, attn_c, cgt1, csc2, csh2, cgt2, 256, 256, False)
    return x_new, cx


def kernel(x, c, ctx, c_ctx, w_ada, b_ada, norm1_g, norm2_g, w_in, hy_conv_w, hy_conv_b, hy_f_w1, hy_f_b1, hy_f_w2, hy_f_b2, hy_f_w3, hy_f_freq, hy_decay, hy_bias, hy_out, gdn_conv_w, gdn_a_log, gdn_dt_bias, gdn_norm_g, gdn_out, mla_q_norm_g, mla_w_uq, mla_kv_norm_g, mla_w_ukv, mla_out, w_out, moe_w1, moe_w3, moe_w2, router_w, router_b, final_norm_g):
    per_layer = dict(norm1_g=norm1_g, norm2_g=norm2_g, w_in=w_in, hy_conv_w=hy_conv_w, hy_conv_b=hy_conv_b,
                     hy_f_w1=hy_f_w1, hy_f_b1=hy_f_b1, hy_f_w2=hy_f_w2, hy_f_b2=hy_f_b2, hy_f_w3=hy_f_w3,
                     hy_f_freq=hy_f_freq, hy_decay=hy_decay, hy_bias=hy_bias, hy_out=hy_out,
                     gdn_conv_w=gdn_conv_w, gdn_a_log=gdn_a_log, gdn_dt_bias=gdn_dt_bias, gdn_norm_g=gdn_norm_g,
                     gdn_out=gdn_out, mla_q_norm_g=mla_q_norm_g, mla_w_uq=mla_w_uq, mla_kv_norm_g=mla_kv_norm_g,
                     mla_w_ukv=mla_w_ukv, mla_out=mla_out, w_out=w_out, moe_w1=moe_w1, moe_w3=moe_w3, moe_w2=moe_w2)
    B, S, D = x.shape
    Lc = ctx.shape[1]
    depth = w_ada.shape[0]
    rows = S // GRID_W
    row = jnp.repeat(jnp.arange(rows, dtype=jnp.int32), GRID_W)
    col = jnp.tile(jnp.arange(GRID_W, dtype=jnp.int32), rows)
    cos_l, sin_l = rope_tables(row, col)
    zero = jnp.zeros((Lc,), jnp.int32)
    cos_c, sin_c = rope_tables(zero, zero)
    tabs = (cos_l, sin_l, cos_c, sin_c)
    cc = jnp.concatenate([c, c_ctx[None, :], jnp.zeros((2 * SUBLANES - B - 1, D), F32)], axis=0)
    final_g = final_norm_g[None, :]
    cx = ctx
    for l in range(depth):
        lw = {k: v[l] for k, v in per_layer.items()}
        mods = ada_mod(cc, w_ada[l], b_ada[l][None, :])
        x, cx = _layer(x, cx, mods[:B], mods[B:B + 1], lw, tabs, router_w, router_b, final_g,
                       l < depth - 1, l == depth - 1)
    return x
```

```python
import functools
import math

import jax
import jax.numpy as jnp
from jax import lax
from jax.experimental import pallas as pl
from jax.experimental.pallas import tpu as pltpu

F32 = jnp.float32
BF16 = jnp.bfloat16
HI = lax.Precision.HIGHEST
EPS = 1e-6

GRID_W = 64
HY_W = 512
HY_EMB = 33
HY_BANDS = (HY_EMB - 1) // 2
HY_MOD_SHIFT = 0.05
GDN_HEADS = 4
GDN_DK = 128
GDN_DV = 128
GDN_CHUNK = 64
MLA_HEADS = 8
MLA_NOPE = 64
MLA_ROPE = 32
MLA_V = 64
MLA_Q_LORA = 768
MLA_KV_LORA = 256
ROPE_THETA = 10000.0
N_EXPERTS = 16
N_GROUPS = 4
EXPERTS_PER_GROUP = N_EXPERTS // N_GROUPS
EXPERT_FF = 512
LANES = 128
SUBLANES = 8
VMEM_LIMIT = 56 * 1024 * 1024


def _cparams(*sem):
    return pltpu.CompilerParams(dimension_semantics=sem, vmem_limit_bytes=VMEM_LIMIT)


def _silu(x):
    return x * jax.nn.sigmoid(x)


def _ada_kernel(c_ref, w_ref, b_ref, o_ref):
    a = _silu(c_ref[...])
    o_ref[...] = jnp.dot(a, w_ref[...], precision=HI, preferred_element_type=F32) + b_ref[...]


def ada_mod(cc, w, b):
    R, D = cc.shape
    N = w.shape[1]
    tn = 1536
    return pl.pallas_call(
        _ada_kernel,
        grid=(N // tn,),
        in_specs=[pl.BlockSpec((R, D), lambda j: (0, 0)),
                  pl.BlockSpec((D, tn), lambda j: (0, j)),
                  pl.BlockSpec((1, tn), lambda j: (0, j))],
        out_specs=pl.BlockSpec((R, tn), lambda j: (0, j)),
        out_shape=jax.ShapeDtypeStruct((R, N), F32),
        compiler_params=_cparams("parallel"),
        name="ada_mod",
    )(cc, w, b)


def _inproj_kernel(x_ref, g_ref, sc_ref, sh_ref, w_ref, o_ref):
    x = x_ref[...]
    y = x * lax.rsqrt(jnp.mean(x * x, axis=-1, keepdims=True) + EPS) * g_ref[...]
    h = y * (1.0 + sc_ref[...]) + sh_ref[...]
    o_ref[...] = jnp.dot(h.astype(BF16), w_ref[...], preferred_element_type=F32).astype(o_ref.dtype)


def _mod_spec(m, D):
    if m.shape[0] == 1:
        return pl.BlockSpec((None, 1, D), lambda b, i: (0, 0, 0))
    return pl.BlockSpec((None, 1, D), lambda b, i: (b, 0, 0))


def _inproj_into_kernel(x_ref, g_ref, sc_ref, sh_ref, w_ref, dst_ref, o_ref):
    del dst_ref
    _inproj_kernel(x_ref, g_ref, sc_ref, sh_ref, w_ref, o_ref)


def in_proj(x, g, sc, sh, w, tm, rows_total=None, into=None):
    B, S, D = x.shape
    N = w.shape[1]
    tm = min(tm, S)
    in_specs = [pl.BlockSpec((None, tm, D), lambda b, i: (b, i, 0)),
                pl.BlockSpec((1, D), lambda b, i: (0, 0)),
                _mod_spec(sc, D), _mod_spec(sh, D),
                pl.BlockSpec((D, N), lambda b, i: (0, 0))]
    args = [x, g, sc, sh, w]
    if into is None:
        body, rows, off, alias = _inproj_kernel, rows_total or S, 0, {}
    else:
        body, rows, alias = _inproj_into_kernel, into.shape[1], {len(args): 0}
        off = (rows - S) // tm
        in_specs.append(pl.BlockSpec(memory_space=pl.ANY))
        args.append(into)
    return pl.pallas_call(
        body,
        grid=(B, S // tm),
        in_specs=in_specs,
        out_specs=pl.BlockSpec((None, tm, N), lambda b, i: (b, i + off, 0)),
        out_shape=jax.ShapeDtypeStruct((B, rows, N), F32),
        input_output_aliases=alias,
        compiler_params=_cparams("parallel", "parallel"),
        name="in_proj",
    )(*args)


def _halo_specs(tm, S, C, col_block=0):
    nb8 = tm // SUBLANES
    last8 = S // SUBLANES - 1
    main = pl.BlockSpec((None, tm, C), lambda b, i: (b, i, col_block))
    prev = pl.BlockSpec((None, SUBLANES, C), lambda b, i: (b, jnp.maximum(i * nb8 - 1, 0), col_block))
    nxt = pl.BlockSpec((None, SUBLANES, C), lambda b, i: (b, jnp.minimum((i + 1) * nb8, last8), col_block))
    return main, prev, nxt


def _conv3(x, prev8, next8, w, first, last):
    tm = x.shape[0]
    row = lax.broadcasted_iota(jnp.int32, x.shape, 0)
    p_row = jnp.where(first, 0.0, prev8[SUBLANES - 1:SUBLANES, :])
    n_row = jnp.where(last, 0.0, next8[0:1, :])
    x_prev = jnp.where(row == 0, p_row, pltpu.roll(x, 1, 0))
    x_next = jnp.where(row == tm - 1, n_row, pltpu.roll(x, tm - 1, 0))
    return x_prev * w[0:1, :] + x * w[1:2, :] + x_next * w[2:3, :]


def _hy_pre_kernel(p_ref, pp_ref, pn_ref, w_ref, b_ref, x0_ref, u_ref):
    i = pl.program_id(1)
    y = _conv3(p_ref[...], pp_ref[...], pn_ref[...], w_ref[...], i == 0, i == pl.num_programs(1) - 1)
    y = y + b_ref[...]
    x0_ref[...] = y[:, :HY_W]
    u_ref[...] = y[:, HY_W:2 * HY_W] * y[:, 2 * HY_W:]


def hy_pre(p_hy, conv_w, conv_b, tm):
    B, S, C = p_hy.shape
    tm = min(tm, S)
    main, prev, nxt = _halo_specs(tm, S, C)
    o_spec = pl.BlockSpec((None, tm, HY_W), lambda b, i: (b, i, 0))
    return pl.pallas_call(
        _hy_pre_kernel,
        grid=(B, S // tm),
        in_specs=[main, prev, nxt,
                  pl.BlockSpec((3, C), lambda b, i: (0, 0)),
                  pl.BlockSpec((1, C), lambda b, i: (0, 0))],
        out_specs=[o_spec, o_spec],
        out_shape=[jax.ShapeDtypeStruct((B, S, HY_W), F32)] * 2,
        compiler_params=_cparams("parallel", "parallel"),
        name="hy_pre",
    )(p_hy, p_hy, p_hy, conv_w, conv_b)


def _hy_filter_kernel(z_ref, w1_ref, b1_ref, w2_ref, b2_ref, w3_ref, fq_ref, dc_ref, o_ref):
    fq = fq_ref[...]
    h = jnp.sin(fq * (jnp.dot(z_ref[...], w1_ref[...], precision=HI, preferred_element_type=F32) + b1_ref[...]))
    h = jnp.sin(fq * (jnp.dot(h, w2_ref[...], precision=HI, preferred_element_type=F32) + b2_ref[...]))
    h = jnp.dot(h, w3_ref[...], precision=HI, preferred_element_type=F32)
    taps = h * (jnp.exp(-z_ref[:, 0:1] * jnp.abs(dc_ref[...])) + HY_MOD_SHIFT)
    C = o_ref.shape[1]
    is_fwd = z_ref[:, LANES - 2:LANES - 1] > 0.5
    o_ref[...] = jnp.where(is_fwd, taps[:, :C], taps[:, C:]) * z_ref[:, LANES - 1:LANES]


def hy_filter(L, w1, b1, w2, b2, w3, freq, decay):
    t = jnp.linspace(0.0, 1.0, L, dtype=F32)[:, None]
    w = (2.0 * math.pi / L) * jnp.arange(L, dtype=F32)[:, None]
    f = jnp.linspace(1e-4, HY_BANDS - 1, HY_BANDS, dtype=F32)[None, :]
    z = jnp.concatenate([t, jnp.cos(f * w), -jnp.sin(f * w)], axis=-1)
    emb_pad = LANES - HY_EMB
    lag = jnp.concatenate([jnp.arange(L), jnp.zeros((1,), jnp.int32), jnp.arange(L - 1, 0, -1)])
    r = jnp.arange(2 * L)
    flags = jnp.stack([(r < L).astype(F32), (r != L).astype(F32)], axis=1)
    z = jnp.concatenate([z[lag], jnp.zeros((2 * L, emb_pad - 2), F32), flags], axis=1)
    w1 = jnp.pad(w1, ((0, emb_pad), (0, 0)))
    tl = min(2 * L, 512)
    C = w3.shape[1] // 2
    full = lambda a: pl.BlockSpec(a.shape, lambda i: (0, 0))
    b1, b2, freq, decay = b1[None, :], b2[None, :], freq[None, :], decay[None, :]
    return pl.pallas_call(
        _hy_filter_kernel,
        grid=(2 * L // tl,),
        in_specs=[pl.BlockSpec((tl, LANES), lambda i: (i, 0)),
                  full(w1), full(b1), full(w2), full(b2), full(w3), full(freq), full(decay)],
        out_specs=pl.BlockSpec((tl, C), lambda i: (i, 0)),
        out_shape=jax.ShapeDtypeStruct((2 * L, C), F32),
        compiler_params=_cparams("parallel"),
        name="hy_filter",
    )(z, w1, b1, w2, b2, w3, freq, decay)


def _dft_tables(N1, N2):
    N = N1 * N2
    two_pi = 2.0 * math.pi

    def cs(num, den):
        ang = (two_pi / den) * (num % den).astype(F32)
        return jnp.cos(ang), jnp.sin(ang)

    a1 = jnp.arange(N1, dtype=jnp.int32)
    a2 = jnp.arange(N2, dtype=jnp.int32)
    c1, s1 = cs(a1[:, None] * a1[None, :], N1)
    c2, s2 = cs(a2[:, None] * a2[None, :], N2)
    ct, st = cs(a2[:, None] * a1[None, :], N)

    def stack(re, im):
        return jnp.concatenate([jnp.concatenate([re, -im], axis=-1),
                                jnp.concatenate([im, re], axis=-1)], axis=-2)

    tr = ct[:, :, None] * c1[None] - st[:, :, None] * s1[None]
    ti = -(ct[:, :, None] * s1[None] + st[:, :, None] * c1[None])
    h = N1 // 2
    m1_data = stack(tr[:, :, :h], ti[:, :, :h])
    m1_real = jnp.concatenate([tr, ti], axis=-2)
    m2 = stack(c2, -s2)
    ctk, stk = ct.T, st.T
    gr = ctk[:, :, None] * c2.T[None] - stk[:, :, None] * s2.T[None]
    gi = ctk[:, :, None] * s2.T[None] + stk[:, :, None] * c2.T[None]
    m2inv = stack(gr, gi)
    er, ei = c1.T[:h] / N, s1.T[:h] / N
    m3 = stack(er, ei)
    return [_hi_lo_rows(m) for m in (m1_data, m1_real, m2, m2inv, m3)]


def _hi_lo_rows(m):
    hi = m.astype(BF16)
    lo = (m - hi.astype(F32)).astype(BF16)
    return jnp.concatenate([hi, lo], axis=-2)


def _dot3(m2, x):
    M = m2.shape[0] // 2
    x_hi = x.astype(BF16)
    x_lo = (x - x_hi.astype(F32)).astype(BF16)
    a = jnp.dot(m2, x_hi, preferred_element_type=F32)
    return a[:M] + a[M:] + jnp.dot(m2[:M], x_lo, preferred_element_type=F32)


def _hy_pass_a_kernel(u_ref, m_ref, o_ref):
    n1 = o_ref.shape[1]
    for j in range(SUBLANES):
        xj = jnp.concatenate([u_ref[0, :, j, :], u_ref[1, :, j, :]], axis=0)
        a = _dot3(m_ref[j], xj)
        o_ref[0, :, j, :] = a[:n1]
        o_ref[1, :, j, :] = a[n1:]


def _hy_pass_a(u4, m1, n_pairs):
    _, h, N2, C = u4.shape
    N1 = 2 * h
    return pl.pallas_call(
        _hy_pass_a_kernel,
        grid=(n_pairs, N2 // SUBLANES),
        in_specs=[pl.BlockSpec((2, h, SUBLANES, C), lambda p, j: (p, 0, j, 0)),
                  pl.BlockSpec((SUBLANES, 4 * N1, N1), lambda p, j: (j, 0, 0))],
        out_specs=pl.BlockSpec((2, N1, SUBLANES, C), lambda p, j: (0, 0, j, p)),
        out_shape=jax.ShapeDtypeStruct((2, N1, N2, n_pairs * C), F32),
        compiler_params=_cparams("parallel", "parallel"),
        name="hy_pass_a",
    )(u4, m1)


def _hy_spec_kernel(a_ref, m2_ref, o_ref):
    n2 = a_ref.shape[2]
    for j in range(SUBLANES):
        a = jnp.concatenate([a_ref[0, j], a_ref[1, j]], axis=0)
        x = _dot3(m2_ref[...], a)
        o_ref[0, j] = x[:n2]
        o_ref[1, j] = x[n2:]


def _hy_spectrum(a, m2):
    _, N1, N2, C = a.shape
    spec = pl.BlockSpec((2, SUBLANES, N2, C), lambda k: (0, k, 0, 0))
    return pl.pallas_call(
        _hy_spec_kernel,
        grid=(N1 // SUBLANES,),
        in_specs=[spec, pl.BlockSpec((4 * N2, 2 * N2), lambda k: (0, 0))],
        out_specs=spec,
        out_shape=jax.ShapeDtypeStruct(a.shape, F32),
        compiler_params=_cparams("parallel"),
        name="hy_spectrum",
    )(a, m2)


def _hy_pass_b_kernel(a_ref, k_ref, m2_ref, mi_ref, o_ref):
    n2 = a_ref.shape[2]
    for j in range(SUBLANES):
        a = jnp.concatenate([a_ref[0, j], a_ref[1, j]], axis=0)
        x = _dot3(m2_ref[...], a)
        xr, xi = x[:n2], x[n2:]
        kr, ki = k_ref[0, j], k_ref[1, j]
        y = jnp.concatenate([xr * kr - xi * ki, xr * ki + xi * kr], axis=0)
        b = _dot3(mi_ref[j], y)
        o_ref[0, :, j, :] = b[:n2]
        o_ref[1, :, j, :] = b[n2:]


def _hy_pass_b(a, kf, m2, m2inv, n_pairs):
    _, N1, N2, PC = a.shape
    C = PC // n_pairs
    return pl.pallas_call(
        _hy_pass_b_kernel,
        grid=(n_pairs, N1 // SUBLANES),
        in_specs=[pl.BlockSpec((2, SUBLANES, N2, C), lambda p, k: (0, k, 0, p)),
                  pl.BlockSpec((2, SUBLANES, N2, C), lambda p, k: (0, k, 0, 0)),
                  pl.BlockSpec((4 * N2, 2 * N2), lambda p, k: (0, 0)),
                  pl.BlockSpec((SUBLANES, 4 * N2, 2 * N2), lambda p, k: (k, 0, 0))],
        out_specs=pl.BlockSpec((2, N2, SUBLANES, C), lambda p, k: (0, 0, k, p)),
        out_shape=jax.ShapeDtypeStruct((2, N2, N1, PC), F32),
        compiler_params=_cparams("parallel", "parallel"),
        name="hy_pass_b",
    )(a, kf, m2, m2inv)


def _hy_pass_c_kernel(b_ref, u_ref, x0_ref, bias_ref, m3_ref, o_ref):
    h = o_ref.shape[1]
    bias = bias_ref[...]
    for j in range(SUBLANES):
        bb = jnp.concatenate([b_ref[0, j], b_ref[1, j]], axis=0)
        y = _dot3(m3_ref[...], bb)
        for r in range(2):
            o_ref[r, :, j, :] = x0_ref[r, :, j, :] * (y[r * h:(r + 1) * h] + bias * u_ref[r, :, j, :])


def _hy_pass_c(bq, u4, x04, bias, m3, n_pairs):
    _, N2, N1, PC = bq.shape
    C = PC // n_pairs
    h = N1 // 2
    io = pl.BlockSpec((2, h, SUBLANES, C), lambda p, j: (p, 0, j, 0))
    return pl.pallas_call(
        _hy_pass_c_kernel,
        grid=(n_pairs, N2 // SUBLANES),
        in_specs=[pl.BlockSpec((2, SUBLANES, N1, C), lambda p, j: (0, j, 0, p)),
                  io, io,
                  pl.BlockSpec((1, C), lambda p, j: (0, 0)),
                  pl.BlockSpec((2 * N1, 2 * N1), lambda p, j: (0, 0))],
        out_specs=io,
        out_shape=jax.ShapeDtypeStruct(u4.shape, F32),
        compiler_params=_cparams("parallel", "parallel"),
        name="hy_pass_c",
    )(bq, u4, x04, bias, m3)


def hyena_long_conv(x0, u, kbuf, bias):
    B, L, C = u.shape
    N2 = min(128, L // 32)
    N1 = 2 * L // N2
    n_pairs = B // 2
    m1_data, m1_real, m2, m2inv, m3 = _dft_tables(N1, N2)
    kf = _hy_spectrum(_hy_pass_a(kbuf.reshape(2, N1 // 2, N2, C), m1_real, 1), m2)
    u4 = u.reshape(B, N1 // 2, N2, C)
    a = _hy_pass_a(u4, m1_data, n_pairs)
    bq = _hy_pass_b(a, kf, m2, m2inv, n_pairs)
    y = _hy_pass_c(bq, u4, x0.reshape(u4.shape), bias[None, :], m3, n_pairs)
    return y.reshape(B, L, C)


GDN_QKV = GDN_HEADS * (2 * GDN_DK + GDN_DV)


def _gdn_pre_kernel(p_ref, pp_ref, pn_ref, w_ref, alog_ref, dtb_ref, q_ref, k_ref, v_ref, gb_ref, *, n_first):
    i = pl.program_id(1)
    C = GDN_QKV
    first = jnp.logical_or(i == 0, i == n_first)
    last = jnp.logical_or(i == n_first - 1, i == pl.num_programs(1) - 1)
    y = _conv3(p_ref[:, :C], pp_ref[:, :C], pn_ref[:, :C], w_ref[...], first, last)
    y = _silu(y)
    nk = GDN_HEADS * GDN_DK
    for h in range(GDN_HEADS):
        sl = slice(h * GDN_DK, (h + 1) * GDN_DK)
        qh = y[:, sl]
        kh = y[:, nk + h * GDN_DK:nk + (h + 1) * GDN_DK]
        q_ref[:, sl] = qh * (lax.rsqrt(jnp.sum(qh * qh, axis=-1, keepdims=True) + EPS) * (GDN_DK ** -0.5))
        k_ref[:, sl] = kh * lax.rsqrt(jnp.sum(kh * kh, axis=-1, keepdims=True) + EPS)
    v_ref[...] = y[:, 2 * nk:]
    s = p_ref[:, C:]
    lane = lax.broadcasted_iota(jnp.int32, s.shape, 1)
    xa = s + dtb_ref[...]
    softplus = jnp.maximum(xa, 0.0) + jnp.log1p(jnp.exp(-jnp.abs(xa)))
    g = -jnp.exp(alog_ref[...]) * softplus
    gb_ref[...] = jnp.where(lane < 2 * GDN_HEADS, g, jnp.where(lane < 4 * GDN_HEADS, jax.nn.sigmoid(s), 0.0))


def gdn_pre(p_gdn, conv_w, a_log, dt_bias, tm, first_rows):
    B, S, C = p_gdn.shape
    assert first_rows % tm == 0 and S % tm == 0
    main, prev, nxt = _halo_specs(tm, S, C)
    pad = LANES - 2 * GDN_HEADS
    alog = jnp.pad(a_log.reshape(1, -1), ((0, 0), (0, pad)))
    dtb = jnp.pad(dt_bias.reshape(1, -1), ((0, 0), (0, pad)))
    nv = GDN_HEADS * GDN_DV
    o_spec = pl.BlockSpec((None, tm, nv), lambda b, i: (b, i, 0))
    return pl.pallas_call(
        functools.partial(_gdn_pre_kernel, n_first=first_rows // tm),
        grid=(B, S // tm),
        in_specs=[main, prev, nxt,
                  pl.BlockSpec((3, GDN_QKV), lambda b, i: (0, 0)),
                  pl.BlockSpec((1, LANES), lambda b, i: (0, 0)),
                  pl.BlockSpec((1, LANES), lambda b, i: (0, 0))],
        out_specs=[o_spec, o_spec, o_spec, pl.BlockSpec((None, tm, LANES), lambda b, i: (b, i, 0))],
        out_shape=[jax.ShapeDtypeStruct((B, S, nv), F32)] * 3 + [jax.ShapeDtypeStruct((B, S, LANES), F32)],
        compiler_params=_cparams("parallel", "parallel"),
        name="gdn_pre",
    )(p_gdn, p_gdn, p_gdn, conv_w, alog, dtb)


def _bdot(a, b):
    return jnp.dot(a.astype(BF16), b.astype(BF16), preferred_element_type=F32)


def _bdot_nt(a, b):
    return lax.dot_general(a.astype(BF16), b.astype(BF16), (((1,), (1,)), ((), ())), preferred_element_type=F32)


def _bdot_tn(a, b):
    return lax.dot_general(a.astype(BF16), b.astype(BF16), (((0,), (0,)), ((), ())), preferred_element_type=F32)


GDN_STEP_CHUNKS = 4


def _gdn_scan_kernel(qf_ref, kf_ref, vf_ref, gf_ref, qb_ref, kb_ref, vb_ref, gb_ref, of_ref, ob_ref, s_ref):
    C = GDN_CHUNK
    H = GDN_HEADS
    G = qf_ref.shape[0] // C
    in_refs = ((qf_ref, kf_ref, vf_ref, gf_ref), (qb_ref, kb_ref, vb_ref, gb_ref))
    o_refs = (of_ref, ob_ref)

    @pl.when(pl.program_id(1) == 0)
    def _():
        s_ref[...] = jnp.zeros_like(s_ref)

    ri = lax.broadcasted_iota(jnp.int32, (C, C), 0)
    ci = lax.broadcasted_iota(jnp.int32, (C, C), 1)
    eye = (ri == ci).astype(F32)
    incl = ((ci <= ri), (ci >= ri))
    strict = ((ci < ri), (ci > ri))
    rows = lambda g: slice(g * C, (g + 1) * C)
    cols = lambda h: slice(h * GDN_DK, (h + 1) * GDN_DK)
    chains = [(d, g, h) for d in range(2) for g in range(G) for h in range(H)]

    gbv = {(d, g): in_refs[d][3][rows(g), :] for d in range(2) for g in range(G)}
    gc_all = {dg: jnp.dot(incl[dg[0]].astype(F32), gbv[dg], precision=HI, preferred_element_type=F32)
              for dg in gbv}
    gc_t = {dg: gc_all[dg].T for dg in gbv}

    kk, gamma, rhs, qe, kdec, gend = {}, {}, {}, {}, {}, {}
    for ch in chains:
        d, g, h = ch
        c = d * H + h
        q, k, v = (in_refs[d][n][rows(g), cols(h)] for n in range(3))
        end = C - 1 if d == 0 else 0
        beta = gbv[d, g][:, 2 * H + c:2 * H + c + 1]
        gc_c = gc_all[d, g][:, c:c + 1]
        gc_r = gc_t[d, g][c:c + 1, :]
        g_tot = gc_all[d, g][end:end + 1, c:c + 1]
        gamma[ch] = jnp.where(incl[d], jnp.exp(jnp.where(incl[d], gc_c - gc_r, 0.0)), 0.0)
        e_c = jnp.exp(gc_c)
        kb = k * beta
        kk[ch] = _bdot_nt(jnp.concatenate([kb, q], axis=0), k)
        rhs[ch] = jnp.concatenate([v * beta, kb * e_c], axis=1).astype(BF16)
        qe[ch] = q * e_c
        kdec[ch] = (k * jnp.exp(g_tot - gc_c)).astype(BF16)
        gend[ch] = jnp.exp(g_tot)
    m = {ch: jnp.where(strict[ch[0]], kk[ch][:C] * gamma[ch], 0.0) for ch in chains}
    a_intra = {ch: (kk[ch][C:] * gamma[ch]).astype(BF16) for ch in chains}
    t = {ch: eye - m[ch] for ch in chains}
    pw = m
    for _ in range(5):
        pw = {ch: _bdot(pw[ch], pw[ch]) for ch in chains}
        t = {ch: t[ch] + _bdot(t[ch], pw[ch]) for ch in chains}
    uw = {ch: jnp.dot(t[ch].astype(BF16), rhs[ch], preferred_element_type=F32) for ch in chains}
    wq = {ch: jnp.concatenate([uw[ch][:, GDN_DV:], qe[ch]], axis=0).astype(BF16) for ch in chains}

    heads = [(d, h) for d in range(2) for h in range(H)]
    s = {dh: s_ref[dh[0] * H + dh[1]] for dh in heads}
    for j in range(G):
        at = lambda dh: (dh[0], j if dh[0] == 0 else G - 1 - j, dh[1])
        ws = {dh: jnp.dot(wq[at(dh)], s[dh].astype(BF16), preferred_element_type=F32) for dh in heads}
        v_new = {dh: uw[at(dh)][:, :GDN_DV] - ws[dh][:C] for dh in heads}
        for dh in heads:
            d, g, h = at(dh)
            o_refs[d][rows(g), cols(h)] = ws[dh][C:] + jnp.dot(a_intra[at(dh)], v_new[dh].astype(BF16),
                                                               preferred_element_type=F32)
        s = {dh: s[dh] * gend[at(dh)] + _bdot_tn(kdec[at(dh)], v_new[dh]) for dh in heads}
    for dh in heads:
        s_ref[dh[0] * H + dh[1]] = s[dh]


def gdn_scan(q, k, v, gb, n_ctx_chunks):
    B, Lt, NV = q.shape
    R = GDN_CHUNK * GDN_STEP_CHUNKS
    n = Lt // R
    n_ctx = n_ctx_chunks // GDN_STEP_CHUNKS
    n_lat = n - n_ctx
    assert n * R == Lt and n_ctx * GDN_STEP_CHUNKS == n_ctx_chunks

    def fwd(b, i):
        return (b, jnp.where(i < n_ctx, n_lat + i, i - n_ctx), 0)

    def bwd(b, i):
        return (b, n - 1 - i, 0)

    def specs(imap):
        return [pl.BlockSpec((None, R, NV), imap)] * 3 + [pl.BlockSpec((None, R, LANES), imap)]

    return pl.pallas_call(
        _gdn_scan_kernel,
        grid=(B, n),
        in_specs=specs(fwd) + specs(bwd),
        out_specs=[pl.BlockSpec((None, R, NV), fwd), pl.BlockSpec((None, R, NV), bwd)],
        out_shape=[jax.ShapeDtypeStruct((B, Lt, NV), F32)] * 2,
        scratch_shapes=[pltpu.VMEM((2 * GDN_HEADS, GDN_DK, GDN_DV), F32)],
        compiler_params=_cparams("parallel", "arbitrary"),
        name="gdn_scan",
    )(q, k, v, gb, q, k, v, gb)


MLA_QK = MLA_NOPE + MLA_ROPE
MLA_KVIN = MLA_KV_LORA + LANES


def _rope_partner(w_rope):
    nf = MLA_ROPE // 4
    parts = []
    for half in range(2):
        a = w_rope[..., half * 2 * nf:half * 2 * nf + nf]
        b = w_rope[..., half * 2 * nf + nf:(half + 1) * 2 * nf]
        parts += [-b, a]
    return jnp.concatenate(parts, axis=-1)


def _head_pad(nope, rope):
    pad = jnp.zeros(nope.shape[:-1] + (LANES - MLA_QK,), nope.dtype)
    out = jnp.concatenate([nope, rope, pad], axis=-1)
    return out.reshape(out.shape[:-2] + (MLA_HEADS * LANES,))


def mla_weights(w_uq, w_ukv):
    wq = w_uq.reshape(MLA_Q_LORA, MLA_HEADS, MLA_QK)
    qn, qr = wq[..., :MLA_NOPE], wq[..., MLA_NOPE:]
    wq2 = jnp.concatenate([_head_pad(qn, qr), _head_pad(jnp.zeros_like(qn), _rope_partner(qr))], axis=-1)
    wkv = w_ukv.reshape(MLA_KV_LORA, MLA_HEADS, MLA_NOPE + MLA_V)
    kn, vv = wkv[..., :MLA_NOPE], wkv[..., MLA_NOPE:]
    eye = jnp.broadcast_to(jnp.eye(MLA_ROPE, dtype=F32)[:, None, :], (MLA_ROPE, MLA_HEADS, MLA_ROPE))
    z_kn = jnp.zeros((MLA_ROPE, MLA_HEADS, MLA_NOPE), F32)
    z_rope = jnp.zeros((MLA_KV_LORA, MLA_HEADS, MLA_ROPE), F32)
    top = jnp.concatenate([_head_pad(kn, z_rope), _head_pad(jnp.zeros_like(kn), z_rope), _head_pad(vv, z_rope)],
                          axis=-1)
    mid = jnp.concatenate([_head_pad(z_kn, eye), _head_pad(z_kn, _rope_partner(eye)),
                           jnp.zeros((MLA_ROPE, MLA_HEADS * LANES), F32)], axis=-1)
    bot = jnp.zeros((MLA_KVIN - MLA_KV_LORA - MLA_ROPE, top.shape[1]), F32)
    return wq2.astype(BF16), jnp.concatenate([top, mid, bot], axis=0).astype(BF16)


def rope_tables(row, col):
    nf = MLA_ROPE // 4
    inv_freq = ROPE_THETA ** (-jnp.arange(nf, dtype=F32) / nf)
    ang = jnp.concatenate([row.astype(F32)[:, None] * inv_freq[None, :]] * 2
                          + [col.astype(F32)[:, None] * inv_freq[None, :]] * 2, axis=-1)
    n = ang.shape[0]
    pad = jnp.zeros((n, LANES - MLA_QK), F32)
    cos = jnp.concatenate([jnp.ones((n, MLA_NOPE), F32), jnp.cos(ang), pad], axis=-1)
    sin = jnp.concatenate([jnp.zeros((n, MLA_NOPE), F32), jnp.sin(ang), pad], axis=-1)
    return cos, sin


def _mla_proj_kernel(p_ref, gq_ref, gkv_ref, wq_ref, wk_ref, cos_ref, sin_ref, q_ref, k_ref, v_ref):
    HL = MLA_HEADS * LANES
    cos = jnp.concatenate([cos_ref[...]] * MLA_HEADS, axis=1)
    sin = jnp.concatenate([sin_ref[...]] * MLA_HEADS, axis=1)
    cq = p_ref[:, :MLA_Q_LORA]
    cqn = cq * lax.rsqrt(jnp.mean(cq * cq, axis=-1, keepdims=True) + EPS) * gq_ref[...]
    qq = jnp.dot(cqn.astype(BF16), wq_ref[...], preferred_element_type=F32)
    q_ref[...] = ((qq[:, :HL] * cos + qq[:, HL:] * sin) * (MLA_QK ** -0.5 * math.log2(math.e))).astype(q_ref.dtype)
    ck = p_ref[:, MLA_Q_LORA:]
    lane = lax.broadcasted_iota(jnp.int32, ck.shape, 1)
    is_kv = lane < MLA_KV_LORA
    ms = jnp.sum(jnp.where(is_kv, ck * ck, 0.0), axis=-1, keepdims=True) * (1.0 / MLA_KV_LORA)
    ckn = jnp.where(is_kv, ck * lax.rsqrt(ms + EPS) * gkv_ref[...], ck)
    kk = jnp.dot(ckn.astype(BF16), wk_ref[...], preferred_element_type=F32)
    k_ref[...] = (kk[:, :HL] * cos + kk[:, HL:2 * HL] * sin).astype(k_ref.dtype)
    vv = kk[:, 2 * HL:]
    vlane = lax.broadcasted_iota(jnp.int32, vv.shape, 1)
    v_ref[...] = jnp.where(vlane % LANES == MLA_V, 1.0, vv).astype(v_ref.dtype)


def mla_proj(p_mla, gq, gkv, wq2, wk2, cos, sin, tm):
    B, S, C = p_mla.shape
    tm = min(tm, S)
    HL = MLA_HEADS * LANES
    gkv = jnp.pad(gkv, ((0, 0), (0, MLA_KVIN - MLA_KV_LORA)))
    row = lambda n: pl.BlockSpec((None, tm, n), lambda b, i: (b, i, 0))
    full = lambda a: pl.BlockSpec(a.shape, lambda b, i: (0, 0))
    tab = pl.BlockSpec((tm, LANES), lambda b, i: (i, 0))
    return pl.pallas_call(
        _mla_proj_kernel,
        grid=(B, S // tm),
        in_specs=[row(C), full(gq), full(gkv), full(wq2), full(wk2), tab, tab],
        out_specs=[row(HL)] * 3,
        out_shape=[jax.ShapeDtypeStruct((B, S, HL), BF16)] * 3,
        compiler_params=_cparams("parallel", "parallel"),
        name="mla_proj",
    )(p_mla, gq, gkv, wq2, wk2, cos, sin)


ATT_SLAB = 32


def _mla_attn_kernel(q_ref, k_ref, v_ref, o_ref, *, tk):
    tq = q_ref.shape[0]
    Tk = k_ref.shape[0]
    n_full, rem = Tk // tk, Tk % tk
    heads = (slice(0, LANES), slice(LANES, 2 * LANES))
    qs = [q_ref[:, hs] for hs in heads]

    def step(carry, start, size):
        ss = [lax.dot_general(qs[h], k_ref[pl.ds(start, size), heads[h]], (((1,), (1,)), ((), ())),
                              preferred_element_type=F32) for h in range(2)]
        out = []
        for h in range(2):
            m, acc = carry[h]
            m_new = jnp.maximum(m, jnp.max(ss[h], axis=-1, keepdims=True))
            p = jnp.concatenate([jnp.exp2(ss[h][r:r + ATT_SLAB] - m_new[r:r + ATT_SLAB]).astype(BF16)
                                 for r in range(0, tq, ATT_SLAB)], axis=0)
            acc = acc * jnp.exp2(m - m_new) + jnp.dot(p, v_ref[pl.ds(start, size), heads[h]],
                                                      preferred_element_type=F32)
            out.append((m_new, acc))
        return tuple(out)

    carry = tuple((jnp.full((tq, 1), -jnp.inf, F32), jnp.zeros((tq, LANES), F32)) for _ in range(2))
    if n_full:
        carry = lax.fori_loop(0, n_full, lambda c, cr: step(cr, pl.multiple_of(c * tk, tk), tk), carry,
                              unroll=2 if n_full % 2 == 0 else 1)
    if rem:
        carry = step(carry, n_full * tk, rem)
    o0, o1 = [acc / acc[:, MLA_V:MLA_V + 1] for _, acc in carry]
    lane = lax.broadcasted_iota(jnp.int32, (tq, LANES), 1)
    o_ref[...] = jnp.where(lane < MLA_V, o0, pltpu.roll(o1, MLA_V, 1)).astype(o_ref.dtype)


def mla_attn(q, k, v, q_rows, k_rows, tq, tk):
    B = q.shape[0]
    (q0, S), (k0, Tk) = q_rows, k_rows
    tq = min(tq, S)
    assert q0 % tq == 0 and k0 % Tk == 0
    qb, kb = q0 // tq, k0 // Tk
    return pl.pallas_call(
        functools.partial(_mla_attn_kernel, tk=tk),
        grid=(B, MLA_HEADS // 2, S // tq),
        in_specs=[pl.BlockSpec((None, tq, 2 * LANES), lambda b, h, i: (b, qb + i, h)),
                  pl.BlockSpec((None, Tk, 2 * LANES), lambda b, h, i: (b, kb, h)),
                  pl.BlockSpec((None, Tk, 2 * LANES), lambda b, h, i: (b, kb, h))],
        out_specs=pl.BlockSpec((None, tq, LANES), lambda b, h, i: (b, i, h)),
        out_shape=jax.ShapeDtypeStruct((B, S, MLA_HEADS * MLA_V), BF16),
        compiler_params=_cparams("parallel", "parallel", "arbitrary"),
        name="mla_attn",
    )(q, k, v)


def _route(logits, rb):
    lane = lax.broadcasted_iota(jnp.int32, logits.shape, 1)
    neg = -jnp.inf
    scores = jax.nn.sigmoid(logits)
    sel = scores + rb

    def top2(masked):
        m1 = jnp.max(masked, axis=-1, keepdims=True)
        i1 = jnp.min(jnp.where(masked == m1, lane, LANES), axis=-1, keepdims=True)
        rest = jnp.where(lane == i1, neg, masked)
        m2 = jnp.max(rest, axis=-1, keepdims=True)
        i2 = jnp.min(jnp.where(rest == m2, lane, LANES), axis=-1, keepdims=True)
        return m1, i1, m2, i2

    best = None
    for gi in range(N_GROUPS):
        in_g = jnp.logical_and(lane >= gi * EXPERTS_PER_GROUP, lane < (gi + 1) * EXPERTS_PER_GROUP)
        m1, _, m2, _ = top2(jnp.where(in_g, sel, neg))
        gs = m1 + m2
        if best is None:
            best, grp = gs, jnp.zeros_like(gs, dtype=jnp.int32)
        else:
            better = gs > best
            grp = jnp.where(better, gi, grp)
            best = jnp.where(better, gs, best)
    lo = grp * EXPERTS_PER_GROUP
    in_grp = jnp.logical_and(lane >= lo, lane < lo + EXPERTS_PER_GROUP)
    _, i1, _, i2 = top2(jnp.where(in_grp, sel, neg))
    picked = jnp.where(jnp.logical_or(lane == i1, lane == i2), scores, 0.0)
    return picked / jnp.sum(picked, axis=-1, keepdims=True)


def _merge_kernel(hyv_ref, of_ref, ob_ref, pg_ref, at_ref, x_ref, gt1_ref, sc2_ref, sh2_ref, gng_ref, n2g_ref,
                  whb_ref, wgd_ref, wml_ref, wo_ref, rw_ref, rb_ref, xo_ref, h2_ref, gate_ref):
    nv = GDN_HEADS * GDN_DV
    D = x_ref.shape[1]
    o = of_ref[...] + ob_ref[...]
    z = pg_ref[:, :nv]
    ys = []
    for h in range(GDN_HEADS):
        sl = slice(h * GDN_DV, (h + 1) * GDN_DV)
        oh = o[:, sl]
        on = oh * lax.rsqrt(jnp.mean(oh * oh, axis=-1, keepdims=True) + EPS) * gng_ref[...]
        ys.append(on * _silu(z[:, sl]))
    y_gdn = jnp.dot(jnp.concatenate(ys, axis=1).astype(BF16), wgd_ref[...], preferred_element_type=F32)
    y_hy = jnp.dot(hyv_ref[...].astype(BF16), whb_ref[...], preferred_element_type=F32)
    y_mla = jnp.dot(at_ref[...], wml_ref[...], preferred_element_type=F32)
    merged = (jax.nn.sigmoid(pg_ref[:, nv:nv + D]) * y_hy
              + jax.nn.sigmoid(pg_ref[:, nv + D:nv + 2 * D]) * y_gdn
              + jax.nn.sigmoid(pg_ref[:, nv + 2 * D:]) * y_mla)
    mix = jnp.dot(merged.astype(BF16), wo_ref[...], preferred_element_type=F32)
    xn = x_ref[...] + gt1_ref[...] * mix
    xo_ref[...] = xn
    y2 = xn * lax.rsqrt(jnp.mean(xn * xn, axis=-1, keepdims=True) + EPS) * n2g_ref[...]
    h2 = y2 * (1.0 + sc2_ref[...]) + sh2_ref[...]
    h2_ref[...] = h2.astype(h2_ref.dtype)
    logits = jnp.dot(h2, rw_ref[...], precision=HI, preferred_element_type=F32)
    gate_ref[...] = _route(logits, rb_ref[...])


def merge_out(hyv, o_f, o_b, pg, attn, x, gt1, sc2, sh2, gdn_norm_g, norm2_g, w_hy, w_gdn, w_mla, w_out,
              router_w, router_b, tm, o_row0):
    B, S, D = x.shape
    tm = min(tm, S)
    assert o_row0 % tm == 0
    ob0 = o_row0 // tm
    row = lambda n: pl.BlockSpec((None, tm, n), lambda b, i: (b, i, 0))
    full = lambda a: pl.BlockSpec(a.shape, lambda b, i: (0, 0))
    nv = GDN_HEADS * GDN_DV
    o_spec = pl.BlockSpec((None, tm, nv), lambda b, i: (b, ob0 + i, 0))
    return pl.pallas_call(
        _merge_kernel,
        grid=(B, S // tm),
        in_specs=[row(HY_W), o_spec, o_spec, row(pg.shape[2]), row(MLA_HEADS * MLA_V), row(D),
                  _mod_spec(gt1, D), _mod_spec(sc2, D), _mod_spec(sh2, D),
                  full(gdn_norm_g), full(norm2_g), full(w_hy), full(w_gdn), full(w_mla), full(w_out),
                  full(router_w), full(router_b)],
        out_specs=[row(D), row(D), row(LANES)],
        out_shape=[jax.ShapeDtypeStruct((B, S, D), F32), jax.ShapeDtypeStruct((B, S, D), BF16),
                   jax.ShapeDtypeStruct((B, S, LANES), F32)],
        compiler_params=_cparams("parallel", "parallel"),
        name="merge_out",
    )(hyv, o_f, o_b, pg, attn, x, gt1, sc2, sh2, gdn_norm_g, norm2_g, w_hy, w_gdn, w_mla, w_out,
      router_w, router_b)


def _moe_kernel(h_ref, gate_ref, w1_ref, w3_ref, w2_ref, x_ref, gt2_ref, fg_ref, o_ref, acc_ref, *, final_norm):
    e = pl.program_id(1)

    @pl.when(e == 0)
    def _():
        acc_ref[...] = jnp.zeros_like(acc_ref)

    h = h_ref[...]
    he = _silu(jnp.dot(h, w1_ref[...], preferred_element_type=F32)) * jnp.dot(h, w3_ref[...], preferred_element_type=F32)
    gate = gate_ref[...]
    lane = lax.broadcasted_iota(jnp.int32, gate.shape, 1)
    ge = jnp.sum(jnp.where(lane == e, gate, 0.0), axis=-1, keepdims=True)
    acc_ref[...] += ge * jnp.dot(he.astype(BF16), w2_ref[...], preferred_element_type=F32)

    @pl.when(e == pl.num_programs(1) - 1)
    def _():
        xn = x_ref[...] + gt2_ref[...] * acc_ref[...]
        if final_norm:
            xn = xn * lax.rsqrt(jnp.mean(xn * xn, axis=-1, keepdims=True) + EPS) * fg_ref[...]
        o_ref[...] = xn


def moe(h2, gate, w1, w3, w2, x, gt2, final_g, S, tm, final_norm):
    T, D = x.shape
    E, _, FF = w1.shape
    tm = min(tm, S)
    per_b = S // tm
    if gt2.shape[0] == 1:
        gt_spec = pl.BlockSpec((None, 1, D), lambda i, e: (0, 0, 0))
    else:
        gt_spec = pl.BlockSpec((None, 1, D), lambda i, e: (i // per_b, 0, 0))
    return pl.pallas_call(
        functools.partial(_moe_kernel, final_norm=final_norm),
        grid=(T // tm, E),
        in_specs=[pl.BlockSpec((tm, D), lambda i, e: (i, 0)),
                  pl.BlockSpec((tm, LANES), lambda i, e: (i, 0)),
                  pl.BlockSpec((None, D, FF), lambda i, e: (e, 0, 0)),
                  pl.BlockSpec((None, D, FF), lambda i, e: (e, 0, 0)),
                  pl.BlockSpec((None, FF, D), lambda i, e: (e, 0, 0)),
                  pl.BlockSpec((tm, D), lambda i, e: (i, 0)),
                  gt_spec,
                  pl.BlockSpec((1, D), lambda i, e: (0, 0))],
        out_specs=pl.BlockSpec((tm, D), lambda i, e: (i, 0)),
        out_shape=jax.ShapeDtypeStruct((T, D), F32),
        scratch_shapes=[pltpu.VMEM((tm, D), F32)],
        compiler_params=_cparams("parallel", "arbitrary"),
        name="moe",
    )(h2, gate, w1, w3, w2, x, gt2, final_g)


IN_SIZES = (3 * HY_W, GDN_QKV, GDN_HEADS * GDN_DV, 2 * GDN_HEADS, 2 * GDN_HEADS, MLA_Q_LORA, MLA_KV_LORA, MLA_ROPE)


def _split_w_in(w_in):
    D = w_in.shape[0]
    parts, off = [], 0
    for n in IN_SIZES:
        parts.append(w_in[:, off:off + n])
        off += n
    hy, qkv, z, a, b, cq, ckv, kr = parts
    gate = w_in[:, off:]
    zeros = lambda n: jnp.zeros((D, n), w_in.dtype)
    w_hy = hy
    w_gdn = jnp.concatenate([qkv, a, b, zeros(LANES - 4 * GDN_HEADS)], axis=1)
    w_mla = jnp.concatenate([cq, ckv, kr, zeros(LANES - MLA_ROPE)], axis=1)
    w_gate = jnp.concatenate([z, gate], axis=1)
    return [w.astype(BF16) for w in (w_hy, w_gdn, w_mla, w_gate)]


def _layer(x, cx, mod, mod_c, lw, tabs, router_w, router_b, final_g, update_ctx, last):
    B, S, D = x.shape
    Lc = cx.shape[1]
    sh1, sc1, gt1, sh2, sc2, gt2 = [m[:, None, :] for m in jnp.split(mod, 6, axis=-1)]
    csh1, csc1, cgt1, csh2, csc2, cgt2 = [m[:, None, :] for m in jnp.split(mod_c, 6, axis=-1)]
    n1g = lw['norm1_g'][None, :]
    w_hy, w_gdn, w_mla, w_gate = _split_w_in(lw['w_in'])

    Lt = S + Lc
    TM, TMC = 512, 256

    def project(w, joint):
        if joint:
            lat = in_proj(x, n1g, sc1, sh1, w, TM, rows_total=Lt)
            return in_proj(cx, n1g, csc1, csh1, w, TMC, into=lat)
        return in_proj(x, n1g, sc1, sh1, w, TM), in_proj(cx, n1g, csc1, csh1, w, TMC)

    p_hy, c_hy = project(w_hy, False)
    p_gate, c_gate = project(w_gate, False)
    pc_gdn = project(w_gdn, True)
    pc_mla = project(w_mla, True)

    o_f, o_b = gdn_scan(*gdn_pre(pc_gdn, lw['gdn_conv_w'], lw['gdn_a_log'], lw['gdn_dt_bias'], TMC, S),
                        Lc // GDN_CHUNK)

    wq2, wk2 = mla_weights(lw['mla_w_uq'], lw['mla_w_ukv'])
    gq, gkv = lw['mla_q_norm_g'][None, :], lw['mla_kv_norm_g'][None, :]
    q_a, k_a, v_a = mla_proj(pc_mla, gq, gkv, wq2, wk2, tabs[0], tabs[1], TMC)
    attn_l = mla_attn(q_a, k_a, v_a, (0, S), (0, Lt), 512, 1024)

    def hyena(p, L):
        filt = hy_filter(L, lw['hy_f_w1'], lw['hy_f_b1'], lw['hy_f_w2'], lw['hy_f_b2'], lw['hy_f_w3'],
                         lw['hy_f_freq'], lw['hy_decay'])
        x0, u = hy_pre(p, lw['hy_conv_w'], lw['hy_conv_b'][None, :], 512)
        return hyena_long_conv(x0, u, filt, lw['hy_bias'])

    hyv_l = hyena(p_hy, S)

    wb = lambda name: lw[name].astype(BF16)
    rw = jnp.pad(router_w, ((0, 0), (0, LANES - N_EXPERTS)))
    rb = jnp.pad(router_b[None, :], ((0, 0), (0, LANES - N_EXPERTS)))
    n2g, gng = lw['norm2_g'][None, :], lw['gdn_norm_g'][None, :]
    w1, w3, w2 = wb('moe_w1'), wb('moe_w3'), wb('moe_w2')

    def finish(xx, hyv, o_row0, pg, attn, gt1_, sc2_, sh2_, gt2_, tm, tm_moe, fin):
        Bx, Sx, _ = xx.shape
        xn, h2, gate = merge_out(hyv, o_f, o_b, pg, attn, xx, gt1_, sc2_, sh2_, gng, n2g, wb('hy_out'), wb('gdn_out'),
                                 wb('mla_out'), wb('w_out'), rw, rb, tm, o_row0)
        out = moe(h2.reshape(Bx * Sx, D), gate.reshape(Bx * Sx, LANES), w1, w3, w2, xn.reshape(Bx * Sx, D),
                  gt2_, final_g, Sx, tm_moe, fin)
        return out.reshape(Bx, Sx, D)

    x_new = finish(x, hyv_l, 0, p_gate, attn_l, gt1, sc2, sh2, gt2, 512, 1024, last)
    if update_ctx:
        hyv_c = hyena(c_hy, Lc)
        attn_c = mla_attn(q_a, k_a, v_a, (S, Lc), (S, Lc), 256, 1024)
        cx = finish(cx, hyv_c, S, c_gate, attn_c, cgt1, csc2, csh2, cgt2, 256, 256, False)
    return x_new, cx


def kernel(x, c, ctx, c_ctx, w_ada, b_ada, norm1_g, norm2_g, w_in, hy_conv_w, hy_conv_b, hy_f_w1, hy_f_b1, hy_f_w2, hy_f_b2, hy_f_w3, hy_f_freq, hy_decay, hy_bias, hy_out, gdn_conv_w, gdn_a_log, gdn_dt_bias, gdn_norm_g, gdn_out, mla_q_norm_g, mla_w_uq, mla_kv_norm_g, mla_w_ukv, mla_out, w_out, moe_w1, moe_w3, moe_w2, router_w, router_b, final_norm_g):
    per_layer = dict(norm1_g=norm1_g, norm2_g=norm2_g, w_in=w_in, hy_conv_w=hy_conv_w, hy_conv_b=hy_conv_b,
                     hy_f_w1=hy_f_w1, hy_f_b1=hy_f_b1, hy_f_w2=hy_f_w2, hy_f_b2=hy_f_b2, hy_f_w3=hy_f_w3,
                     hy_f_freq=hy_f_freq, hy_decay=hy_decay, hy_bias=hy_bias, hy_out=hy_out,
                     gdn_conv_w=gdn_conv_w, gdn_a_log=gdn_a_log, gdn_dt_bias=gdn_dt_bias, gdn_norm_g=gdn_norm_g,
                     gdn_out=gdn_out, mla_q_norm_g=mla_q_norm_g, mla_w_uq=mla_w_uq, mla_kv_norm_g=mla_kv_norm_g,
                     mla_w_ukv=mla_w_ukv, mla_out=mla_out, w_out=w_out, moe_w1=moe_w1, moe_w3=moe_w3, moe_w2=moe_w2)
    B, S, D = x.shape
    Lc = ctx.shape[1]
    depth = w_ada.shape[0]
    rows = S // GRID_W
    row = jnp.repeat(jnp.arange(rows, dtype=jnp.int32), GRID_W)
    col = jnp.tile(jnp.arange(GRID_W, dtype=jnp.int32), rows)
    zero = jnp.zeros((Lc,), jnp.int32)
    tabs = rope_tables(jnp.concatenate([row, zero]), jnp.concatenate([col, zero]))
    cc = jnp.concatenate([c, c_ctx[None, :], jnp.zeros((2 * SUBLANES - B - 1, D), F32)], axis=0)
    final_g = final_norm_g[None, :]
    cx = ctx
    for l in range(depth):
        lw = {k: v[l] for k, v in per_layer.items()}
        mods = ada_mod(cc, w_ada[l], b_ada[l][None, :])
        x, cx = _layer(x, cx, mods[:B], mods[B:B + 1], lw, tabs, router_w, router_b, final_g,
                       l < depth - 1, l == depth - 1)
    return x
```

```python
import functools
import math

import jax
import jax.numpy as jnp
from jax import lax
from jax.experimental import pallas as pl
from jax.experimental.pallas import tpu as pltpu

F32 = jnp.float32
BF16 = jnp.bfloat16
HI = lax.Precision.HIGHEST
EPS = 1e-6

GRID_W = 64
HY_W = 512
HY_EMB = 33
HY_BANDS = (HY_EMB - 1) // 2
HY_MOD_SHIFT = 0.05
GDN_HEADS = 4
GDN_DK = 128
GDN_DV = 128
GDN_CHUNK = 64
MLA_HEADS = 8
MLA_NOPE = 64
MLA_ROPE = 32
MLA_V = 64
MLA_Q_LORA = 768
MLA_KV_LORA = 256
ROPE_THETA = 10000.0
N_EXPERTS = 16
N_GROUPS = 4
EXPERTS_PER_GROUP = N_EXPERTS // N_GROUPS
EXPERT_FF = 512
LANES = 128
SUBLANES = 8
VMEM_LIMIT = 56 * 1024 * 1024


def _cparams(*sem):
    return pltpu.CompilerParams(dimension_semantics=sem, vmem_limit_bytes=VMEM_LIMIT)


def _silu(x):
    return x * jax.nn.sigmoid(x)


def _ada_kernel(c_ref, w_ref, b_ref, o_ref):
    a = _silu(c_ref[...])
    o_ref[...] = jnp.dot(a, w_ref[...], precision=HI, preferred_element_type=F32) + b_ref[...]


def ada_mod(cc, w, b):
    R, D = cc.shape
    N = w.shape[1]
    tn = 1536
    return pl.pallas_call(
        _ada_kernel,
        grid=(N // tn,),
        in_specs=[pl.BlockSpec((R, D), lambda j: (0, 0)),
                  pl.BlockSpec((D, tn), lambda j: (0, j)),
                  pl.BlockSpec((1, tn), lambda j: (0, j))],
        out_specs=pl.BlockSpec((R, tn), lambda j: (0, j)),
        out_shape=jax.ShapeDtypeStruct((R, N), F32),
        compiler_params=_cparams("parallel"),
        name="ada_mod",
    )(cc, w, b)


def _inproj_kernel(x_ref, g_ref, sc_ref, sh_ref, w_ref, o_ref):
    x = x_ref[...]
    y = x * lax.rsqrt(jnp.mean(x * x, axis=-1, keepdims=True) + EPS) * g_ref[...]
    h = y * (1.0 + sc_ref[...]) + sh_ref[...]
    o_ref[...] = jnp.dot(h.astype(BF16), w_ref[...], preferred_element_type=F32).astype(o_ref.dtype)


def _mod_spec(m, D):
    if m.shape[0] == 1:
        return pl.BlockSpec((None, 1, D), lambda b, i: (0, 0, 0))
    return pl.BlockSpec((None, 1, D), lambda b, i: (b, 0, 0))


def _inproj_into_kernel(x_ref, g_ref, sc_ref, sh_ref, w_ref, dst_ref, o_ref):
    del dst_ref
    _inproj_kernel(x_ref, g_ref, sc_ref, sh_ref, w_ref, o_ref)


def in_proj(x, g, sc, sh, w, tm, rows_total=None, into=None):
    B, S, D = x.shape
    N = w.shape[1]
    tm = min(tm, S)
    in_specs = [pl.BlockSpec((None, tm, D), lambda b, i: (b, i, 0)),
                pl.BlockSpec((1, D), lambda b, i: (0, 0)),
                _mod_spec(sc, D), _mod_spec(sh, D),
                pl.BlockSpec((D, N), lambda b, i: (0, 0))]
    args = [x, g, sc, sh, w]
    if into is None:
        body, rows, off, alias = _inproj_kernel, rows_total or S, 0, {}
    else:
        body, rows, alias = _inproj_into_kernel, into.shape[1], {len(args): 0}
        off = (rows - S) // tm
        in_specs.append(pl.BlockSpec(memory_space=pl.ANY))
        args.append(into)
    return pl.pallas_call(
        body,
        grid=(B, S // tm),
        in_specs=in_specs,
        out_specs=pl.BlockSpec((None, tm, N), lambda b, i: (b, i + off, 0)),
        out_shape=jax.ShapeDtypeStruct((B, rows, N), F32),
        input_output_aliases=alias,
        compiler_params=_cparams("parallel", "parallel"),
        name="in_proj",
    )(*args)


def _halo_specs(tm, S, C, col_block=0):
    nb8 = tm // SUBLANES
    last8 = S // SUBLANES - 1
    main = pl.BlockSpec((None, tm, C), lambda b, i: (b, i, col_block))
    prev = pl.BlockSpec((None, SUBLANES, C), lambda b, i: (b, jnp.maximum(i * nb8 - 1, 0), col_block))
    nxt = pl.BlockSpec((None, SUBLANES, C), lambda b, i: (b, jnp.minimum((i + 1) * nb8, last8), col_block))
    return main, prev, nxt


def _conv3(x, prev8, next8, w, first, last):
    tm = x.shape[0]
    row = lax.broadcasted_iota(jnp.int32, x.shape, 0)
    p_row = jnp.where(first, 0.0, prev8[SUBLANES - 1:SUBLANES, :])
    n_row = jnp.where(last, 0.0, next8[0:1, :])
    x_prev = jnp.where(row == 0, p_row, pltpu.roll(x, 1, 0))
    x_next = jnp.where(row == tm - 1, n_row, pltpu.roll(x, tm - 1, 0))
    return x_prev * w[0:1, :] + x * w[1:2, :] + x_next * w[2:3, :]


def _hy_pre_kernel(p_ref, pp_ref, pn_ref, w_ref, b_ref, x0_ref, u_ref):
    i = pl.program_id(1)
    y = _conv3(p_ref[...], pp_ref[...], pn_ref[...], w_ref[...], i == 0, i == pl.num_programs(1) - 1)
    y = y + b_ref[...]
    x0_ref[...] = y[:, :HY_W]
    u_ref[...] = y[:, HY_W:2 * HY_W] * y[:, 2 * HY_W:]


def hy_pre(p_hy, conv_w, conv_b, tm):
    B, S, C = p_hy.shape
    tm = min(tm, S)
    main, prev, nxt = _halo_specs(tm, S, C)
    o_spec = pl.BlockSpec((None, tm, HY_W), lambda b, i: (b, i, 0))
    return pl.pallas_call(
        _hy_pre_kernel,
        grid=(B, S // tm),
        in_specs=[main, prev, nxt,
                  pl.BlockSpec((3, C), lambda b, i: (0, 0)),
                  pl.BlockSpec((1, C), lambda b, i: (0, 0))],
        out_specs=[o_spec, o_spec],
        out_shape=[jax.ShapeDtypeStruct((B, S, HY_W), F32)] * 2,
        compiler_params=_cparams("parallel", "parallel"),
        name="hy_pre",
    )(p_hy, p_hy, p_hy, conv_w, conv_b)


def _hy_filter_kernel(z_ref, w1_ref, b1_ref, w2_ref, b2_ref, w3_ref, fq_ref, dc_ref, o_ref):
    fq = fq_ref[...]
    h = jnp.sin(fq * (jnp.dot(z_ref[...], w1_ref[...], precision=HI, preferred_element_type=F32) + b1_ref[...]))
    h = jnp.sin(fq * (jnp.dot(h, w2_ref[...], precision=HI, preferred_element_type=F32) + b2_ref[...]))
    h = jnp.dot(h, w3_ref[...], precision=HI, preferred_element_type=F32)
    taps = h * (jnp.exp(-z_ref[:, 0:1] * jnp.abs(dc_ref[...])) + HY_MOD_SHIFT)
    C = o_ref.shape[1]
    is_fwd = z_ref[:, LANES - 2:LANES - 1] > 0.5
    o_ref[...] = jnp.where(is_fwd, taps[:, :C], taps[:, C:]) * z_ref[:, LANES - 1:LANES]


def hy_filter(L, w1, b1, w2, b2, w3, freq, decay):
    t = jnp.linspace(0.0, 1.0, L, dtype=F32)[:, None]
    w = (2.0 * math.pi / L) * jnp.arange(L, dtype=F32)[:, None]
    f = jnp.linspace(1e-4, HY_BANDS - 1, HY_BANDS, dtype=F32)[None, :]
    z = jnp.concatenate([t, jnp.cos(f * w), -jnp.sin(f * w)], axis=-1)
    emb_pad = LANES - HY_EMB
    lag = jnp.concatenate([jnp.arange(L), jnp.zeros((1,), jnp.int32), jnp.arange(L - 1, 0, -1)])
    r = jnp.arange(2 * L)
    flags = jnp.stack([(r < L).astype(F32), (r != L).astype(F32)], axis=1)
    z = jnp.concatenate([z[lag], jnp.zeros((2 * L, emb_pad - 2), F32), flags], axis=1)
    w1 = jnp.pad(w1, ((0, emb_pad), (0, 0)))
    tl = min(2 * L, 512)
    C = w3.shape[1] // 2
    full = lambda a: pl.BlockSpec(a.shape, lambda i: (0, 0))
    b1, b2, freq, decay = b1[None, :], b2[None, :], freq[None, :], decay[None, :]
    return pl.pallas_call(
        _hy_filter_kernel,
        grid=(2 * L // tl,),
        in_specs=[pl.BlockSpec((tl, LANES), lambda i: (i, 0)),
                  full(w1), full(b1), full(w2), full(b2), full(w3), full(freq), full(decay)],
        out_specs=pl.BlockSpec((tl, C), lambda i: (i, 0)),
        out_shape=jax.ShapeDtypeStruct((2 * L, C), F32),
        compiler_params=_cparams("parallel"),
        name="hy_filter",
    )(z, w1, b1, w2, b2, w3, freq, decay)


def _dft_tables(N1, N2):
    N = N1 * N2
    two_pi = 2.0 * math.pi

    def cs(num, den):
        ang = (two_pi / den) * (num % den).astype(F32)
        return jnp.cos(ang), jnp.sin(ang)

    a1 = jnp.arange(N1, dtype=jnp.int32)
    a2 = jnp.arange(N2, dtype=jnp.int32)
    c1, s1 = cs(a1[:, None] * a1[None, :], N1)
    c2, s2 = cs(a2[:, None] * a2[None, :], N2)
    ct, st = cs(a2[:, None] * a1[None, :], N)

    def stack(re, im):
        return jnp.concatenate([jnp.concatenate([re, -im], axis=-1),
                                jnp.concatenate([im, re], axis=-1)], axis=-2)

    tr = ct[:, :, None] * c1[None] - st[:, :, None] * s1[None]
    ti = -(ct[:, :, None] * s1[None] + st[:, :, None] * c1[None])
    h = N1 // 2
    m1_data = stack(tr[:, :, :h], ti[:, :, :h])
    m1_real = jnp.concatenate([tr, ti], axis=-2)
    m2 = stack(c2, -s2)
    ctk, stk = ct.T, st.T
    gr = ctk[:, :, None] * c2.T[None] - stk[:, :, None] * s2.T[None]
    gi = ctk[:, :, None] * s2.T[None] + stk[:, :, None] * c2.T[None]
    m2inv = stack(gr, gi)
    er, ei = c1.T[:h] / N, s1.T[:h] / N
    m3 = stack(er, ei)
    return [_hi_lo_rows(m) for m in (m1_data, m1_real, m2, m2inv, m3)]


def _hi_lo_rows(m):
    hi = m.astype(BF16)
    lo = (m - hi.astype(F32)).astype(BF16)
    return jnp.concatenate([hi, lo], axis=-2)


def _dot3(m2, x):
    M = m2.shape[0] // 2
    x_hi = x.astype(BF16)
    x_lo = (x - x_hi.astype(F32)).astype(BF16)
    a = jnp.dot(m2, x_hi, preferred_element_type=F32)
    return a[:M] + a[M:] + jnp.dot(m2[:M], x_lo, preferred_element_type=F32)


def _hy_pass_a_kernel(u_ref, m_ref, o_ref):
    n1 = o_ref.shape[1]
    for j in range(SUBLANES):
        xj = jnp.concatenate([u_ref[0, :, j, :], u_ref[1, :, j, :]], axis=0)
        a = _dot3(m_ref[j], xj)
        o_ref[0, :, j, :] = a[:n1]
        o_ref[1, :, j, :] = a[n1:]


def _hy_pass_a(u4, m1, n_pairs):
    _, h, N2, C = u4.shape
    N1 = 2 * h
    return pl.pallas_call(
        _hy_pass_a_kernel,
        grid=(n_pairs, N2 // SUBLANES),
        in_specs=[pl.BlockSpec((2, h, SUBLANES, C), lambda p, j: (p, 0, j, 0)),
                  pl.BlockSpec((SUBLANES, 4 * N1, N1), lambda p, j: (j, 0, 0))],
        out_specs=pl.BlockSpec((2, N1, SUBLANES, C), lambda p, j: (0, 0, j, p)),
        out_shape=jax.ShapeDtypeStruct((2, N1, N2, n_pairs * C), F32),
        compiler_params=_cparams("parallel", "parallel"),
        name="hy_pass_a",
    )(u4, m1)


def _hy_spec_kernel(a_ref, m2_ref, o_ref):
    n2 = a_ref.shape[2]
    for j in range(SUBLANES):
        a = jnp.concatenate([a_ref[0, j], a_ref[1, j]], axis=0)
        x = _dot3(m2_ref[...], a)
        o_ref[0, j] = x[:n2]
        o_ref[1, j] = x[n2:]


def _hy_spectrum(a, m2):
    _, N1, N2, C = a.shape
    spec = pl.BlockSpec((2, SUBLANES, N2, C), lambda k: (0, k, 0, 0))
    return pl.pallas_call(
        _hy_spec_kernel,
        grid=(N1 // SUBLANES,),
        in_specs=[spec, pl.BlockSpec((4 * N2, 2 * N2), lambda k: (0, 0))],
        out_specs=spec,
        out_shape=jax.ShapeDtypeStruct(a.shape, F32),
        compiler_params=_cparams("parallel"),
        name="hy_spectrum",
    )(a, m2)


def _hy_pass_b_kernel(a_ref, k_ref, m2_ref, mi_ref, o_ref):
    n2 = a_ref.shape[2]
    for j in range(SUBLANES):
        a = jnp.concatenate([a_ref[0, j], a_ref[1, j]], axis=0)
        x = _dot3(m2_ref[...], a)
        xr, xi = x[:n2], x[n2:]
        kr, ki = k_ref[0, j], k_ref[1, j]
        y = jnp.concatenate([xr * kr - xi * ki, xr * ki + xi * kr], axis=0)
        b = _dot3(mi_ref[j], y)
        o_ref[0, :, j, :] = b[:n2]
        o_ref[1, :, j, :] = b[n2:]


def _hy_pass_b(a, kf, m2, m2inv, n_pairs):
    _, N1, N2, PC = a.shape
    C = PC // n_pairs
    return pl.pallas_call(
        _hy_pass_b_kernel,
        grid=(n_pairs, N1 // SUBLANES),
        in_specs=[pl.BlockSpec((2, SUBLANES, N2, C), lambda p, k: (0, k, 0, p)),
                  pl.BlockSpec((2, SUBLANES, N2, C), lambda p, k: (0, k, 0, 0)),
                  pl.BlockSpec((4 * N2, 2 * N2), lambda p, k: (0, 0)),
                  pl.BlockSpec((SUBLANES, 4 * N2, 2 * N2), lambda p, k: (k, 0, 0))],
        out_specs=pl.BlockSpec((2, N2, SUBLANES, C), lambda p, k: (0, 0, k, p)),
        out_shape=jax.ShapeDtypeStruct((2, N2, N1, PC), F32),
        compiler_params=_cparams("parallel", "parallel"),
        name="hy_pass_b",
    )(a, kf, m2, m2inv)


def _hy_pass_c_kernel(b_ref, u_ref, x0_ref, bias_ref, m3_ref, o_ref):
    h = o_ref.shape[1]
    bias = bias_ref[...]
    for j in range(SUBLANES):
        bb = jnp.concatenate([b_ref[0, j], b_ref[1, j]], axis=0)
        y = _dot3(m3_ref[...], bb)
        for r in range(2):
            o_ref[r, :, j, :] = x0_ref[r, :, j, :] * (y[r * h:(r + 1) * h] + bias * u_ref[r, :, j, :])


def _hy_pass_c(bq, u4, x04, bias, m3, n_pairs):
    _, N2, N1, PC = bq.shape
    C = PC // n_pairs
    h = N1 // 2
    io = pl.BlockSpec((2, h, SUBLANES, C), lambda p, j: (p, 0, j, 0))
    return pl.pallas_call(
        _hy_pass_c_kernel,
        grid=(n_pairs, N2 // SUBLANES),
        in_specs=[pl.BlockSpec((2, SUBLANES, N1, C), lambda p, j: (0, j, 0, p)),
                  io, io,
                  pl.BlockSpec((1, C), lambda p, j: (0, 0)),
                  pl.BlockSpec((2 * N1, 2 * N1), lambda p, j: (0, 0))],
        out_specs=io,
        out_shape=jax.ShapeDtypeStruct(u4.shape, F32),
        compiler_params=_cparams("parallel", "parallel"),
        name="hy_pass_c",
    )(bq, u4, x04, bias, m3)


def hyena_long_conv(x0, u, kbuf, bias):
    B, L, C = u.shape
    N2 = min(128, L // 32)
    N1 = 2 * L // N2
    n_pairs = B // 2
    m1_data, m1_real, m2, m2inv, m3 = _dft_tables(N1, N2)
    kf = _hy_spectrum(_hy_pass_a(kbuf.reshape(2, N1 // 2, N2, C), m1_real, 1), m2)
    u4 = u.reshape(B, N1 // 2, N2, C)
    a = _hy_pass_a(u4, m1_data, n_pairs)
    bq = _hy_pass_b(a, kf, m2, m2inv, n_pairs)
    y = _hy_pass_c(bq, u4, x0.reshape(u4.shape), bias[None, :], m3, n_pairs)
    return y.reshape(B, L, C)


GDN_QKV = GDN_HEADS * (2 * GDN_DK + GDN_DV)


def _gdn_pre_kernel(p_ref, pp_ref, pn_ref, w_ref, alog_ref, dtb_ref, q_ref, k_ref, v_ref, gb_ref, *, n_first):
    i = pl.program_id(1)
    C = GDN_QKV
    first = jnp.logical_or(i == 0, i == n_first)
    last = jnp.logical_or(i == n_first - 1, i == pl.num_programs(1) - 1)
    y = _conv3(p_ref[:, :C], pp_ref[:, :C], pn_ref[:, :C], w_ref[...], first, last)
    y = _silu(y)
    nk = GDN_HEADS * GDN_DK
    for h in range(GDN_HEADS):
        sl = slice(h * GDN_DK, (h + 1) * GDN_DK)
        qh = y[:, sl]
        kh = y[:, nk + h * GDN_DK:nk + (h + 1) * GDN_DK]
        q_ref[:, sl] = qh * (lax.rsqrt(jnp.sum(qh * qh, axis=-1, keepdims=True) + EPS) * (GDN_DK ** -0.5))
        k_ref[:, sl] = kh * lax.rsqrt(jnp.sum(kh * kh, axis=-1, keepdims=True) + EPS)
    v_ref[...] = y[:, 2 * nk:]
    s = p_ref[:, C:]
    lane = lax.broadcasted_iota(jnp.int32, s.shape, 1)
    xa = s + dtb_ref[...]
    softplus = jnp.maximum(xa, 0.0) + jnp.log1p(jnp.exp(-jnp.abs(xa)))
    g = -jnp.exp(alog_ref[...]) * softplus
    gb_ref[...] = jnp.where(lane < 2 * GDN_HEADS, g, jnp.where(lane < 4 * GDN_HEADS, jax.nn.sigmoid(s), 0.0))


def gdn_pre(p_gdn, conv_w, a_log, dt_bias, tm, first_rows):
    B, S, C = p_gdn.shape
    assert first_rows % tm == 0 and S % tm == 0
    main, prev, nxt = _halo_specs(tm, S, C)
    pad = LANES - 2 * GDN_HEADS
    alog = jnp.pad(a_log.reshape(1, -1), ((0, 0), (0, pad)))
    dtb = jnp.pad(dt_bias.reshape(1, -1), ((0, 0), (0, pad)))
    nv = GDN_HEADS * GDN_DV
    o_spec = pl.BlockSpec((None, tm, nv), lambda b, i: (b, i, 0))
    return pl.pallas_call(
        functools.partial(_gdn_pre_kernel, n_first=first_rows // tm),
        grid=(B, S // tm),
        in_specs=[main, prev, nxt,
                  pl.BlockSpec((3, GDN_QKV), lambda b, i: (0, 0)),
                  pl.BlockSpec((1, LANES), lambda b, i: (0, 0)),
                  pl.BlockSpec((1, LANES), lambda b, i: (0, 0))],
        out_specs=[o_spec, o_spec, o_spec, pl.BlockSpec((None, tm, LANES), lambda b, i: (b, i, 0))],
        out_shape=[jax.ShapeDtypeStruct((B, S, nv), F32)] * 3 + [jax.ShapeDtypeStruct((B, S, LANES), F32)],
        compiler_params=_cparams("parallel", "parallel"),
        name="gdn_pre",
    )(p_gdn, p_gdn, p_gdn, conv_w, alog, dtb)


def _bdot(a, b):
    return jnp.dot(a.astype(BF16), b.astype(BF16), preferred_element_type=F32)


def _bdot_nt(a, b):
    return lax.dot_general(a.astype(BF16), b.astype(BF16), (((1,), (1,)), ((), ())), preferred_element_type=F32)


def _bdot_tn(a, b):
    return lax.dot_general(a.astype(BF16), b.astype(BF16), (((0,), (0,)), ((), ())), preferred_element_type=F32)


GDN_STEP_CHUNKS = 4


def _gdn_scan_kernel(qf_ref, kf_ref, vf_ref, gf_ref, qb_ref, kb_ref, vb_ref, gb_ref, of_ref, ob_ref, s_ref):
    C = GDN_CHUNK
    H = GDN_HEADS
    G = qf_ref.shape[0] // C
    in_refs = ((qf_ref, kf_ref, vf_ref, gf_ref), (qb_ref, kb_ref, vb_ref, gb_ref))
    o_refs = (of_ref, ob_ref)

    @pl.when(pl.program_id(1) == 0)
    def _():
        s_ref[...] = jnp.zeros_like(s_ref)

    ri = lax.broadcasted_iota(jnp.int32, (C, C), 0)
    ci = lax.broadcasted_iota(jnp.int32, (C, C), 1)
    eye = (ri == ci).astype(F32)
    incl = ((ci <= ri), (ci >= ri))
    strict = ((ci < ri), (ci > ri))
    rows = lambda g: slice(g * C, (g + 1) * C)
    cols = lambda h: slice(h * GDN_DK, (h + 1) * GDN_DK)
    chains = [(d, g, h) for d in range(2) for g in range(G) for h in range(H)]

    gbv = {(d, g): in_refs[d][3][rows(g), :] for d in range(2) for g in range(G)}
    gc_all = {dg: jnp.dot(incl[dg[0]].astype(F32), gbv[dg], precision=HI, preferred_element_type=F32)
              for dg in gbv}
    gc_t = {dg: gc_all[dg].T for dg in gbv}

    kk, gamma, rhs, qe, kdec, gend = {}, {}, {}, {}, {}, {}
    for ch in chains:
        d, g, h = ch
        c = d * H + h
        q, k, v = (in_refs[d][n][rows(g), cols(h)] for n in range(3))
        end = C - 1 if d == 0 else 0
        beta = gbv[d, g][:, 2 * H + c:2 * H + c + 1]
        gc_c = gc_all[d, g][:, c:c + 1]
        gc_r = gc_t[d, g][c:c + 1, :]
        g_tot = gc_all[d, g][end:end + 1, c:c + 1]
        gamma[ch] = jnp.where(incl[d], jnp.exp(jnp.where(incl[d], gc_c - gc_r, 0.0)), 0.0)
        e_c = jnp.exp(gc_c)
        kb = k * beta
        kk[ch] = _bdot_nt(jnp.concatenate([kb, q], axis=0), k)
        rhs[ch] = jnp.concatenate([v * beta, kb * e_c], axis=1).astype(BF16)
        qe[ch] = q * e_c
        kdec[ch] = (k * jnp.exp(g_tot - gc_c)).astype(BF16)
        gend[ch] = jnp.exp(g_tot)
    m = {ch: jnp.where(strict[ch[0]], kk[ch][:C] * gamma[ch], 0.0) for ch in chains}
    a_intra = {ch: (kk[ch][C:] * gamma[ch]).astype(BF16) for ch in chains}
    t = {ch: eye - m[ch] for ch in chains}
    pw = m
    for _ in range(5):
        pw = {ch: _bdot(pw[ch], pw[ch]) for ch in chains}
        t = {ch: t[ch] + _bdot(t[ch], pw[ch]) for ch in chains}
    uw = {ch: jnp.dot(t[ch].astype(BF16), rhs[ch], preferred_element_type=F32) for ch in chains}
    wq = {ch: jnp.concatenate([uw[ch][:, GDN_DV:], qe[ch]], axis=0).astype(BF16) for ch in chains}

    heads = [(d, h) for d in range(2) for h in range(H)]
    s = {dh: s_ref[dh[0] * H + dh[1]] for dh in heads}
    for j in range(G):
        at = lambda dh: (dh[0], j if dh[0] == 0 else G - 1 - j, dh[1])
        ws = {dh: jnp.dot(wq[at(dh)], s[dh].astype(BF16), preferred_element_type=F32) for dh in heads}
        v_new = {dh: uw[at(dh)][:, :GDN_DV] - ws[dh][:C] for dh in heads}
        for dh in heads:
            d, g, h = at(dh)
            o_refs[d][rows(g), cols(h)] = ws[dh][C:] + jnp.dot(a_intra[at(dh)], v_new[dh].astype(BF16),
                                                               preferred_element_type=F32)
        s = {dh: s[dh] * gend[at(dh)] + _bdot_tn(kdec[at(dh)], v_new[dh]) for dh in heads}
    for dh in heads:
        s_ref[dh[0] * H + dh[1]] = s[dh]


def gdn_scan(q, k, v, gb, n_ctx_chunks):
    B, Lt, NV = q.shape
    R = GDN_CHUNK * GDN_STEP_CHUNKS
    n = Lt // R
    n_ctx = n_ctx_chunks // GDN_STEP_CHUNKS
    n_lat = n - n_ctx
    assert n * R == Lt and n_ctx * GDN_STEP_CHUNKS == n_ctx_chunks

    def fwd(b, i):
        return (b, jnp.where(i < n_ctx, n_lat + i, i - n_ctx), 0)

    def bwd(b, i):
        return (b, n - 1 - i, 0)

    def specs(imap):
        return [pl.BlockSpec((None, R, NV), imap)] * 3 + [pl.BlockSpec((None, R, LANES), imap)]

    return pl.pallas_call(
        _gdn_scan_kernel,
        grid=(B, n),
        in_specs=specs(fwd) + specs(bwd),
        out_specs=[pl.BlockSpec((None, R, NV), fwd), pl.BlockSpec((None, R, NV), bwd)],
        out_shape=[jax.ShapeDtypeStruct((B, Lt, NV), F32)] * 2,
        scratch_shapes=[pltpu.VMEM((2 * GDN_HEADS, GDN_DK, GDN_DV), F32)],
        compiler_params=_cparams("parallel", "arbitrary"),
        name="gdn_scan",
    )(q, k, v, gb, q, k, v, gb)


MLA_QK = MLA_NOPE + MLA_ROPE
MLA_KVIN = MLA_KV_LORA + LANES


def _rope_partner(w_rope):
    nf = MLA_ROPE // 4
    parts = []
    for half in range(2):
        a = w_rope[..., half * 2 * nf:half * 2 * nf + nf]
        b = w_rope[..., half * 2 * nf + nf:(half + 1) * 2 * nf]
        parts += [-b, a]
    return jnp.concatenate(parts, axis=-1)


def _head_pad(nope, rope):
    pad = jnp.zeros(nope.shape[:-1] + (LANES - MLA_QK,), nope.dtype)
    out = jnp.concatenate([nope, rope, pad], axis=-1)
    return out.reshape(out.shape[:-2] + (MLA_HEADS * LANES,))


def mla_weights(w_uq, w_ukv):
    wq = w_uq.reshape(MLA_Q_LORA, MLA_HEADS, MLA_QK)
    qn, qr = wq[..., :MLA_NOPE], wq[..., MLA_NOPE:]
    wq2 = jnp.concatenate([_head_pad(qn, qr), _head_pad(jnp.zeros_like(qn), _rope_partner(qr))], axis=-1)
    wkv = w_ukv.reshape(MLA_KV_LORA, MLA_HEADS, MLA_NOPE + MLA_V)
    kn, vv = wkv[..., :MLA_NOPE], wkv[..., MLA_NOPE:]
    eye = jnp.broadcast_to(jnp.eye(MLA_ROPE, dtype=F32)[:, None, :], (MLA_ROPE, MLA_HEADS, MLA_ROPE))
    z_kn = jnp.zeros((MLA_ROPE, MLA_HEADS, MLA_NOPE), F32)
    z_rope = jnp.zeros((MLA_KV_LORA, MLA_HEADS, MLA_ROPE), F32)
    top = jnp.concatenate([_head_pad(kn, z_rope), _head_pad(jnp.zeros_like(kn), z_rope), _head_pad(vv, z_rope)],
                          axis=-1)
    mid = jnp.concatenate([_head_pad(z_kn, eye), _head_pad(z_kn, _rope_partner(eye)),
                           jnp.zeros((MLA_ROPE, MLA_HEADS * LANES), F32)], axis=-1)
    bot = jnp.zeros((MLA_KVIN - MLA_KV_LORA - MLA_ROPE, top.shape[1]), F32)
    return wq2.astype(BF16), jnp.concatenate([top, mid, bot], axis=0).astype(BF16)


def rope_tables(row, col):
    nf = MLA_ROPE // 4
    inv_freq = ROPE_THETA ** (-jnp.arange(nf, dtype=F32) / nf)
    ang = jnp.concatenate([row.astype(F32)[:, None] * inv_freq[None, :]] * 2
                          + [col.astype(F32)[:, None] * inv_freq[None, :]] * 2, axis=-1)
    n = ang.shape[0]
    pad = jnp.zeros((n, LANES - MLA_QK), F32)
    cos = jnp.concatenate([jnp.ones((n, MLA_NOPE), F32), jnp.cos(ang), pad], axis=-1)
    sin = jnp.concatenate([jnp.zeros((n, MLA_NOPE), F32), jnp.sin(ang), pad], axis=-1)
    return cos, sin


def _mla_proj_kernel(p_ref, gq_ref, gkv_ref, wq_ref, wk_ref, cos_ref, sin_ref, q_ref, k_ref, v_ref):
    HL = MLA_HEADS * LANES
    cos = jnp.concatenate([cos_ref[...]] * MLA_HEADS, axis=1)
    sin = jnp.concatenate([sin_ref[...]] * MLA_HEADS, axis=1)
    cq = p_ref[:, :MLA_Q_LORA]
    cqn = cq * lax.rsqrt(jnp.mean(cq * cq, axis=-1, keepdims=True) + EPS) * gq_ref[...]
    qq = jnp.dot(cqn.astype(BF16), wq_ref[...], preferred_element_type=F32)
    q_ref[...] = ((qq[:, :HL] * cos + qq[:, HL:] * sin) * (MLA_QK ** -0.5 * math.log2(math.e))).astype(q_ref.dtype)
    ck = p_ref[:, MLA_Q_LORA:]
    lane = lax.broadcasted_iota(jnp.int32, ck.shape, 1)
    is_kv = lane < MLA_KV_LORA
    ms = jnp.sum(jnp.where(is_kv, ck * ck, 0.0), axis=-1, keepdims=True) * (1.0 / MLA_KV_LORA)
    ckn = jnp.where(is_kv, ck * lax.rsqrt(ms + EPS) * gkv_ref[...], ck)
    kk = jnp.dot(ckn.astype(BF16), wk_ref[...], preferred_element_type=F32)
    k_ref[...] = (kk[:, :HL] * cos + kk[:, HL:2 * HL] * sin).astype(k_ref.dtype)
    vv = kk[:, 2 * HL:]
    vlane = lax.broadcasted_iota(jnp.int32, vv.shape, 1)
    v_ref[...] = jnp.where(vlane % LANES == MLA_V, 1.0, vv).astype(v_ref.dtype)


def mla_proj(p_mla, gq, gkv, wq2, wk2, cos, sin, tm):
    B, S, C = p_mla.shape
    tm = min(tm, S)
    HL = MLA_HEADS * LANES
    gkv = jnp.pad(gkv, ((0, 0), (0, MLA_KVIN - MLA_KV_LORA)))
    row = lambda n: pl.BlockSpec((None, tm, n), lambda b, i: (b, i, 0))
    full = lambda a: pl.BlockSpec(a.shape, lambda b, i: (0, 0))
    tab = pl.BlockSpec((tm, LANES), lambda b, i: (i, 0))
    return pl.pallas_call(
        _mla_proj_kernel,
        grid=(B, S // tm),
        in_specs=[row(C), full(gq), full(gkv), full(wq2), full(wk2), tab, tab],
        out_specs=[row(HL)] * 3,
        out_shape=[jax.ShapeDtypeStruct((B, S, HL), BF16)] * 3,
        compiler_params=_cparams("parallel", "parallel"),
        name="mla_proj",
    )(p_mla, gq, gkv, wq2, wk2, cos, sin)


ATT_SLAB = 32


def _mla_attn_kernel(q_ref, k_ref, v_ref, o_ref, *, tk):
    tq = q_ref.shape[0]
    Tk = k_ref.shape[0]
    n_full, rem = Tk // tk, Tk % tk
    heads = (slice(0, LANES), slice(LANES, 2 * LANES))
    qs = [q_ref[:, hs] for hs in heads]

    def step(carry, start, size):
        ss = [lax.dot_general(qs[h], k_ref[pl.ds(start, size), heads[h]], (((1,), (1,)), ((), ())),
                              preferred_element_type=F32) for h in range(2)]
        out = []
        for h in range(2):
            m, acc = carry[h]
            m_new = jnp.maximum(m, jnp.max(ss[h], axis=-1, keepdims=True))
            p = jnp.concatenate([jnp.exp2(ss[h][r:r + ATT_SLAB] - m_new[r:r + ATT_SLAB]).astype(BF16)
                                 for r in range(0, tq, ATT_SLAB)], axis=0)
            acc = acc * jnp.exp2(m - m_new) + jnp.dot(p, v_ref[pl.ds(start, size), heads[h]],
                                                      preferred_element_type=F32)
            out.append((m_new, acc))
        return tuple(out)

    carry = tuple((jnp.full((tq, 1), -jnp.inf, F32), jnp.zeros((tq, LANES), F32)) for _ in range(2))
    if n_full:
        carry = lax.fori_loop(0, n_full, lambda c, cr: step(cr, pl.multiple_of(c * tk, tk), tk), carry,
                              unroll=2 if n_full % 2 == 0 else 1)
    if rem:
        carry = step(carry, n_full * tk, rem)
    o0, o1 = [acc / acc[:, MLA_V:MLA_V + 1] for _, acc in carry]
    lane = lax.broadcasted_iota(jnp.int32, (tq, LANES), 1)
    o_ref[...] = jnp.where(lane < MLA_V, o0, pltpu.roll(o1, MLA_V, 1)).astype(o_ref.dtype)


def mla_attn(q, k, v, q_rows, k_rows, tq, tk):
    B = q.shape[0]
    (q0, S), (k0, Tk) = q_rows, k_rows
    tq = min(tq, S)
    assert q0 % tq == 0 and k0 % Tk == 0
    qb, kb = q0 // tq, k0 // Tk
    return pl.pallas_call(
        functools.partial(_mla_attn_kernel, tk=tk),
        grid=(B, MLA_HEADS // 2, S // tq),
        in_specs=[pl.BlockSpec((None, tq, 2 * LANES), lambda b, h, i: (b, qb + i, h)),
                  pl.BlockSpec((None, Tk, 2 * LANES), lambda b, h, i: (b, kb, h)),
                  pl.BlockSpec((None, Tk, 2 * LANES), lambda b, h, i: (b, kb, h))],
        out_specs=pl.BlockSpec((None, tq, LANES), lambda b, h, i: (b, i, h)),
        out_shape=jax.ShapeDtypeStruct((B, S, MLA_HEADS * MLA_V), BF16),
        compiler_params=_cparams("parallel", "parallel", "arbitrary"),
        name="mla_attn",
    )(q, k, v)


def _route(logits, rb):
    lane = lax.broadcasted_iota(jnp.int32, logits.shape, 1)
    neg = -jnp.inf
    scores = jax.nn.sigmoid(logits)
    sel = scores + rb

    def top2(masked):
        m1 = jnp.max(masked, axis=-1, keepdims=True)
        i1 = jnp.min(jnp.where(masked == m1, lane, LANES), axis=-1, keepdims=True)
        rest = jnp.where(lane == i1, neg, masked)
        m2 = jnp.max(rest, axis=-1, keepdims=True)
        i2 = jnp.min(jnp.where(rest == m2, lane, LANES), axis=-1, keepdims=True)
        return m1, i1, m2, i2

    best = None
    for gi in range(N_GROUPS):
        in_g = jnp.logical_and(lane >= gi * EXPERTS_PER_GROUP, lane < (gi + 1) * EXPERTS_PER_GROUP)
        m1, _, m2, _ = top2(jnp.where(in_g, sel, neg))
        gs = m1 + m2
        if best is None:
            best, grp = gs, jnp.zeros_like(gs, dtype=jnp.int32)
        else:
            better = gs > best
            grp = jnp.where(better, gi, grp)
            best = jnp.where(better, gs, best)
    lo = grp * EXPERTS_PER_GROUP
    in_grp = jnp.logical_and(lane >= lo, lane < lo + EXPERTS_PER_GROUP)
    _, i1, _, i2 = top2(jnp.where(in_grp, sel, neg))
    picked = jnp.where(jnp.logical_or(lane == i1, lane == i2), scores, 0.0)
    gate = picked / jnp.sum(picked, axis=-1, keepdims=True)
    return jnp.where(lane == LANES - 1, grp.astype(F32), gate)


def _merge_kernel(hyv_ref, of_ref, ob_ref, pg_ref, at_ref, x_ref, gt1_ref, sc2_ref, sh2_ref, gng_ref, n2g_ref,
                  whb_ref, wgd_ref, wml_ref, wo_ref, rw_ref, rb_ref, xo_ref, h2_ref, gate_ref):
    nv = GDN_HEADS * GDN_DV
    D = x_ref.shape[1]
    o = of_ref[...] + ob_ref[...]
    z = pg_ref[:, :nv]
    ys = []
    for h in range(GDN_HEADS):
        sl = slice(h * GDN_DV, (h + 1) * GDN_DV)
        oh = o[:, sl]
        on = oh * lax.rsqrt(jnp.mean(oh * oh, axis=-1, keepdims=True) + EPS) * gng_ref[...]
        ys.append(on * _silu(z[:, sl]))
    y_gdn = jnp.dot(jnp.concatenate(ys, axis=1).astype(BF16), wgd_ref[...], preferred_element_type=F32)
    y_hy = jnp.dot(hyv_ref[...].astype(BF16), whb_ref[...], preferred_element_type=F32)
    y_mla = jnp.dot(at_ref[...], wml_ref[...], preferred_element_type=F32)
    merged = (jax.nn.sigmoid(pg_ref[:, nv:nv + D]) * y_hy
              + jax.nn.sigmoid(pg_ref[:, nv + D:nv + 2 * D]) * y_gdn
              + jax.nn.sigmoid(pg_ref[:, nv + 2 * D:]) * y_mla)
    mix = jnp.dot(merged.astype(BF16), wo_ref[...], preferred_element_type=F32)
    xn = x_ref[...] + gt1_ref[...] * mix
    xo_ref[...] = xn
    y2 = xn * lax.rsqrt(jnp.mean(xn * xn, axis=-1, keepdims=True) + EPS) * n2g_ref[...]
    h2 = y2 * (1.0 + sc2_ref[...]) + sh2_ref[...]
    h2_ref[...] = h2.astype(h2_ref.dtype)
    logits = jnp.dot(h2, rw_ref[...], precision=HI, preferred_element_type=F32)
    gate_ref[...] = _route(logits, rb_ref[...])


def merge_out(hyv, o_f, o_b, pg, attn, x, gt1, sc2, sh2, gdn_norm_g, norm2_g, w_hy, w_gdn, w_mla, w_out,
              router_w, router_b, tm, o_row0):
    B, S, D = x.shape
    tm = min(tm, S)
    assert o_row0 % tm == 0
    ob0 = o_row0 // tm
    row = lambda n: pl.BlockSpec((None, tm, n), lambda b, i: (b, i, 0))
    full = lambda a: pl.BlockSpec(a.shape, lambda b, i: (0, 0))
    nv = GDN_HEADS * GDN_DV
    o_spec = pl.BlockSpec((None, tm, nv), lambda b, i: (b, ob0 + i, 0))
    return pl.pallas_call(
        _merge_kernel,
        grid=(B, S // tm),
        in_specs=[row(HY_W), o_spec, o_spec, row(pg.shape[2]), row(MLA_HEADS * MLA_V), row(D),
                  _mod_spec(gt1, D), _mod_spec(sc2, D), _mod_spec(sh2, D),
                  full(gdn_norm_g), full(norm2_g), full(w_hy), full(w_gdn), full(w_mla), full(w_out),
                  full(router_w), full(router_b)],
        out_specs=[row(D), row(D), row(LANES)],
        out_shape=[jax.ShapeDtypeStruct((B, S, D), F32), jax.ShapeDtypeStruct((B, S, D), BF16),
                   jax.ShapeDtypeStruct((B, S, LANES), F32)],
        compiler_params=_cparams("parallel", "parallel"),
        name="merge_out",
    )(hyv, o_f, o_b, pg, attn, x, gt1, sc2, sh2, gdn_norm_g, norm2_g, w_hy, w_gdn, w_mla, w_out,
      router_w, router_b)


MOE_BLK = 128


def _moe_kernel(h_ref, gate_ref, w1_ref, w3_ref, w2_ref, x_ref, gt2_ref, fg_ref, o_ref,
                xs_ref, gs_ref, y_ref, slot_ref, seg_ref, *, final_norm):
    e = pl.program_id(1)
    tm, D = h_ref.shape
    A = xs_ref.shape[0]

    @pl.when(e == 0)
    def _():
        gate = gate_ref[...]
        lane = lax.broadcasted_iota(jnp.int32, gate.shape, 1)
        grp = gate[:, LANES - 1:LANES]
        member = jnp.where(jnp.logical_and(lane.astype(F32) == grp, lane < N_GROUPS), 1.0, 0.0)
        ri = lax.broadcasted_iota(jnp.int32, (tm, tm), 0)
        ci = lax.broadcasted_iota(jnp.int32, (tm, tm), 1)
        before = jnp.where(ci < ri, 1.0, 0.0).astype(BF16)
        rank = jnp.dot(before, member.astype(BF16), preferred_element_type=F32)
        cnt = jnp.sum(member, axis=0, keepdims=True)
        blocks = jnp.floor((cnt + (MOE_BLK - 1)) * (1.0 / MOE_BLK))
        padded = blocks * MOE_BLK
        l1 = lane[0:1, :]
        p0, p1, p2 = padded[:, 0:1], padded[:, 1:2], padded[:, 2:3]
        start = jnp.where(l1 == 0, 0.0, jnp.where(l1 == 1, p0, jnp.where(l1 == 2, p0 + p1, p0 + p1 + p2)))
        slot = jnp.sum(member * (start + rank), axis=-1, keepdims=True)
        slot_b = jnp.broadcast_to(slot, (tm, LANES))
        slot_ref[...] = slot_b
        start_i, blocks_i = start.astype(jnp.int32), blocks.astype(jnp.int32)
        for g in range(N_GROUPS):
            seg_ref[g] = start_i[0, g]
            seg_ref[N_GROUPS + g] = blocks_i[0, g]
        slot_row = slot_b.T[0:1, :]
        pick = jnp.where(lax.broadcasted_iota(jnp.int32, (A, tm), 0).astype(F32) == slot_row, 1.0, 0.0).astype(BF16)
        g_hi = gate.astype(BF16)
        g_lo = (gate - g_hi.astype(F32)).astype(BF16)
        got = jnp.dot(pick, jnp.concatenate([h_ref[...], g_hi, g_lo], axis=1), preferred_element_type=F32)
        xs_ref[...] = got[:, :D].astype(BF16)
        gs_ref[...] = got[:, D:D + LANES] + got[:, D + LANES:]
        y_ref[...] = jnp.zeros_like(y_ref)

    g = e // EXPERTS_PER_GROUP
    first = seg_ref[g]
    lane_b = lax.broadcasted_iota(jnp.int32, (MOE_BLK, LANES), 1)

    def block(b, carry):
        rows = pl.ds(pl.multiple_of(first + b * MOE_BLK, MOE_BLK), MOE_BLK)
        xb = xs_ref[rows, :]
        he = (_silu(jnp.dot(xb, w1_ref[...], preferred_element_type=F32))
              * jnp.dot(xb, w3_ref[...], preferred_element_type=F32))
        ge = jnp.sum(jnp.where(lane_b == e, gs_ref[rows, :], 0.0), axis=-1, keepdims=True)
        y_ref[rows, :] += ge * jnp.dot(he.astype(BF16), w2_ref[...], preferred_element_type=F32)
        return carry

    lax.fori_loop(0, seg_ref[N_GROUPS + g], block, 0)

    @pl.when(e == pl.num_programs(1) - 1)
    def _():
        back = jnp.where(lax.broadcasted_iota(jnp.int32, (tm, A), 1).astype(F32) == slot_ref[:, 0:1], 1.0, 0.0)
        moe_out = jnp.dot(back.astype(BF16), y_ref[...].astype(BF16), preferred_element_type=F32)
        xn = x_ref[...] + gt2_ref[...] * moe_out
        if final_norm:
            xn = xn * lax.rsqrt(jnp.mean(xn * xn, axis=-1, keepdims=True) + EPS) * fg_ref[...]
        o_ref[...] = xn


def moe(h2, gate, w1, w3, w2, x, gt2, final_g, S, tm, final_norm):
    T, D = x.shape
    E, _, FF = w1.shape
    if gt2.shape[0] == 1:
        tm = min(tm, T)
        gt_spec = pl.BlockSpec((None, 1, D), lambda i, e: (0, 0, 0))
    else:
        tm = min(tm, S)
        per_b = S // tm
        gt_spec = pl.BlockSpec((None, 1, D), lambda i, e: (i // per_b, 0, 0))
    A = tm + N_GROUPS * MOE_BLK
    return pl.pallas_call(
        functools.partial(_moe_kernel, final_norm=final_norm),
        grid=(T // tm, E),
        in_specs=[pl.BlockSpec((tm, D), lambda i, e: (i, 0)),
                  pl.BlockSpec((tm, LANES), lambda i, e: (i, 0)),
                  pl.BlockSpec((None, D, FF), lambda i, e: (e, 0, 0)),
                  pl.BlockSpec((None, D, FF), lambda i, e: (e, 0, 0)),
                  pl.BlockSpec((None, FF, D), lambda i, e: (e, 0, 0)),
                  pl.BlockSpec((tm, D), lambda i, e: (i, 0)),
                  gt_spec,
                  pl.BlockSpec((1, D), lambda i, e: (0, 0))],
        out_specs=pl.BlockSpec((tm, D), lambda i, e: (i, 0)),
        out_shape=jax.ShapeDtypeStruct((T, D), F32),
        scratch_shapes=[pltpu.VMEM((A, D), BF16), pltpu.VMEM((A, LANES), F32), pltpu.VMEM((A, D), F32),
                        pltpu.VMEM((tm, LANES), F32), pltpu.SMEM((2 * N_GROUPS,), jnp.int32)],
        compiler_params=_cparams("parallel", "arbitrary"),
        name="moe",
    )(h2, gate, w1, w3, w2, x, gt2, final_g)


IN_SIZES = (3 * HY_W, GDN_QKV, GDN_HEADS * GDN_DV, 2 * GDN_HEADS, 2 * GDN_HEADS, MLA_Q_LORA, MLA_KV_LORA, MLA_ROPE)


def _split_w_in(w_in):
    D = w_in.shape[0]
    parts, off = [], 0
    for n in IN_SIZES:
        parts.append(w_in[:, off:off + n])
        off += n
    hy, qkv, z, a, b, cq, ckv, kr = parts
    gate = w_in[:, off:]
    zeros = lambda n: jnp.zeros((D, n), w_in.dtype)
    w_hy = hy
    w_gdn = jnp.concatenate([qkv, a, b, zeros(LANES - 4 * GDN_HEADS)], axis=1)
    w_mla = jnp.concatenate([cq, ckv, kr, zeros(LANES - MLA_ROPE)], axis=1)
    w_gate = jnp.concatenate([z, gate], axis=1)
    return [w.astype(BF16) for w in (w_hy, w_gdn, w_mla, w_gate)]


def _layer(x, cx, mod, mod_c, lw, tabs, router_w, router_b, final_g, update_ctx, last):
    B, S, D = x.shape
    Lc = cx.shape[1]
    sh1, sc1, gt1, sh2, sc2, gt2 = [m[:, None, :] for m in jnp.split(mod, 6, axis=-1)]
    csh1, csc1, cgt1, csh2, csc2, cgt2 = [m[:, None, :] for m in jnp.split(mod_c, 6, axis=-1)]
    n1g = lw['norm1_g'][None, :]
    w_hy, w_gdn, w_mla, w_gate = _split_w_in(lw['w_in'])

    Lt = S + Lc
    TM, TMC = 512, 256

    def project(w, joint):
        if joint:
            lat = in_proj(x, n1g, sc1, sh1, w, TM, rows_total=Lt)
            return in_proj(cx, n1g, csc1, csh1, w, TMC, into=lat)
        return in_proj(x, n1g, sc1, sh1, w, TM), in_proj(cx, n1g, csc1, csh1, w, TMC)

    p_hy, c_hy = project(w_hy, False)
    p_gate, c_gate = project(w_gate, False)
    pc_gdn = project(w_gdn, True)
    pc_mla = project(w_mla, True)

    o_f, o_b = gdn_scan(*gdn_pre(pc_gdn, lw['gdn_conv_w'], lw['gdn_a_log'], lw['gdn_dt_bias'], TMC, S),
                        Lc // GDN_CHUNK)

    wq2, wk2 = mla_weights(lw['mla_w_uq'], lw['mla_w_ukv'])
    gq, gkv = lw['mla_q_norm_g'][None, :], lw['mla_kv_norm_g'][None, :]
    q_a, k_a, v_a = mla_proj(pc_mla, gq, gkv, wq2, wk2, tabs[0], tabs[1], TMC)
    attn_l = mla_attn(q_a, k_a, v_a, (0, S), (0, Lt), 512, 1024)

    def hyena(p, L):
        filt = hy_filter(L, lw['hy_f_w1'], lw['hy_f_b1'], lw['hy_f_w2'], lw['hy_f_b2'], lw['hy_f_w3'],
                         lw['hy_f_freq'], lw['hy_decay'])
        x0, u = hy_pre(p, lw['hy_conv_w'], lw['hy_conv_b'][None, :], 512)
        return hyena_long_conv(x0, u, filt, lw['hy_bias'])

    hyv_l = hyena(p_hy, S)

    wb = lambda name: lw[name].astype(BF16)
    rw = jnp.pad(router_w, ((0, 0), (0, LANES - N_EXPERTS)))
    rb = jnp.pad(router_b[None, :], ((0, 0), (0, LANES - N_EXPERTS)))
    n2g, gng = lw['norm2_g'][None, :], lw['gdn_norm_g'][None, :]
    w1, w3, w2 = wb('moe_w1'), wb('moe_w3'), wb('moe_w2')

    def finish(xx, hyv, o_row0, pg, attn, gt1_, sc2_, sh2_, gt2_, tm, tm_moe, fin):
        Bx, Sx, _ = xx.shape
        xn, h2, gate = merge_out(hyv, o_f, o_b, pg, attn, xx, gt1_, sc2_, sh2_, gng, n2g, wb('hy_out'), wb('gdn_out'),
                                 wb('mla_out'), wb('w_out'), rw, rb, tm, o_row0)
        out = moe(h2.reshape(Bx * Sx, D), gate.reshape(Bx * Sx, LANES), w1, w3, w2, xn.reshape(Bx * Sx, D),
                  gt2_, final_g, Sx, tm_moe, fin)
        return out.reshape(Bx, Sx, D)

    x_new = finish(x, hyv_l, 0, p_gate, attn_l, gt1, sc2, sh2, gt2, 512, 1024, last)
    if update_ctx:
        hyv_c = hyena(c_hy, Lc)
        attn_c = mla_attn(q_a, k_a, v_a, (S, Lc), (S, Lc), 256, 1024)
        cx = finish(cx, hyv_c, S, c_gate, attn_c, cgt1, csc2, csh2, cgt2, 256, 1024, False)
    return x_new, cx


def kernel(x, c, ctx, c_ctx, w_ada, b_ada, norm1_g, norm2_g, w_in, hy_conv_w, hy_conv_b, hy_f_w1, hy_f_b1, hy_f_w2, hy_f_b2, hy_f_w3, hy_f_freq, hy_decay, hy_bias, hy_out, gdn_conv_w, gdn_a_log, gdn_dt_bias, gdn_norm_g, gdn_out, mla_q_norm_g, mla_w_uq, mla_kv_norm_g, mla_w_ukv, mla_out, w_out, moe_w1, moe_w3, moe_w2, router_w, router_b, final_norm_g):
    per_layer = dict(norm1_g=norm1_g, norm2_g=norm2_g, w_in=w_in, hy_conv_w=hy_conv_w, hy_conv_b=hy_conv_b,
                     hy_f_w1=hy_f_w1, hy_f_b1=hy_f_b1, hy_f_w2=hy_f_w2, hy_f_b2=hy_f_b2, hy_f_w3=hy_f_w3,
                     hy_f_freq=hy_f_freq, hy_decay=hy_decay, hy_bias=hy_bias, hy_out=hy_out,
                     gdn_conv_w=gdn_conv_w, gdn_a_log=gdn_a_log, gdn_dt_bias=gdn_dt_bias, gdn_norm_g=gdn_norm_g,
                     gdn_out=gdn_out, mla_q_norm_g=mla_q_norm_g, mla_w_uq=mla_w_uq, mla_kv_norm_g=mla_kv_norm_g,
                     mla_w_ukv=mla_w_ukv, mla_out=mla_out, w_out=w_out, moe_w1=moe_w1, moe_w3=moe_w3, moe_w2=moe_w2)
    B, S, D = x.shape
    Lc = ctx.shape[1]
    depth = w_ada.shape[0]
    rows = S // GRID_W
    row = jnp.repeat(jnp.arange(rows, dtype=jnp.int32), GRID_W)
    col = jnp.tile(jnp.arange(GRID_W, dtype=jnp.int32), rows)
    zero = jnp.zeros((Lc,), jnp.int32)
    tabs = rope_tables(jnp.concatenate([row, zero]), jnp.concatenate([col, zero]))
    cc = jnp.concatenate([c, c_ctx[None, :], jnp.zeros((2 * SUBLANES - B - 1, D), F32)], axis=0)
    final_g = final_norm_g[None, :]
    cx = ctx
    for l in range(depth):
        lw = {k: v[l] for k, v in per_layer.items()}
        mods = ada_mod(cc, w_ada[l], b_ada[l][None, :])
        x, cx = _layer(x, cx, mods[:B], mods[B:B + 1], lw, tabs, router_w, router_b, final_g,
                       l < depth - 1, l == depth - 1)
    return x
```

```python
import functools
import math

import jax
import jax.numpy as jnp
from jax import lax
from jax.experimental import pallas as pl
from jax.experimental.pallas import tpu as pltpu

F32 = jnp.float32
BF16 = jnp.bfloat16
HI = lax.Precision.HIGHEST
EPS = 1e-6

GRID_W = 64
HY_W = 512
HY_EMB = 33
HY_BANDS = (HY_EMB - 1) // 2
HY_MOD_SHIFT = 0.05
GDN_HEADS = 4
GDN_DK = 128
GDN_DV = 128
GDN_CHUNK = 64
MLA_HEADS = 8
MLA_NOPE = 64
MLA_ROPE = 32
MLA_V = 64
MLA_Q_LORA = 768
MLA_KV_LORA = 256
ROPE_THETA = 10000.0
N_EXPERTS = 16
N_GROUPS = 4
EXPERTS_PER_GROUP = N_EXPERTS // N_GROUPS
EXPERT_FF = 512
LANES = 128
SUBLANES = 8
VMEM_LIMIT = 56 * 1024 * 1024


def _cparams(*sem):
    return pltpu.CompilerParams(dimension_semantics=sem, vmem_limit_bytes=VMEM_LIMIT)


def _silu(x):
    return x * jax.nn.sigmoid(x)


def _ada_kernel(c_ref, w_ref, b_ref, o_ref):
    a = _silu(c_ref[...])
    o_ref[...] = jnp.dot(a, w_ref[...], precision=HI, preferred_element_type=F32) + b_ref[...]


def ada_mod(cc, w, b):
    R, D = cc.shape
    N = w.shape[1]
    tn = 1536
    return pl.pallas_call(
        _ada_kernel,
        grid=(N // tn,),
        in_specs=[pl.BlockSpec((R, D), lambda j: (0, 0)),
                  pl.BlockSpec((D, tn), lambda j: (0, j)),
                  pl.BlockSpec((1, tn), lambda j: (0, j))],
        out_specs=pl.BlockSpec((R, tn), lambda j: (0, j)),
        out_shape=jax.ShapeDtypeStruct((R, N), F32),
        compiler_params=_cparams("parallel"),
        name="ada_mod",
    )(cc, w, b)


def _norm_mod_matmul(x_ref, g, sc, sh, w_ref, o_ref):
    tm = x_ref.shape[0]
    hs = []
    for rs in (slice(0, tm // 2), slice(tm // 2, tm)):
        x = x_ref[rs, :]
        y = x * lax.rsqrt(jnp.mean(x * x, axis=-1, keepdims=True) + EPS) * g
        hs.append((rs, (y * (1.0 + sc) + sh).astype(BF16)))
    for rs, h in hs:
        o_ref[rs, :] = jnp.dot(h, w_ref[...], preferred_element_type=F32).astype(o_ref.dtype)


def _inproj_kernel(x_ref, g_ref, sc_ref, sh_ref, w_ref, o_ref):
    _norm_mod_matmul(x_ref, g_ref[...], sc_ref[...], sh_ref[...], w_ref, o_ref)


def _mod_spec(m, D):
    if m.shape[0] == 1:
        return pl.BlockSpec((None, 1, D), lambda b, i: (0, 0, 0))
    return pl.BlockSpec((None, 1, D), lambda b, i: (b, 0, 0))


def in_proj(x, g, sc, sh, w, tm):
    B, S, D = x.shape
    N = w.shape[1]
    tm = min(tm, S)
    return pl.pallas_call(
        _inproj_kernel,
        grid=(B, S // tm),
        in_specs=[pl.BlockSpec((None, tm, D), lambda b, i: (b, i, 0)),
                  pl.BlockSpec((1, D), lambda b, i: (0, 0)),
                  _mod_spec(sc, D), _mod_spec(sh, D),
                  pl.BlockSpec((D, N), lambda b, i: (0, 0))],
        out_specs=pl.BlockSpec((None, tm, N), lambda b, i: (b, i, 0)),
        out_shape=jax.ShapeDtypeStruct((B, S, N), F32),
        compiler_params=_cparams("parallel", "parallel"),
        name="in_proj",
    )(x, g, sc, sh, w)


def _inproj_joint_kernel(x_ref, cx_ref, g_ref, sc_ref, sh_ref, csc_ref, csh_ref, w_ref, o_ref, *, n_lat):
    is_lat = pl.program_id(1) < n_lat
    sc = jnp.where(is_lat, sc_ref[...], csc_ref[...])
    sh = jnp.where(is_lat, sh_ref[...], csh_ref[...])

    @pl.when(is_lat)
    def _():
        _norm_mod_matmul(x_ref, g_ref[...], sc, sh, w_ref, o_ref)

    @pl.when(jnp.logical_not(is_lat))
    def _():
        _norm_mod_matmul(cx_ref, g_ref[...], sc, sh, w_ref, o_ref)


def in_proj_joint(x, cx, g, sc, sh, csc, csh, w, tm):
    B, S, D = x.shape
    Lc = cx.shape[1]
    N = w.shape[1]
    assert S % tm == 0 and Lc % tm == 0
    n_lat, n_ctx = S // tm, Lc // tm
    return pl.pallas_call(
        functools.partial(_inproj_joint_kernel, n_lat=n_lat),
        grid=(B, n_lat + n_ctx),
        in_specs=[pl.BlockSpec((None, tm, D), lambda b, i: (b, jnp.minimum(i, n_lat - 1), 0)),
                  pl.BlockSpec((None, tm, D), lambda b, i: (b, jnp.maximum(i - n_lat, 0), 0)),
                  pl.BlockSpec((1, D), lambda b, i: (0, 0)),
                  _mod_spec(sc, D), _mod_spec(sh, D), _mod_spec(csc, D), _mod_spec(csh, D),
                  pl.BlockSpec((D, N), lambda b, i: (0, 0))],
        out_specs=pl.BlockSpec((None, tm, N), lambda b, i: (b, i, 0)),
        out_shape=jax.ShapeDtypeStruct((B, S + Lc, N), F32),
        compiler_params=_cparams("parallel", "parallel"),
        name="in_proj",
    )(x, cx, g, sc, sh, csc, csh, w)


def _halo_specs(tm, S, C, col_block=0):
    nb8 = tm // SUBLANES
    last8 = S // SUBLANES - 1
    main = pl.BlockSpec((None, tm, C), lambda b, i: (b, i, col_block))
    prev = pl.BlockSpec((None, SUBLANES, C), lambda b, i: (b, jnp.maximum(i * nb8 - 1, 0), col_block))
    nxt = pl.BlockSpec((None, SUBLANES, C), lambda b, i: (b, jnp.minimum((i + 1) * nb8, last8), col_block))
    return main, prev, nxt


def _conv3(x, prev8, next8, w, first, last):
    tm = x.shape[0]
    row = lax.broadcasted_iota(jnp.int32, x.shape, 0)
    p_row = jnp.where(first, 0.0, prev8[SUBLANES - 1:SUBLANES, :])
    n_row = jnp.where(last, 0.0, next8[0:1, :])
    x_prev = jnp.where(row == 0, p_row, pltpu.roll(x, 1, 0))
    x_next = jnp.where(row == tm - 1, n_row, pltpu.roll(x, tm - 1, 0))
    return x_prev * w[0:1, :] + x * w[1:2, :] + x_next * w[2:3, :]


def _hy_pre_kernel(p_ref, pp_ref, pn_ref, w_ref, b_ref, x0_ref, u_ref):
    i = pl.program_id(1)
    y = _conv3(p_ref[...], pp_ref[...], pn_ref[...], w_ref[...], i == 0, i == pl.num_programs(1) - 1)
    y = y + b_ref[...]
    x0_ref[...] = y[:, :HY_W]
    u_ref[...] = y[:, HY_W:2 * HY_W] * y[:, 2 * HY_W:]


def hy_pre(p_hy, conv_w, conv_b, tm):
    B, S, C = p_hy.shape
    tm = min(tm, S)
    main, prev, nxt = _halo_specs(tm, S, C)
    o_spec = pl.BlockSpec((None, tm, HY_W), lambda b, i: (b, i, 0))
    return pl.pallas_call(
        _hy_pre_kernel,
        grid=(B, S // tm),
        in_specs=[main, prev, nxt,
                  pl.BlockSpec((3, C), lambda b, i: (0, 0)),
                  pl.BlockSpec((1, C), lambda b, i: (0, 0))],
        out_specs=[o_spec, o_spec],
        out_shape=[jax.ShapeDtypeStruct((B, S, HY_W), F32)] * 2,
        compiler_params=_cparams("parallel", "parallel"),
        name="hy_pre",
    )(p_hy, p_hy, p_hy, conv_w, conv_b)


def _hy_filter_kernel(z_ref, w1_ref, b1_ref, w2_ref, b2_ref, w3_ref, fq_ref, dc_ref, o_ref):
    fq = fq_ref[...]
    h = jnp.sin(fq * (jnp.dot(z_ref[...], w1_ref[...], precision=HI, preferred_element_type=F32) + b1_ref[...]))
    h = jnp.sin(fq * (jnp.dot(h, w2_ref[...], precision=HI, preferred_element_type=F32) + b2_ref[...]))
    h = jnp.dot(h, w3_ref[...], precision=HI, preferred_element_type=F32)
    taps = h * (jnp.exp(-z_ref[:, 0:1] * jnp.abs(dc_ref[...])) + HY_MOD_SHIFT)
    C = o_ref.shape[1]
    is_fwd = z_ref[:, LANES - 2:LANES - 1] > 0.5
    o_ref[...] = jnp.where(is_fwd, taps[:, :C], taps[:, C:]) * z_ref[:, LANES - 1:LANES]


def hy_filter(L, w1, b1, w2, b2, w3, freq, decay):
    t = jnp.linspace(0.0, 1.0, L, dtype=F32)[:, None]
    w = (2.0 * math.pi / L) * jnp.arange(L, dtype=F32)[:, None]
    f = jnp.linspace(1e-4, HY_BANDS - 1, HY_BANDS, dtype=F32)[None, :]
    z = jnp.concatenate([t, jnp.cos(f * w), -jnp.sin(f * w)], axis=-1)
    emb_pad = LANES - HY_EMB
    lag = jnp.concatenate([jnp.arange(L), jnp.zeros((1,), jnp.int32), jnp.arange(L - 1, 0, -1)])
    r = jnp.arange(2 * L)
    flags = jnp.stack([(r < L).astype(F32), (r != L).astype(F32)], axis=1)
    z = jnp.concatenate([z[lag], jnp.zeros((2 * L, emb_pad - 2), F32), flags], axis=1)
    w1 = jnp.pad(w1, ((0, emb_pad), (0, 0)))
    tl = min(2 * L, 512)
    C = w3.shape[1] // 2
    full = lambda a: pl.BlockSpec(a.shape, lambda i: (0, 0))
    b1, b2, freq, decay = b1[None, :], b2[None, :], freq[None, :], decay[None, :]
    return pl.pallas_call(
        _hy_filter_kernel,
        grid=(2 * L // tl,),
        in_specs=[pl.BlockSpec((tl, LANES), lambda i: (i, 0)),
                  full(w1), full(b1), full(w2), full(b2), full(w3), full(freq), full(decay)],
        out_specs=pl.BlockSpec((tl, C), lambda i: (i, 0)),
        out_shape=jax.ShapeDtypeStruct((2 * L, C), F32),
        compiler_params=_cparams("parallel"),
        name="hy_filter",
    )(z, w1, b1, w2, b2, w3, freq, decay)


def _dft_tables(N1, N2):
    N = N1 * N2
    two_pi = 2.0 * math.pi

    def cs(num, den):
        ang = (two_pi / den) * (num % den).astype(F32)
        return jnp.cos(ang), jnp.sin(ang)

    a1 = jnp.arange(N1, dtype=jnp.int32)
    a2 = jnp.arange(N2, dtype=jnp.int32)
    c1, s1 = cs(a1[:, None] * a1[None, :], N1)
    c2, s2 = cs(a2[:, None] * a2[None, :], N2)
    ct, st = cs(a2[:, None] * a1[None, :], N)

    def stack(re, im):
        return jnp.concatenate([jnp.concatenate([re, -im], axis=-1),
                                jnp.concatenate([im, re], axis=-1)], axis=-2)

    tr = ct[:, :, None] * c1[None] - st[:, :, None] * s1[None]
    ti = -(ct[:, :, None] * s1[None] + st[:, :, None] * c1[None])
    h = N1 // 2
    m1_data = stack(tr[:, :, :h], ti[:, :, :h])
    m1_real = jnp.concatenate([tr, ti], axis=-2)
    m2 = stack(c2, -s2)
    ctk, stk = ct.T, st.T
    gr = ctk[:, :, None] * c2.T[None] - stk[:, :, None] * s2.T[None]
    gi = ctk[:, :, None] * s2.T[None] + stk[:, :, None] * c2.T[None]
    m2inv = stack(gr, gi)
    er, ei = c1.T[:h] / N, s1.T[:h] / N
    m3 = stack(er, ei)
    return [_hi_lo_rows(m) for m in (m1_data, m1_real, m2, m2inv, m3)]


def _hi_lo_rows(m):
    hi = m.astype(BF16)
    lo = (m - hi.astype(F32)).astype(BF16)
    return jnp.concatenate([hi, lo], axis=-2)


def _dot3(m2, x):
    M = m2.shape[0] // 2
    x_hi = x.astype(BF16)
    x_lo = (x - x_hi.astype(F32)).astype(BF16)
    a = jnp.dot(m2, x_hi, preferred_element_type=F32)
    return a[:M] + a[M:] + jnp.dot(m2[:M], x_lo, preferred_element_type=F32)


def _hy_pass_a_kernel(u_ref, m_ref, o_ref):
    n1 = o_ref.shape[1]
    for j in range(SUBLANES):
        xj = jnp.concatenate([u_ref[0, :, j, :], u_ref[1, :, j, :]], axis=0)
        a = _dot3(m_ref[j], xj)
        o_ref[0, :, j, :] = a[:n1]
        o_ref[1, :, j, :] = a[n1:]


def _hy_pass_a(u4, m1, n_pairs):
    _, h, N2, C = u4.shape
    N1 = 2 * h
    return pl.pallas_call(
        _hy_pass_a_kernel,
        grid=(n_pairs, N2 // SUBLANES),
        in_specs=[pl.BlockSpec((2, h, SUBLANES, C), lambda p, j: (p, 0, j, 0)),
                  pl.BlockSpec((SUBLANES, 4 * N1, N1), lambda p, j: (j, 0, 0))],
        out_specs=pl.BlockSpec((2, N1, SUBLANES, C), lambda p, j: (0, 0, j, p)),
        out_shape=jax.ShapeDtypeStruct((2, N1, N2, n_pairs * C), F32),
        compiler_params=_cparams("parallel", "parallel"),
        name="hy_pass_a",
    )(u4, m1)


def _hy_spec_kernel(a_ref, m2_ref, o_ref):
    n2 = a_ref.shape[2]
    for j in range(SUBLANES):
        a = jnp.concatenate([a_ref[0, j], a_ref[1, j]], axis=0)
        x = _dot3(m2_ref[...], a)
        o_ref[0, j] = x[:n2]
        o_ref[1, j] = x[n2:]


def _hy_spectrum(a, m2):
    _, N1, N2, C = a.shape
    spec = pl.BlockSpec((2, SUBLANES, N2, C), lambda k: (0, k, 0, 0))
    return pl.pallas_call(
        _hy_spec_kernel,
        grid=(N1 // SUBLANES,),
        in_specs=[spec, pl.BlockSpec((4 * N2, 2 * N2), lambda k: (0, 0))],
        out_specs=spec,
        out_shape=jax.ShapeDtypeStruct(a.shape, F32),
        compiler_params=_cparams("parallel"),
        name="hy_spectrum",
    )(a, m2)


def _hy_pass_b_kernel(a_ref, k_ref, m2_ref, mi_ref, o_ref):
    n2 = a_ref.shape[2]
    for j in range(SUBLANES):
        a = jnp.concatenate([a_ref[0, j], a_ref[1, j]], axis=0)
        x = _dot3(m2_ref[...], a)
        xr, xi = x[:n2], x[n2:]
        kr, ki = k_ref[0, j], k_ref[1, j]
        y = jnp.concatenate([xr * kr - xi * ki, xr * ki + xi * kr], axis=0)
        b = _dot3(mi_ref[j], y)
        o_ref[0, :, j, :] = b[:n2]
        o_ref[1, :, j, :] = b[n2:]


def _hy_pass_b(a, kf, m2, m2inv, n_pairs):
    _, N1, N2, PC = a.shape
    C = PC // n_pairs
    return pl.pallas_call(
        _hy_pass_b_kernel,
        grid=(n_pairs, N1 // SUBLANES),
        in_specs=[pl.BlockSpec((2, SUBLANES, N2, C), lambda p, k: (0, k, 0, p)),
                  pl.BlockSpec((2, SUBLANES, N2, C), lambda p, k: (0, k, 0, 0)),
                  pl.BlockSpec((4 * N2, 2 * N2), lambda p, k: (0, 0)),
                  pl.BlockSpec((SUBLANES, 4 * N2, 2 * N2), lambda p, k: (k, 0, 0))],
        out_specs=pl.BlockSpec((2, N2, SUBLANES, C), lambda p, k: (0, 0, k, p)),
        out_shape=jax.ShapeDtypeStruct((2, N2, N1, PC), F32),
        compiler_params=_cparams("parallel", "parallel"),
        name="hy_pass_b",
    )(a, kf, m2, m2inv)


def _hy_pass_c_kernel(b_ref, u_ref, x0_ref, bias_ref, m3_ref, o_ref):
    h = o_ref.shape[1]
    bias = bias_ref[...]
    for j in range(SUBLANES):
        bb = jnp.concatenate([b_ref[0, j], b_ref[1, j]], axis=0)
        y = _dot3(m3_ref[...], bb)
        for r in range(2):
            o_ref[r, :, j, :] = x0_ref[r, :, j, :] * (y[r * h:(r + 1) * h] + bias * u_ref[r, :, j, :])


def _hy_pass_c(bq, u4, x04, bias, m3, n_pairs):
    _, N2, N1, PC = bq.shape
    C = PC // n_pairs
    h = N1 // 2
    io = pl.BlockSpec((2, h, SUBLANES, C), lambda p, j: (p, 0, j, 0))
    return pl.pallas_call(
        _hy_pass_c_kernel,
        grid=(n_pairs, N2 // SUBLANES),
        in_specs=[pl.BlockSpec((2, SUBLANES, N1, C), lambda p, j: (0, j, 0, p)),
                  io, io,
                  pl.BlockSpec((1, C), lambda p, j: (0, 0)),
                  pl.BlockSpec((2 * N1, 2 * N1), lambda p, j: (0, 0))],
        out_specs=io,
        out_shape=jax.ShapeDtypeStruct(u4.shape, F32),
        compiler_params=_cparams("parallel", "parallel"),
        name="hy_pass_c",
    )(bq, u4, x04, bias, m3)


def hyena_long_conv(x0, u, kbuf, bias):
    B, L, C = u.shape
    N2 = min(128, L // 32)
    N1 = 2 * L // N2
    n_pairs = B // 2
    m1_data, m1_real, m2, m2inv, m3 = _dft_tables(N1, N2)
    kf = _hy_spectrum(_hy_pass_a(kbuf.reshape(2, N1 // 2, N2, C), m1_real, 1), m2)
    u4 = u.reshape(B, N1 // 2, N2, C)
    a = _hy_pass_a(u4, m1_data, n_pairs)
    bq = _hy_pass_b(a, kf, m2, m2inv, n_pairs)
    y = _hy_pass_c(bq, u4, x0.reshape(u4.shape), bias[None, :], m3, n_pairs)
    return y.reshape(B, L, C)


GDN_QKV = GDN_HEADS * (2 * GDN_DK + GDN_DV)


def _gdn_pre_kernel(p_ref, pp_ref, pn_ref, w_ref, alog_ref, dtb_ref, q_ref, k_ref, v_ref, gb_ref, *, n_first):
    i = pl.program_id(1)
    C = GDN_QKV
    first = jnp.logical_or(i == 0, i == n_first)
    last = jnp.logical_or(i == n_first - 1, i == pl.num_programs(1) - 1)
    y = _conv3(p_ref[:, :C], pp_ref[:, :C], pn_ref[:, :C], w_ref[...], first, last)
    y = _silu(y)
    nk = GDN_HEADS * GDN_DK
    for h in range(GDN_HEADS):
        sl = slice(h * GDN_DK, (h + 1) * GDN_DK)
        qh = y[:, sl]
        kh = y[:, nk + h * GDN_DK:nk + (h + 1) * GDN_DK]
        q_ref[:, sl] = qh * (lax.rsqrt(jnp.sum(qh * qh, axis=-1, keepdims=True) + EPS) * (GDN_DK ** -0.5))
        k_ref[:, sl] = kh * lax.rsqrt(jnp.sum(kh * kh, axis=-1, keepdims=True) + EPS)
    v_ref[...] = y[:, 2 * nk:]
    s = p_ref[:, C:]
    lane = lax.broadcasted_iota(jnp.int32, s.shape, 1)
    xa = s + dtb_ref[...]
    softplus = jnp.maximum(xa, 0.0) + jnp.log1p(jnp.exp(-jnp.abs(xa)))
    g = -jnp.exp(alog_ref[...]) * softplus
    gb_ref[...] = jnp.where(lane < 2 * GDN_HEADS, g, jnp.where(lane < 4 * GDN_HEADS, jax.nn.sigmoid(s), 0.0))


def gdn_pre(p_gdn, conv_w, a_log, dt_bias, tm, first_rows):
    B, S, C = p_gdn.shape
    assert first_rows % tm == 0 and S % tm == 0
    main, prev, nxt = _halo_specs(tm, S, C)
    pad = LANES - 2 * GDN_HEADS
    alog = jnp.pad(a_log.reshape(1, -1), ((0, 0), (0, pad)))
    dtb = jnp.pad(dt_bias.reshape(1, -1), ((0, 0), (0, pad)))
    nv = GDN_HEADS * GDN_DV
    o_spec = pl.BlockSpec((None, tm, nv), lambda b, i: (b, i, 0))
    return pl.pallas_call(
        functools.partial(_gdn_pre_kernel, n_first=first_rows // tm),
        grid=(B, S // tm),
        in_specs=[main, prev, nxt,
                  pl.BlockSpec((3, GDN_QKV), lambda b, i: (0, 0)),
                  pl.BlockSpec((1, LANES), lambda b, i: (0, 0)),
                  pl.BlockSpec((1, LANES), lambda b, i: (0, 0))],
        out_specs=[o_spec, o_spec, o_spec, pl.BlockSpec((None, tm, LANES), lambda b, i: (b, i, 0))],
        out_shape=[jax.ShapeDtypeStruct((B, S, nv), F32)] * 3 + [jax.ShapeDtypeStruct((B, S, LANES), F32)],
        compiler_params=_cparams("parallel", "parallel"),
        name="gdn_pre",
    )(p_gdn, p_gdn, p_gdn, conv_w, alog, dtb)


def _bdot(a, b):
    return jnp.dot(a.astype(BF16), b.astype(BF16), preferred_element_type=F32)


def _bdot_nt(a, b):
    return lax.dot_general(a.astype(BF16), b.astype(BF16), (((1,), (1,)), ((), ())), preferred_element_type=F32)


def _bdot_tn(a, b):
    return lax.dot_general(a.astype(BF16), b.astype(BF16), (((0,), (0,)), ((), ())), preferred_element_type=F32)


GDN_STEP_CHUNKS = 4


def _gdn_scan_kernel(qf_ref, kf_ref, vf_ref, gf_ref, qb_ref, kb_ref, vb_ref, gb_ref, of_ref, ob_ref, s_ref):
    C = GDN_CHUNK
    H = GDN_HEADS
    G = qf_ref.shape[0] // C
    in_refs = ((qf_ref, kf_ref, vf_ref, gf_ref), (qb_ref, kb_ref, vb_ref, gb_ref))
    o_refs = (of_ref, ob_ref)

    @pl.when(pl.program_id(1) == 0)
    def _():
        s_ref[...] = jnp.zeros_like(s_ref)

    ri = lax.broadcasted_iota(jnp.int32, (C, C), 0)
    ci = lax.broadcasted_iota(jnp.int32, (C, C), 1)
    eye = (ri == ci).astype(F32)
    incl = ((ci <= ri), (ci >= ri))
    strict = ((ci < ri), (ci > ri))
    rows = lambda g: slice(g * C, (g + 1) * C)
    cols = lambda h: slice(h * GDN_DK, (h + 1) * GDN_DK)
    chains = [(d, g, h) for d in range(2) for g in range(G) for h in range(H)]

    gbv = {(d, g): in_refs[d][3][rows(g), :] for d in range(2) for g in range(G)}
    gc_all = {dg: jnp.dot(incl[dg[0]].astype(F32), gbv[dg], precision=HI, preferred_element_type=F32)
              for dg in gbv}
    gc_t = {dg: gc_all[dg].T for dg in gbv}

    kk, gamma, rhs, qe, kdec, gend = {}, {}, {}, {}, {}, {}
    for ch in chains:
        d, g, h = ch
        c = d * H + h
        q, k, v = (in_refs[d][n][rows(g), cols(h)] for n in range(3))
        end = C - 1 if d == 0 else 0
        beta = gbv[d, g][:, 2 * H + c:2 * H + c + 1]
        gc_c = gc_all[d, g][:, c:c + 1]
        gc_r = gc_t[d, g][c:c + 1, :]
        g_tot = gc_all[d, g][end:end + 1, c:c + 1]
        gamma[ch] = jnp.where(incl[d], jnp.exp(jnp.where(incl[d], gc_c - gc_r, 0.0)), 0.0)
        e_c = jnp.exp(gc_c)
        kb = k * beta
        kk[ch] = _bdot_nt(jnp.concatenate([kb, q], axis=0), k)
        rhs[ch] = jnp.concatenate([v * beta, kb * e_c], axis=1).astype(BF16)
        qe[ch] = q * e_c
        kdec[ch] = (k * jnp.exp(g_tot - gc_c)).astype(BF16)
        gend[ch] = jnp.exp(g_tot)
    m = {ch: jnp.where(strict[ch[0]], kk[ch][:C] * gamma[ch], 0.0) for ch in chains}
    a_intra = {ch: (kk[ch][C:] * gamma[ch]).astype(BF16) for ch in chains}
    t = {ch: eye - m[ch] for ch in chains}
    pw = m
    for _ in range(5):
        pw = {ch: _bdot(pw[ch], pw[ch]) for ch in chains}
        t = {ch: t[ch] + _bdot(t[ch], pw[ch]) for ch in chains}
    uw = {ch: jnp.dot(t[ch].astype(BF16), rhs[ch], preferred_element_type=F32) for ch in chains}
    wq = {ch: jnp.concatenate([uw[ch][:, GDN_DV:], qe[ch]], axis=0).astype(BF16) for ch in chains}

    heads = [(d, h) for d in range(2) for h in range(H)]
    s = {dh: s_ref[dh[0] * H + dh[1]] for dh in heads}
    for j in range(G):
        at = lambda dh: (dh[0], j if dh[0] == 0 else G - 1 - j, dh[1])
        ws = {dh: jnp.dot(wq[at(dh)], s[dh].astype(BF16), preferred_element_type=F32) for dh in heads}
        v_new = {dh: uw[at(dh)][:, :GDN_DV] - ws[dh][:C] for dh in heads}
        for dh in heads:
            d, g, h = at(dh)
            o_refs[d][rows(g), cols(h)] = ws[dh][C:] + jnp.dot(a_intra[at(dh)], v_new[dh].astype(BF16),
                                                               preferred_element_type=F32)
        s = {dh: s[dh] * gend[at(dh)] + _bdot_tn(kdec[at(dh)], v_new[dh]) for dh in heads}
    for dh in heads:
        s_ref[dh[0] * H + dh[1]] = s[dh]


def gdn_scan(q, k, v, gb, n_ctx_chunks):
    B, Lt, NV = q.shape
    R = GDN_CHUNK * GDN_STEP_CHUNKS
    n = Lt // R
    n_ctx = n_ctx_chunks // GDN_STEP_CHUNKS
    n_lat = n - n_ctx
    assert n * R == Lt and n_ctx * GDN_STEP_CHUNKS == n_ctx_chunks

    def fwd(b, i):
        return (b, jnp.where(i < n_ctx, n_lat + i, i - n_ctx), 0)

    def bwd(b, i):
        return (b, n - 1 - i, 0)

    def specs(imap):
        return [pl.BlockSpec((None, R, NV), imap)] * 3 + [pl.BlockSpec((None, R, LANES), imap)]

    return pl.pallas_call(
        _gdn_scan_kernel,
        grid=(B, n),
        in_specs=specs(fwd) + specs(bwd),
        out_specs=[pl.BlockSpec((None, R, NV), fwd), pl.BlockSpec((None, R, NV), bwd)],
        out_shape=[jax.ShapeDtypeStruct((B, Lt, NV), F32)] * 2,
        scratch_shapes=[pltpu.VMEM((2 * GDN_HEADS, GDN_DK, GDN_DV), F32)],
        compiler_params=_cparams("parallel", "arbitrary"),
        name="gdn_scan",
    )(q, k, v, gb, q, k, v, gb)


MLA_QK = MLA_NOPE + MLA_ROPE
MLA_KVIN = MLA_KV_LORA + LANES


def _rope_partner(w_rope):
    nf = MLA_ROPE // 4
    parts = []
    for half in range(2):
        a = w_rope[..., half * 2 * nf:half * 2 * nf + nf]
        b = w_rope[..., half * 2 * nf + nf:(half + 1) * 2 * nf]
        parts += [-b, a]
    return jnp.concatenate(parts, axis=-1)


def _head_pad(nope, rope):
    pad = jnp.zeros(nope.shape[:-1] + (LANES - MLA_QK,), nope.dtype)
    out = jnp.concatenate([nope, rope, pad], axis=-1)
    return out.reshape(out.shape[:-2] + (MLA_HEADS * LANES,))


def mla_weights(w_uq, w_ukv):
    wq = w_uq.reshape(MLA_Q_LORA, MLA_HEADS, MLA_QK)
    qn, qr = wq[..., :MLA_NOPE], wq[..., MLA_NOPE:]
    wq2 = jnp.concatenate([_head_pad(qn, qr), _head_pad(jnp.zeros_like(qn), _rope_partner(qr))], axis=-1)
    wkv = w_ukv.reshape(MLA_KV_LORA, MLA_HEADS, MLA_NOPE + MLA_V)
    kn, vv = wkv[..., :MLA_NOPE], wkv[..., MLA_NOPE:]
    eye = jnp.broadcast_to(jnp.eye(MLA_ROPE, dtype=F32)[:, None, :], (MLA_ROPE, MLA_HEADS, MLA_ROPE))
    z_kn = jnp.zeros((MLA_ROPE, MLA_HEADS, MLA_NOPE), F32)
    z_rope = jnp.zeros((MLA_KV_LORA, MLA_HEADS, MLA_ROPE), F32)
    top = jnp.concatenate([_head_pad(kn, z_rope), _head_pad(jnp.zeros_like(kn), z_rope), _head_pad(vv, z_rope)],
                          axis=-1)
    mid = jnp.concatenate([_head_pad(z_kn, eye), _head_pad(z_kn, _rope_partner(eye)),
                           jnp.zeros((MLA_ROPE, MLA_HEADS * LANES), F32)], axis=-1)
    bot = jnp.zeros((MLA_KVIN - MLA_KV_LORA - MLA_ROPE, top.shape[1]), F32)
    return wq2.astype(BF16), jnp.concatenate([top, mid, bot], axis=0).astype(BF16)


def rope_tables(row, col):
    nf = MLA_ROPE // 4
    inv_freq = ROPE_THETA ** (-jnp.arange(nf, dtype=F32) / nf)
    ang = jnp.concatenate([row.astype(F32)[:, None] * inv_freq[None, :]] * 2
                          + [col.astype(F32)[:, None] * inv_freq[None, :]] * 2, axis=-1)
    n = ang.shape[0]
    pad = jnp.zeros((n, LANES - MLA_QK), F32)
    cos = jnp.concatenate([jnp.ones((n, MLA_NOPE), F32), jnp.cos(ang), pad], axis=-1)
    sin = jnp.concatenate([jnp.zeros((n, MLA_NOPE), F32), jnp.sin(ang), pad], axis=-1)
    return cos, sin


def _mla_proj_kernel(p_ref, gq_ref, gkv_ref, wq_ref, wk_ref, cos_ref, sin_ref, q_ref, k_ref, v_ref):
    HL = MLA_HEADS * LANES
    cos = jnp.concatenate([cos_ref[...]] * MLA_HEADS, axis=1)
    sin = jnp.concatenate([sin_ref[...]] * MLA_HEADS, axis=1)
    cq = p_ref[:, :MLA_Q_LORA]
    cqn = (cq * lax.rsqrt(jnp.mean(cq * cq, axis=-1, keepdims=True) + EPS) * gq_ref[...]).astype(BF16)
    ck = p_ref[:, MLA_Q_LORA:]
    lane = lax.broadcasted_iota(jnp.int32, ck.shape, 1)
    is_kv = lane < MLA_KV_LORA
    ms = jnp.sum(jnp.where(is_kv, ck * ck, 0.0), axis=-1, keepdims=True) * (1.0 / MLA_KV_LORA)
    ckn = jnp.where(is_kv, ck * lax.rsqrt(ms + EPS) * gkv_ref[...], ck).astype(BF16)
    qq = jnp.dot(cqn, wq_ref[...], preferred_element_type=F32)
    kk = jnp.dot(ckn, wk_ref[...], preferred_element_type=F32)
    q_ref[...] = ((qq[:, :HL] * cos + qq[:, HL:] * sin) * (MLA_QK ** -0.5 * math.log2(math.e))).astype(q_ref.dtype)
    k_ref[...] = (kk[:, :HL] * cos + kk[:, HL:2 * HL] * sin).astype(k_ref.dtype)
    vv = kk[:, 2 * HL:]
    vlane = lax.broadcasted_iota(jnp.int32, vv.shape, 1)
    v_ref[...] = jnp.where(vlane % LANES == MLA_V, 1.0, vv).astype(v_ref.dtype)


def mla_proj(p_mla, gq, gkv, wq2, wk2, cos, sin, tm):
    B, S, C = p_mla.shape
    tm = min(tm, S)
    HL = MLA_HEADS * LANES
    gkv = jnp.pad(gkv, ((0, 0), (0, MLA_KVIN - MLA_KV_LORA)))
    row = lambda n: pl.BlockSpec((None, tm, n), lambda b, i: (b, i, 0))
    full = lambda a: pl.BlockSpec(a.shape, lambda b, i: (0, 0))
    tab = pl.BlockSpec((tm, LANES), lambda b, i: (i, 0))
    return pl.pallas_call(
        _mla_proj_kernel,
        grid=(B, S // tm),
        in_specs=[row(C), full(gq), full(gkv), full(wq2), full(wk2), tab, tab],
        out_specs=[row(HL)] * 3,
        out_shape=[jax.ShapeDtypeStruct((B, S, HL), BF16)] * 3,
        compiler_params=_cparams("parallel", "parallel"),
        name="mla_proj",
    )(p_mla, gq, gkv, wq2, wk2, cos, sin)


ATT_SLAB = 32


def _mla_attn_kernel(q_ref, k_ref, v_ref, o_ref, *, tk):
    tq = q_ref.shape[0]
    Tk = k_ref.shape[0]
    n_full, rem = Tk // tk, Tk % tk
    heads = (slice(0, LANES), slice(LANES, 2 * LANES))
    qs = [q_ref[:, hs] for hs in heads]

    def step(carry, start, size):
        ss = [lax.dot_general(qs[h], k_ref[pl.ds(start, size), heads[h]], (((1,), (1,)), ((), ())),
                              preferred_element_type=F32) for h in range(2)]
        out = []
        for h in range(2):
            m, acc = carry[h]
            m_new = jnp.maximum(m, jnp.max(ss[h], axis=-1, keepdims=True))
            p = jnp.concatenate([jnp.exp2(ss[h][r:r + ATT_SLAB] - m_new[r:r + ATT_SLAB]).astype(BF16)
                                 for r in range(0, tq, ATT_SLAB)], axis=0)
            acc = acc * jnp.exp2(m - m_new) + jnp.dot(p, v_ref[pl.ds(start, size), heads[h]],
                                                      preferred_element_type=F32)
            out.append((m_new, acc))
        return tuple(out)

    carry = tuple((jnp.full((tq, 1), -jnp.inf, F32), jnp.zeros((tq, LANES), F32)) for _ in range(2))
    if n_full:
        carry = lax.fori_loop(0, n_full, lambda c, cr: step(cr, pl.multiple_of(c * tk, tk), tk), carry,
                              unroll=2 if n_full % 2 == 0 else 1)
    if rem:
        carry = step(carry, n_full * tk, rem)
    o0, o1 = [acc / acc[:, MLA_V:MLA_V + 1] for _, acc in carry]
    lane = lax.broadcasted_iota(jnp.int32, (tq, LANES), 1)
    o_ref[...] = jnp.where(lane < MLA_V, o0, pltpu.roll(o1, MLA_V, 1)).astype(o_ref.dtype)


def mla_attn(q, k, v, q_rows, k_rows, tq, tk):
    B = q.shape[0]
    (q0, S), (k0, Tk) = q_rows, k_rows
    tq = min(tq, S)
    assert q0 % tq == 0 and k0 % Tk == 0
    qb, kb = q0 // tq, k0 // Tk
    return pl.pallas_call(
        functools.partial(_mla_attn_kernel, tk=tk),
        grid=(B, MLA_HEADS // 2, S // tq),
        in_specs=[pl.BlockSpec((None, tq, 2 * LANES), lambda b, h, i: (b, qb + i, h)),
                  pl.BlockSpec((None, Tk, 2 * LANES), lambda b, h, i: (b, kb, h)),
                  pl.BlockSpec((None, Tk, 2 * LANES), lambda b, h, i: (b, kb, h))],
        out_specs=pl.BlockSpec((None, tq, LANES), lambda b, h, i: (b, i, h)),
        out_shape=jax.ShapeDtypeStruct((B, S, MLA_HEADS * MLA_V), BF16),
        compiler_params=_cparams("parallel", "parallel", "arbitrary"),
        name="mla_attn",
    )(q, k, v)


def _route(logits, rb):
    lane = lax.broadcasted_iota(jnp.int32, logits.shape, 1)
    neg = -jnp.inf
    scores = jax.nn.sigmoid(logits)
    sel = scores + rb

    def top2(masked):
        m1 = jnp.max(masked, axis=-1, keepdims=True)
        i1 = jnp.min(jnp.where(masked == m1, lane, LANES), axis=-1, keepdims=True)
        rest = jnp.where(lane == i1, neg, masked)
        m2 = jnp.max(rest, axis=-1, keepdims=True)
        i2 = jnp.min(jnp.where(rest == m2, lane, LANES), axis=-1, keepdims=True)
        return m1, i1, m2, i2

    best = None
    for gi in range(N_GROUPS):
        in_g = jnp.logical_and(lane >= gi * EXPERTS_PER_GROUP, lane < (gi + 1) * EXPERTS_PER_GROUP)
        m1, _, m2, _ = top2(jnp.where(in_g, sel, neg))
        gs = m1 + m2
        if best is None:
            best, grp = gs, jnp.zeros_like(gs, dtype=jnp.int32)
        else:
            better = gs > best
            grp = jnp.where(better, gi, grp)
            best = jnp.where(better, gs, best)
    lo = grp * EXPERTS_PER_GROUP
    in_grp = jnp.logical_and(lane >= lo, lane < lo + EXPERTS_PER_GROUP)
    _, i1, _, i2 = top2(jnp.where(in_grp, sel, neg))
    picked = jnp.where(jnp.logical_or(lane == i1, lane == i2), scores, 0.0)
    gate = picked / jnp.sum(picked, axis=-1, keepdims=True)
    return jnp.where(lane == LANES - 1, grp.astype(F32), gate)


MERGE_SPLIT = 2


def _merge_kernel(hyv_ref, of_ref, ob_ref, pg_ref, at_ref, x_ref, gt1_ref, sc2_ref, sh2_ref, gng_ref, n2g_ref,
                  whb_ref, wgd_ref, wml_ref, wo_ref, rw_ref, rb_ref, xo_ref, h2_ref, gate_ref):
    nv = GDN_HEADS * GDN_DV
    tm, D = x_ref.shape
    slabs = [slice(r, r + tm // MERGE_SPLIT) for r in range(0, tm, tm // MERGE_SPLIT)]

    def gdn_gate(rs):
        o = of_ref[rs, :] + ob_ref[rs, :]
        ys = []
        for h in range(GDN_HEADS):
            sl = slice(h * GDN_DV, (h + 1) * GDN_DV)
            oh = o[:, sl]
            on = oh * lax.rsqrt(jnp.mean(oh * oh, axis=-1, keepdims=True) + EPS) * gng_ref[...]
            ys.append(on * _silu(pg_ref[rs, sl]))
        return jnp.concatenate(ys, axis=1).astype(BF16)

    y_in = [gdn_gate(rs) for rs in slabs]
    branches = [(jnp.dot(hyv_ref[rs, :].astype(BF16), whb_ref[...], preferred_element_type=F32),
                 jnp.dot(y, wgd_ref[...], preferred_element_type=F32),
                 jnp.dot(at_ref[rs, :], wml_ref[...], preferred_element_type=F32)) for rs, y in zip(slabs, y_in)]
    merged = [(jax.nn.sigmoid(pg_ref[rs, nv:nv + D]) * y_hy
               + jax.nn.sigmoid(pg_ref[rs, nv + D:nv + 2 * D]) * y_gdn
               + jax.nn.sigmoid(pg_ref[rs, nv + 2 * D:]) * y_mla).astype(BF16)
              for rs, (y_hy, y_gdn, y_mla) in zip(slabs, branches)]
    mix = [jnp.dot(m, wo_ref[...], preferred_element_type=F32) for m in merged]
    h2s = []
    for rs, mx in zip(slabs, mix):
        xn = x_ref[rs, :] + gt1_ref[...] * mx
        xo_ref[rs, :] = xn
        y2 = xn * lax.rsqrt(jnp.mean(xn * xn, axis=-1, keepdims=True) + EPS) * n2g_ref[...]
        h2 = y2 * (1.0 + sc2_ref[...]) + sh2_ref[...]
        h2_ref[rs, :] = h2.astype(h2_ref.dtype)
        h2s.append(h2)
    logits = []
    for h2 in h2s:
        h_hi = h2.astype(BF16)
        h_lo = (h2 - h_hi.astype(F32)).astype(BF16)
        a = jnp.dot(h_hi, rw_ref[...], preferred_element_type=F32)
        logits.append(a[:, :LANES] + a[:, LANES:] + jnp.dot(h_lo, rw_ref[:, :LANES], preferred_element_type=F32))
    for rs, lg in zip(slabs, logits):
        gate_ref[rs, :] = _route(lg, rb_ref[...])


def merge_out(hyv, o_f, o_b, pg, attn, x, gt1, sc2, sh2, gdn_norm_g, norm2_g, w_hy, w_gdn, w_mla, w_out,
              router_w, router_b, tm, o_row0):
    B, S, D = x.shape
    tm = min(tm, S)
    assert o_row0 % tm == 0
    ob0 = o_row0 // tm
    row = lambda n: pl.BlockSpec((None, tm, n), lambda b, i: (b, i, 0))
    full = lambda a: pl.BlockSpec(a.shape, lambda b, i: (0, 0))
    nv = GDN_HEADS * GDN_DV
    o_spec = pl.BlockSpec((None, tm, nv), lambda b, i: (b, ob0 + i, 0))
    return pl.pallas_call(
        _merge_kernel,
        grid=(B, S // tm),
        in_specs=[row(HY_W), o_spec, o_spec, row(pg.shape[2]), row(MLA_HEADS * MLA_V), row(D),
                  _mod_spec(gt1, D), _mod_spec(sc2, D), _mod_spec(sh2, D),
                  full(gdn_norm_g), full(norm2_g), full(w_hy), full(w_gdn), full(w_mla), full(w_out),
                  full(router_w), full(router_b)],
        out_specs=[row(D), row(D), row(LANES)],
        out_shape=[jax.ShapeDtypeStruct((B, S, D), F32), jax.ShapeDtypeStruct((B, S, D), BF16),
                   jax.ShapeDtypeStruct((B, S, LANES), F32)],
        compiler_params=_cparams("parallel", "parallel"),
        name="merge_out",
    )(hyv, o_f, o_b, pg, attn, x, gt1, sc2, sh2, gdn_norm_g, norm2_g, w_hy, w_gdn, w_mla, w_out,
      router_w, router_b)


MOE_BLK = 128
MOE_STEP = 2


def _moe_kernel(h_ref, gate_ref, w1_ref, w3_ref, w2_ref, x_ref, gt2_ref, fg_ref, o_ref,
                xs_ref, gs_ref, y_ref, slot_ref, seg_ref, *, final_norm):
    e = pl.program_id(1)
    tm, D = h_ref.shape
    A = xs_ref.shape[0]

    @pl.when(e == 0)
    def _():
        gate = gate_ref[...]
        lane = lax.broadcasted_iota(jnp.int32, gate.shape, 1)
        grp = gate[:, LANES - 1:LANES]
        member = jnp.where(jnp.logical_and(lane.astype(F32) == grp, lane < N_GROUPS), 1.0, 0.0)
        ri = lax.broadcasted_iota(jnp.int32, (tm, tm), 0)
        ci = lax.broadcasted_iota(jnp.int32, (tm, tm), 1)
        before = jnp.where(ci < ri, 1.0, 0.0).astype(BF16)
        rank = jnp.dot(before, member.astype(BF16), preferred_element_type=F32)
        cnt = jnp.sum(member, axis=0, keepdims=True)
        blocks = jnp.floor((cnt + (MOE_BLK - 1)) * (1.0 / MOE_BLK))
        padded = blocks * MOE_BLK
        l1 = lane[0:1, :]
        p0, p1, p2 = padded[:, 0:1], padded[:, 1:2], padded[:, 2:3]
        start = jnp.where(l1 == 0, 0.0, jnp.where(l1 == 1, p0, jnp.where(l1 == 2, p0 + p1, p0 + p1 + p2)))
        slot = jnp.sum(member * (start + rank), axis=-1, keepdims=True)
        slot_b = jnp.broadcast_to(slot, (tm, LANES))
        slot_ref[...] = slot_b
        start_i, blocks_i = start.astype(jnp.int32), blocks.astype(jnp.int32)
        for g in range(N_GROUPS):
            seg_ref[g] = start_i[0, g]
            seg_ref[N_GROUPS + g] = blocks_i[0, g]
        slot_row = slot_b.T[0:1, :]
        pick = jnp.where(lax.broadcasted_iota(jnp.int32, (A, tm), 0).astype(F32) == slot_row, 1.0, 0.0).astype(BF16)
        g_hi = gate.astype(BF16)
        g_lo = (gate - g_hi.astype(F32)).astype(BF16)
        got = jnp.dot(pick, jnp.concatenate([h_ref[...], g_hi, g_lo], axis=1), preferred_element_type=F32)
        xs_ref[...] = got[:, :D].astype(BF16)
        gs_ref[...] = got[:, D:D + LANES] + got[:, D + LANES:]
        y_ref[...] = jnp.zeros_like(y_ref)

    g = e // (EXPERTS_PER_GROUP // MOE_STEP)
    first = seg_ref[g]
    lane_b = lax.broadcasted_iota(jnp.int32, (MOE_BLK, LANES), 1)

    def block(b, carry):
        rows = pl.ds(pl.multiple_of(first + b * MOE_BLK, MOE_BLK), MOE_BLK)
        xb = xs_ref[rows, :]
        gsb = gs_ref[rows, :]
        up = [(jnp.dot(xb, w1_ref[j], preferred_element_type=F32), jnp.dot(xb, w3_ref[j], preferred_element_type=F32))
              for j in range(MOE_STEP)]
        he = [(_silu(a) * b3).astype(BF16) for a, b3 in up]
        down = [jnp.dot(he[j], w2_ref[j], preferred_element_type=F32) for j in range(MOE_STEP)]
        acc = y_ref[rows, :]
        for j in range(MOE_STEP):
            ge = jnp.sum(jnp.where(lane_b == e * MOE_STEP + j, gsb, 0.0), axis=-1, keepdims=True)
            acc = acc + ge * down[j]
        y_ref[rows, :] = acc
        return carry

    lax.fori_loop(0, seg_ref[N_GROUPS + g], block, 0)

    @pl.when(e == pl.num_programs(1) - 1)
    def _():
        back = jnp.where(lax.broadcasted_iota(jnp.int32, (tm, A), 1).astype(F32) == slot_ref[:, 0:1], 1.0, 0.0)
        moe_out = jnp.dot(back.astype(BF16), y_ref[...].astype(BF16), preferred_element_type=F32)
        xn = x_ref[...] + gt2_ref[...] * moe_out
        if final_norm:
            xn = xn * lax.rsqrt(jnp.mean(xn * xn, axis=-1, keepdims=True) + EPS) * fg_ref[...]
        o_ref[...] = xn


def moe(h2, gate, w1, w3, w2, x, gt2, final_g, S, tm, final_norm):
    T, D = x.shape
    E, _, FF = w1.shape
    if gt2.shape[0] == 1:
        tm = min(tm, T)
        gt_spec = pl.BlockSpec((None, 1, D), lambda i, e: (0, 0, 0))
    else:
        tm = min(tm, S)
        per_b = S // tm
        gt_spec = pl.BlockSpec((None, 1, D), lambda i, e: (i // per_b, 0, 0))
    A = tm + N_GROUPS * MOE_BLK
    return pl.pallas_call(
        functools.partial(_moe_kernel, final_norm=final_norm),
        grid=(T // tm, E // MOE_STEP),
        in_specs=[pl.BlockSpec((tm, D), lambda i, e: (i, 0)),
                  pl.BlockSpec((tm, LANES), lambda i, e: (i, 0)),
                  pl.BlockSpec((MOE_STEP, D, FF), lambda i, e: (e, 0, 0)),
                  pl.BlockSpec((MOE_STEP, D, FF), lambda i, e: (e, 0, 0)),
                  pl.BlockSpec((MOE_STEP, FF, D), lambda i, e: (e, 0, 0)),
                  pl.BlockSpec((tm, D), lambda i, e: (i, 0)),
                  gt_spec,
                  pl.BlockSpec((1, D), lambda i, e: (0, 0))],
        out_specs=pl.BlockSpec((tm, D), lambda i, e: (i, 0)),
        out_shape=jax.ShapeDtypeStruct((T, D), F32),
        scratch_shapes=[pltpu.VMEM((A, D), BF16), pltpu.VMEM((A, LANES), F32), pltpu.VMEM((A, D), F32),
                        pltpu.VMEM((tm, LANES), F32), pltpu.SMEM((2 * N_GROUPS,), jnp.int32)],
        compiler_params=_cparams("parallel", "arbitrary"),
        name="moe",
    )(h2, gate, w1, w3, w2, x, gt2, final_g)


IN_SIZES = (3 * HY_W, GDN_QKV, GDN_HEADS * GDN_DV, 2 * GDN_HEADS, 2 * GDN_HEADS, MLA_Q_LORA, MLA_KV_LORA, MLA_ROPE)


def _split_w_in(w_in):
    D = w_in.shape[0]
    parts, off = [], 0
    for n in IN_SIZES:
        parts.append(w_in[:, off:off + n])
        off += n
    hy, qkv, z, a, b, cq, ckv, kr = parts
    gate = w_in[:, off:]
    zeros = lambda n: jnp.zeros((D, n), w_in.dtype)
    w_hy = hy
    w_gdn = jnp.concatenate([qkv, a, b, zeros(LANES - 4 * GDN_HEADS)], axis=1)
    w_mla = jnp.concatenate([cq, ckv, kr, zeros(LANES - MLA_ROPE)], axis=1)
    w_gate = jnp.concatenate([z, gate], axis=1)
    return [w.astype(BF16) for w in (w_hy, w_gdn, w_mla, w_gate)]


def _layer(x, cx, mod, mod_c, lw, tabs, router_w, router_b, final_g, update_ctx, last):
    B, S, D = x.shape
    Lc = cx.shape[1]
    sh1, sc1, gt1, sh2, sc2, gt2 = [m[:, None, :] for m in jnp.split(mod, 6, axis=-1)]
    csh1, csc1, cgt1, csh2, csc2, cgt2 = [m[:, None, :] for m in jnp.split(mod_c, 6, axis=-1)]
    n1g = lw['norm1_g'][None, :]
    w_hy, w_gdn, w_mla, w_gate = _split_w_in(lw['w_in'])

    Lt = S + Lc
    TM, TMC = 512, 256

    def project(w, joint):
        if joint:
            return in_proj_joint(x, cx, n1g, sc1, sh1, csc1, csh1, w, TMC)
        return in_proj(x, n1g, sc1, sh1, w, TM), in_proj(cx, n1g, csc1, csh1, w, TMC)

    p_hy, c_hy = project(w_hy, False)
    p_gate, c_gate = project(w_gate, False)
    pc_gdn = project(w_gdn, True)
    pc_mla = project(w_mla, True)

    o_f, o_b = gdn_scan(*gdn_pre(pc_gdn, lw['gdn_conv_w'], lw['gdn_a_log'], lw['gdn_dt_bias'], TMC, S),
                        Lc // GDN_CHUNK)

    wq2, wk2 = mla_weights(lw['mla_w_uq'], lw['mla_w_ukv'])
    gq, gkv = lw['mla_q_norm_g'][None, :], lw['mla_kv_norm_g'][None, :]
    q_a, k_a, v_a = mla_proj(pc_mla, gq, gkv, wq2, wk2, tabs[0], tabs[1], TMC)
    attn_l = mla_attn(q_a, k_a, v_a, (0, S), (0, Lt), 512, 1024)

    def hyena(p, L):
        filt = hy_filter(L, lw['hy_f_w1'], lw['hy_f_b1'], lw['hy_f_w2'], lw['hy_f_b2'], lw['hy_f_w3'],
                         lw['hy_f_freq'], lw['hy_decay'])
        x0, u = hy_pre(p, lw['hy_conv_w'], lw['hy_conv_b'][None, :], 512)
        return hyena_long_conv(x0, u, filt, lw['hy_bias'])

    hyv_l = hyena(p_hy, S)

    wb = lambda name: lw[name].astype(BF16)
    rw = jnp.pad(router_w, ((0, 0), (0, LANES - N_EXPERTS)))
    rw_hi = rw.astype(BF16)
    rw = jnp.concatenate([rw_hi, (rw - rw_hi.astype(F32)).astype(BF16)], axis=1)
    rb = jnp.pad(router_b[None, :], ((0, 0), (0, LANES - N_EXPERTS)))
    n2g, gng = lw['norm2_g'][None, :], lw['gdn_norm_g'][None, :]
    w1, w3, w2 = wb('moe_w1'), wb('moe_w3'), wb('moe_w2')

    def finish(xx, hyv, o_row0, pg, attn, gt1_, sc2_, sh2_, gt2_, tm, tm_moe, fin):
        Bx, Sx, _ = xx.shape
        xn, h2, gate = merge_out(hyv, o_f, o_b, pg, attn, xx, gt1_, sc2_, sh2_, gng, n2g, wb('hy_out'), wb('gdn_out'),
                                 wb('mla_out'), wb('w_out'), rw, rb, tm, o_row0)
        out = moe(h2.reshape(Bx * Sx, D), gate.reshape(Bx * Sx, LANES), w1, w3, w2, xn.reshape(Bx * Sx, D),
                  gt2_, final_g, Sx, tm_moe, fin)
        return out.reshape(Bx, Sx, D)

    x_new = finish(x, hyv_l, 0, p_gate, attn_l, gt1, sc2, sh2, gt2, 512, 1024, last)
    if update_ctx:
        hyv_c = hyena(c_hy, Lc)
        attn_c = mla_attn(q_a, k_a, v_a, (S, Lc), (S, Lc), 256, 1024)
        cx = finish(cx, hyv_c, S, c_gate, attn_c, cgt1, csc2, csh2, cgt2, 256, 1024, False)
    return x_new, cx


def kernel(x, c, ctx, c_ctx, w_ada, b_ada, norm1_g, norm2_g, w_in, hy_conv_w, hy_conv_b, hy_f_w1, hy_f_b1, hy_f_w2, hy_f_b2, hy_f_w3, hy_f_freq, hy_decay, hy_bias, hy_out, gdn_conv_w, gdn_a_log, gdn_dt_bias, gdn_norm_g, gdn_out, mla_q_norm_g, mla_w_uq, mla_kv_norm_g, mla_w_ukv, mla_out, w_out, moe_w1, moe_w3, moe_w2, router_w, router_b, final_norm_g):
    per_layer = dict(norm1_g=norm1_g, norm2_g=norm2_g, w_in=w_in, hy_conv_w=hy_conv_w, hy_conv_b=hy_conv_b,
                     hy_f_w1=hy_f_w1, hy_f_b1=hy_f_b1, hy_f_w2=hy_f_w2, hy_f_b2=hy_f_b2, hy_f_w3=hy_f_w3,
                     hy_f_freq=hy_f_freq, hy_decay=hy_decay, hy_bias=hy_bias, hy_out=hy_out,
                     gdn_conv_w=gdn_conv_w, gdn_a_log=gdn_a_log, gdn_dt_bias=gdn_dt_bias, gdn_norm_g=gdn_norm_g,
                     gdn_out=gdn_out, mla_q_norm_g=mla_q_norm_g, mla_w_uq=mla_w_uq, mla_kv_norm_g=mla_kv_norm_g,
                     mla_w_ukv=mla_w_ukv, mla_out=mla_out, w_out=w_out, moe_w1=moe_w1, moe_w3=moe_w3, moe_w2=moe_w2)
    B, S, D = x.shape
    Lc = ctx.shape[1]
    depth = w_ada.shape[0]
    rows = S // GRID_W
    row = jnp.repeat(jnp.arange(rows, dtype=jnp.int32), GRID_W)
    col = jnp.tile(jnp.arange(GRID_W, dtype=jnp.int32), rows)
    zero = jnp.zeros((Lc,), jnp.int32)
    tabs = rope_tables(jnp.concatenate([row, zero]), jnp.concatenate([col, zero]))
    cc = jnp.concatenate([c, c_ctx[None, :], jnp.zeros((2 * SUBLANES - B - 1, D), F32)], axis=0)
    final_g = final_norm_g[None, :]
    cx = ctx
    for l in range(depth):
        lw = {k: v[l] for k, v in per_layer.items()}
        mods = ada_mod(cc, w_ada[l], b_ada[l][None, :])
        x, cx = _layer(x, cx, mods[:B], mods[B:B + 1], lw, tabs, router_w, router_b, final_g,
                       l < depth - 1, l == depth - 1)
    return x
```

```python
import functools
import math

import jax
import jax.numpy as jnp
from jax import lax
from jax.experimental import pallas as pl
from jax.experimental.pallas import tpu as pltpu

F32 = jnp.float32
BF16 = jnp.bfloat16
HI = lax.Precision.HIGHEST
EPS = 1e-6

GRID_W = 64
HY_W = 512
HY_EMB = 33
HY_BANDS = (HY_EMB - 1) // 2
HY_MOD_SHIFT = 0.05
GDN_HEADS = 4
GDN_DK = 128
GDN_DV = 128
GDN_CHUNK = 64
MLA_HEADS = 8
MLA_NOPE = 64
MLA_ROPE = 32
MLA_V = 64
MLA_Q_LORA = 768
MLA_KV_LORA = 256
ROPE_THETA = 10000.0
N_EXPERTS = 16
N_GROUPS = 4
EXPERTS_PER_GROUP = N_EXPERTS // N_GROUPS
EXPERT_FF = 512
LANES = 128
SUBLANES = 8
VMEM_LIMIT = 56 * 1024 * 1024


def _cparams(*sem):
    return pltpu.CompilerParams(dimension_semantics=sem, vmem_limit_bytes=VMEM_LIMIT)


def _silu(x):
    return x * jax.nn.sigmoid(x)


def _ada_kernel(c_ref, w_ref, b_ref, o_ref):
    a = _silu(c_ref[...])
    o_ref[...] = jnp.dot(a, w_ref[...], precision=HI, preferred_element_type=F32) + b_ref[...]


def ada_mod(cc, w, b):
    R, D = cc.shape
    N = w.shape[1]
    tn = 1536
    return pl.pallas_call(
        _ada_kernel,
        grid=(N // tn,),
        in_specs=[pl.BlockSpec((R, D), lambda j: (0, 0)),
                  pl.BlockSpec((D, tn), lambda j: (0, j)),
                  pl.BlockSpec((1, tn), lambda j: (0, j))],
        out_specs=pl.BlockSpec((R, tn), lambda j: (0, j)),
        out_shape=jax.ShapeDtypeStruct((R, N), F32),
        compiler_params=_cparams("parallel"),
        name="ada_mod",
    )(cc, w, b)


def _norm_mod_matmul(x_ref, g, sc, sh, w_ref, o_ref):
    tm = x_ref.shape[0]
    hs = []
    for rs in ((slice(0, tm // 2), slice(tm // 2, tm)) if tm >= 4 * LANES else (slice(0, tm),)):
        x = x_ref[rs, :]
        y = x * lax.rsqrt(jnp.mean(x * x, axis=-1, keepdims=True) + EPS) * g
        hs.append((rs, (y * (1.0 + sc) + sh).astype(BF16)))
    for rs, h in hs:
        o_ref[rs, :] = jnp.dot(h, w_ref[...], preferred_element_type=F32).astype(o_ref.dtype)


def _inproj_kernel(x_ref, g_ref, sc_ref, sh_ref, w_ref, o_ref):
    _norm_mod_matmul(x_ref, g_ref[...], sc_ref[...], sh_ref[...], w_ref, o_ref)


def _mod_spec(m, D):
    if m.shape[0] == 1:
        return pl.BlockSpec((None, 1, D), lambda b, i: (0, 0, 0))
    return pl.BlockSpec((None, 1, D), lambda b, i: (b, 0, 0))


def in_proj(x, g, sc, sh, w, tm):
    B, S, D = x.shape
    N = w.shape[1]
    tm = min(tm, S)
    return pl.pallas_call(
        _inproj_kernel,
        grid=(B, S // tm),
        in_specs=[pl.BlockSpec((None, tm, D), lambda b, i: (b, i, 0)),
                  pl.BlockSpec((1, D), lambda b, i: (0, 0)),
                  _mod_spec(sc, D), _mod_spec(sh, D),
                  pl.BlockSpec((D, N), lambda b, i: (0, 0))],
        out_specs=pl.BlockSpec((None, tm, N), lambda b, i: (b, i, 0)),
        out_shape=jax.ShapeDtypeStruct((B, S, N), F32),
        compiler_params=_cparams("parallel", "parallel"),
        name="in_proj",
    )(x, g, sc, sh, w)


def _inproj_head_kernel(x_ref, g_ref, sc_ref, sh_ref, w_ref, o_ref, *, n_lat):
    i = pl.program_id(1)

    @pl.when(i < n_lat)
    def _():
        _inproj_kernel(x_ref, g_ref, sc_ref, sh_ref, w_ref, o_ref)

    @pl.when(i == n_lat)
    def _():
        o_ref[...] = jnp.zeros_like(o_ref)


def _inproj_tail_kernel(x_ref, g_ref, sc_ref, sh_ref, w_ref, dst_ref, o_ref):
    del dst_ref
    _inproj_kernel(x_ref, g_ref, sc_ref, sh_ref, w_ref, o_ref)


def in_proj_joint(x, cx, g, sc, sh, csc, csh, w, tm, tmc):
    B, S, D = x.shape
    Lc = cx.shape[1]
    N = w.shape[1]
    assert S % tm == 0 and Lc % tmc == 0 and S % tmc == 0 and Lc <= tm
    n_lat = S // tm
    specs = lambda s_, h_, imap: [pl.BlockSpec((None, imap[0], D), imap[1]), pl.BlockSpec((1, D), lambda b, i: (0, 0)),
                                  _mod_spec(s_, D), _mod_spec(h_, D), pl.BlockSpec((D, N), lambda b, i: (0, 0))]
    head = pl.pallas_call(
        functools.partial(_inproj_head_kernel, n_lat=n_lat),
        grid=(B, n_lat + 1),
        in_specs=specs(sc, sh, (tm, lambda b, i: (b, jnp.minimum(i, n_lat - 1), 0))),
        out_specs=pl.BlockSpec((None, tm, N), lambda b, i: (b, i, 0)),
        out_shape=jax.ShapeDtypeStruct((B, S + Lc, N), F32),
        compiler_params=_cparams("parallel", "parallel"),
        name="in_proj",
    )(x, g, sc, sh, w)
    off = S // tmc
    return pl.pallas_call(
        _inproj_tail_kernel,
        grid=(B, Lc // tmc),
        in_specs=specs(csc, csh, (tmc, lambda b, i: (b, i, 0))) + [pl.BlockSpec(memory_space=pl.ANY)],
        out_specs=pl.BlockSpec((None, tmc, N), lambda b, i: (b, off + i, 0)),
        out_shape=jax.ShapeDtypeStruct((B, S + Lc, N), F32),
        input_output_aliases={5: 0},
        compiler_params=_cparams("parallel", "parallel"),
        name="in_proj",
    )(cx, g, csc, csh, w, head)


def _halo_specs(tm, S, C, col_block=0):
    nb8 = tm // SUBLANES
    last8 = S // SUBLANES - 1
    main = pl.BlockSpec((None, tm, C), lambda b, i: (b, i, col_block))
    prev = pl.BlockSpec((None, SUBLANES, C), lambda b, i: (b, jnp.maximum(i * nb8 - 1, 0), col_block))
    nxt = pl.BlockSpec((None, SUBLANES, C), lambda b, i: (b, jnp.minimum((i + 1) * nb8, last8), col_block))
    return main, prev, nxt


def _conv3(x, prev8, next8, w, first, last):
    tm = x.shape[0]
    row = lax.broadcasted_iota(jnp.int32, x.shape, 0)
    p_row = jnp.where(first, 0.0, prev8[SUBLANES - 1:SUBLANES, :])
    n_row = jnp.where(last, 0.0, next8[0:1, :])
    x_prev = jnp.where(row == 0, p_row, pltpu.roll(x, 1, 0))
    x_next = jnp.where(row == tm - 1, n_row, pltpu.roll(x, tm - 1, 0))
    return x_prev * w[0:1, :] + x * w[1:2, :] + x_next * w[2:3, :]


def _hy_pre_kernel(p_ref, pp_ref, pn_ref, w_ref, b_ref, x0_ref, u_ref):
    i = pl.program_id(1)
    y = _conv3(p_ref[...], pp_ref[...], pn_ref[...], w_ref[...], i == 0, i == pl.num_programs(1) - 1)
    y = y + b_ref[...]
    x0_ref[...] = y[:, :HY_W]
    u_ref[...] = y[:, HY_W:2 * HY_W] * y[:, 2 * HY_W:]


def hy_pre(p_hy, conv_w, conv_b, tm):
    B, S, C = p_hy.shape
    tm = min(tm, S)
    main, prev, nxt = _halo_specs(tm, S, C)
    o_spec = pl.BlockSpec((None, tm, HY_W), lambda b, i: (b, i, 0))
    return pl.pallas_call(
        _hy_pre_kernel,
        grid=(B, S // tm),
        in_specs=[main, prev, nxt,
                  pl.BlockSpec((3, C), lambda b, i: (0, 0)),
                  pl.BlockSpec((1, C), lambda b, i: (0, 0))],
        out_specs=[o_spec, o_spec],
        out_shape=[jax.ShapeDtypeStruct((B, S, HY_W), F32)] * 2,
        compiler_params=_cparams("parallel", "parallel"),
        name="hy_pre",
    )(p_hy, p_hy, p_hy, conv_w, conv_b)


def _hy_filter_kernel(z_ref, w1_ref, b1_ref, w2_ref, b2_ref, w3_ref, fq_ref, dc_ref, o_ref):
    fq = fq_ref[...]
    h = jnp.sin(fq * (jnp.dot(z_ref[...], w1_ref[...], precision=HI, preferred_element_type=F32) + b1_ref[...]))
    h = jnp.sin(fq * (jnp.dot(h, w2_ref[...], precision=HI, preferred_element_type=F32) + b2_ref[...]))
    h = jnp.dot(h, w3_ref[...], precision=HI, preferred_element_type=F32)
    taps = h * (jnp.exp(-z_ref[:, 0:1] * jnp.abs(dc_ref[...])) + HY_MOD_SHIFT)
    C = o_ref.shape[1]
    is_fwd = z_ref[:, LANES - 2:LANES - 1] > 0.5
    o_ref[...] = jnp.where(is_fwd, taps[:, :C], taps[:, C:]) * z_ref[:, LANES - 1:LANES]


def hy_filter(L, w1, b1, w2, b2, w3, freq, decay):
    t = jnp.linspace(0.0, 1.0, L, dtype=F32)[:, None]
    w = (2.0 * math.pi / L) * jnp.arange(L, dtype=F32)[:, None]
    f = jnp.linspace(1e-4, HY_BANDS - 1, HY_BANDS, dtype=F32)[None, :]
    z = jnp.concatenate([t, jnp.cos(f * w), -jnp.sin(f * w)], axis=-1)
    emb_pad = LANES - HY_EMB
    lag = jnp.concatenate([jnp.arange(L), jnp.zeros((1,), jnp.int32), jnp.arange(L - 1, 0, -1)])
    r = jnp.arange(2 * L)
    flags = jnp.stack([(r < L).astype(F32), (r != L).astype(F32)], axis=1)
    z = jnp.concatenate([z[lag], jnp.zeros((2 * L, emb_pad - 2), F32), flags], axis=1)
    w1 = jnp.pad(w1, ((0, emb_pad), (0, 0)))
    tl = min(2 * L, 512)
    C = w3.shape[1] // 2
    full = lambda a: pl.BlockSpec(a.shape, lambda i: (0, 0))
    b1, b2, freq, decay = b1[None, :], b2[None, :], freq[None, :], decay[None, :]
    return pl.pallas_call(
        _hy_filter_kernel,
        grid=(2 * L // tl,),
        in_specs=[pl.BlockSpec((tl, LANES), lambda i: (i, 0)),
                  full(w1), full(b1), full(w2), full(b2), full(w3), full(freq), full(decay)],
        out_specs=pl.BlockSpec((tl, C), lambda i: (i, 0)),
        out_shape=jax.ShapeDtypeStruct((2 * L, C), F32),
        compiler_params=_cparams("parallel"),
        name="hy_filter",
    )(z, w1, b1, w2, b2, w3, freq, decay)


def _dft_tables(N1, N2):
    N = N1 * N2
    two_pi = 2.0 * math.pi

    def cs(num, den):
        ang = (two_pi / den) * (num % den).astype(F32)
        return jnp.cos(ang), jnp.sin(ang)

    a1 = jnp.arange(N1, dtype=jnp.int32)
    a2 = jnp.arange(N2, dtype=jnp.int32)
    c1, s1 = cs(a1[:, None] * a1[None, :], N1)
    c2, s2 = cs(a2[:, None] * a2[None, :], N2)
    ct, st = cs(a2[:, None] * a1[None, :], N)

    def stack(re, im):
        return jnp.concatenate([jnp.concatenate([re, -im], axis=-1),
                                jnp.concatenate([im, re], axis=-1)], axis=-2)

    tr = ct[:, :, None] * c1[None] - st[:, :, None] * s1[None]
    ti = -(ct[:, :, None] * s1[None] + st[:, :, None] * c1[None])
    h = N1 // 2
    m1_data = stack(tr[:, :, :h], ti[:, :, :h])
    m1_real = jnp.concatenate([tr, ti], axis=-2)
    m2 = stack(c2, -s2)
    ctk, stk = ct.T, st.T
    gr = ctk[:, :, None] * c2.T[None] - stk[:, :, None] * s2.T[None]
    gi = ctk[:, :, None] * s2.T[None] + stk[:, :, None] * c2.T[None]
    m2inv = stack(gr, gi)
    er, ei = c1.T[:h] / N, s1.T[:h] / N
    m3 = stack(er, ei)
    return [_hi_lo_rows(m) for m in (m1_data, m1_real, m2, m2inv, m3)]


def _hi_lo_rows(m):
    hi = m.astype(BF16)
    lo = (m - hi.astype(F32)).astype(BF16)
    return jnp.concatenate([hi, lo], axis=-2)


def _dot3(m2, x):
    M = m2.shape[0] // 2
    x_hi = x.astype(BF16)
    x_lo = (x - x_hi.astype(F32)).astype(BF16)
    a = jnp.dot(m2, x_hi, preferred_element_type=F32)
    return a[:M] + a[M:] + jnp.dot(m2[:M], x_lo, preferred_element_type=F32)


def _hy_pass_a_kernel(u_ref, m_ref, o_ref):
    n1 = o_ref.shape[1]
    for j in range(SUBLANES):
        xj = jnp.concatenate([u_ref[0, :, j, :], u_ref[1, :, j, :]], axis=0)
        a = _dot3(m_ref[j], xj)
        o_ref[0, :, j, :] = a[:n1]
        o_ref[1, :, j, :] = a[n1:]


def _hy_pass_a(u4, m1, n_pairs):
    _, h, N2, C = u4.shape
    N1 = 2 * h
    return pl.pallas_call(
        _hy_pass_a_kernel,
        grid=(n_pairs, N2 // SUBLANES),
        in_specs=[pl.BlockSpec((2, h, SUBLANES, C), lambda p, j: (p, 0, j, 0)),
                  pl.BlockSpec((SUBLANES, 4 * N1, N1), lambda p, j: (j, 0, 0))],
        out_specs=pl.BlockSpec((2, N1, SUBLANES, C), lambda p, j: (0, 0, j, p)),
        out_shape=jax.ShapeDtypeStruct((2, N1, N2, n_pairs * C), F32),
        compiler_params=_cparams("parallel", "parallel"),
        name="hy_pass_a",
    )(u4, m1)


def _hy_spec_kernel(a_ref, m2_ref, o_ref):
    n2 = a_ref.shape[2]
    for j in range(SUBLANES):
        a = jnp.concatenate([a_ref[0, j], a_ref[1, j]], axis=0)
        x = _dot3(m2_ref[...], a)
        o_ref[0, j] = x[:n2]
        o_ref[1, j] = x[n2:]


def _hy_spectrum(a, m2):
    _, N1, N2, C = a.shape
    spec = pl.BlockSpec((2, SUBLANES, N2, C), lambda k: (0, k, 0, 0))
    return pl.pallas_call(
        _hy_spec_kernel,
        grid=(N1 // SUBLANES,),
        in_specs=[spec, pl.BlockSpec((4 * N2, 2 * N2), lambda k: (0, 0))],
        out_specs=spec,
        out_shape=jax.ShapeDtypeStruct(a.shape, F32),
        compiler_params=_cparams("parallel"),
        name="hy_spectrum",
    )(a, m2)


def _hy_pass_b_kernel(a_ref, k_ref, m2_ref, mi_ref, o_ref):
    n2 = a_ref.shape[2]
    for j in range(SUBLANES):
        a = jnp.concatenate([a_ref[0, j], a_ref[1, j]], axis=0)
        x = _dot3(m2_ref[...], a)
        xr, xi = x[:n2], x[n2:]
        kr, ki = k_ref[0, j], k_ref[1, j]
        y = jnp.concatenate([xr * kr - xi * ki, xr * ki + xi * kr], axis=0)
        b = _dot3(mi_ref[j], y)
        o_ref[0, :, j, :] = b[:n2]
        o_ref[1, :, j, :] = b[n2:]


def _hy_pass_b(a, kf, m2, m2inv, n_pairs):
    _, N1, N2, PC = a.shape
    C = PC // n_pairs
    return pl.pallas_call(
        _hy_pass_b_kernel,
        grid=(n_pairs, N1 // SUBLANES),
        in_specs=[pl.BlockSpec((2, SUBLANES, N2, C), lambda p, k: (0, k, 0, p)),
                  pl.BlockSpec((2, SUBLANES, N2, C), lambda p, k: (0, k, 0, 0)),
                  pl.BlockSpec((4 * N2, 2 * N2), lambda p, k: (0, 0)),
                  pl.BlockSpec((SUBLANES, 4 * N2, 2 * N2), lambda p, k: (k, 0, 0))],
        out_specs=pl.BlockSpec((2, N2, SUBLANES, C), lambda p, k: (0, 0, k, p)),
        out_shape=jax.ShapeDtypeStruct((2, N2, N1, PC), F32),
        compiler_params=_cparams("parallel", "parallel"),
        name="hy_pass_b",
    )(a, kf, m2, m2inv)


def _hy_pass_c_kernel(b_ref, u_ref, x0_ref, bias_ref, m3_ref, o_ref):
    h = o_ref.shape[1]
    bias = bias_ref[...]
    for j in range(SUBLANES):
        bb = jnp.concatenate([b_ref[0, j], b_ref[1, j]], axis=0)
        y = _dot3(m3_ref[...], bb)
        for r in range(2):
            o_ref[r, :, j, :] = x0_ref[r, :, j, :] * (y[r * h:(r + 1) * h] + bias * u_ref[r, :, j, :])


def _hy_pass_c(bq, u4, x04, bias, m3, n_pairs):
    _, N2, N1, PC = bq.shape
    C = PC // n_pairs
    h = N1 // 2
    io = pl.BlockSpec((2, h, SUBLANES, C), lambda p, j: (p, 0, j, 0))
    return pl.pallas_call(
        _hy_pass_c_kernel,
        grid=(n_pairs, N2 // SUBLANES),
        in_specs=[pl.BlockSpec((2, SUBLANES, N1, C), lambda p, j: (0, j, 0, p)),
                  io, io,
                  pl.BlockSpec((1, C), lambda p, j: (0, 0)),
                  pl.BlockSpec((2 * N1, 2 * N1), lambda p, j: (0, 0))],
        out_specs=io,
        out_shape=jax.ShapeDtypeStruct(u4.shape, F32),
        compiler_params=_cparams("parallel", "parallel"),
        name="hy_pass_c",
    )(bq, u4, x04, bias, m3)


def hyena_long_conv(x0, u, kbuf, bias):
    B, L, C = u.shape
    N2 = min(128, L // 32)
    N1 = 2 * L // N2
    n_pairs = B // 2
    m1_data, m1_real, m2, m2inv, m3 = _dft_tables(N1, N2)
    kf = _hy_spectrum(_hy_pass_a(kbuf.reshape(2, N1 // 2, N2, C), m1_real, 1), m2)
    u4 = u.reshape(B, N1 // 2, N2, C)
    a = _hy_pass_a(u4, m1_data, n_pairs)
    bq = _hy_pass_b(a, kf, m2, m2inv, n_pairs)
    y = _hy_pass_c(bq, u4, x0.reshape(u4.shape), bias[None, :], m3, n_pairs)
    return y.reshape(B, L, C)


GDN_QKV = GDN_HEADS * (2 * GDN_DK + GDN_DV)


def _gdn_pre_kernel(p_ref, pp_ref, pn_ref, w_ref, alog_ref, dtb_ref, q_ref, k_ref, v_ref, gb_ref, *, n_first):
    i = pl.program_id(1)
    C = GDN_QKV
    first = jnp.logical_or(i == 0, i == n_first)
    last = jnp.logical_or(i == n_first - 1, i == pl.num_programs(1) - 1)
    y = _conv3(p_ref[:, :C], pp_ref[:, :C], pn_ref[:, :C], w_ref[...], first, last)
    y = _silu(y)
    nk = GDN_HEADS * GDN_DK
    for h in range(GDN_HEADS):
        sl = slice(h * GDN_DK, (h + 1) * GDN_DK)
        qh = y[:, sl]
        kh = y[:, nk + h * GDN_DK:nk + (h + 1) * GDN_DK]
        q_ref[:, sl] = qh * (lax.rsqrt(jnp.sum(qh * qh, axis=-1, keepdims=True) + EPS) * (GDN_DK ** -0.5))
        k_ref[:, sl] = kh * lax.rsqrt(jnp.sum(kh * kh, axis=-1, keepdims=True) + EPS)
    v_ref[...] = y[:, 2 * nk:]
    s = p_ref[:, C:]
    lane = lax.broadcasted_iota(jnp.int32, s.shape, 1)
    xa = s + dtb_ref[...]
    softplus = jnp.maximum(xa, 0.0) + jnp.log1p(jnp.exp(-jnp.abs(xa)))
    g = -jnp.exp(alog_ref[...]) * softplus
    gb_ref[...] = jnp.where(lane < 2 * GDN_HEADS, g, jnp.where(lane < 4 * GDN_HEADS, jax.nn.sigmoid(s), 0.0))


def gdn_pre(p_gdn, conv_w, a_log, dt_bias, tm, first_rows):
    B, S, C = p_gdn.shape
    assert first_rows % tm == 0 and S % tm == 0
    main, prev, nxt = _halo_specs(tm, S, C)
    pad = LANES - 2 * GDN_HEADS
    alog = jnp.pad(a_log.reshape(1, -1), ((0, 0), (0, pad)))
    dtb = jnp.pad(dt_bias.reshape(1, -1), ((0, 0), (0, pad)))
    nv = GDN_HEADS * GDN_DV
    o_spec = pl.BlockSpec((None, tm, nv), lambda b, i: (b, i, 0))
    return pl.pallas_call(
        functools.partial(_gdn_pre_kernel, n_first=first_rows // tm),
        grid=(B, S // tm),
        in_specs=[main, prev, nxt,
                  pl.BlockSpec((3, GDN_QKV), lambda b, i: (0, 0)),
                  pl.BlockSpec((1, LANES), lambda b, i: (0, 0)),
                  pl.BlockSpec((1, LANES), lambda b, i: (0, 0))],
        out_specs=[o_spec, o_spec, o_spec, pl.BlockSpec((None, tm, LANES), lambda b, i: (b, i, 0))],
        out_shape=[jax.ShapeDtypeStruct((B, S, nv), F32)] * 3 + [jax.ShapeDtypeStruct((B, S, LANES), F32)],
        compiler_params=_cparams("parallel", "parallel"),
        name="gdn_pre",
    )(p_gdn, p_gdn, p_gdn, conv_w, alog, dtb)


def _bdot(a, b):
    return jnp.dot(a.astype(BF16), b.astype(BF16), preferred_element_type=F32)


def _bdot_nt(a, b):
    return lax.dot_general(a.astype(BF16), b.astype(BF16), (((1,), (1,)), ((), ())), preferred_element_type=F32)


def _bdot_tn(a, b):
    return lax.dot_general(a.astype(BF16), b.astype(BF16), (((0,), (0,)), ((), ())), preferred_element_type=F32)


GDN_STEP_CHUNKS = 4


def _gdn_scan_kernel(qf_ref, kf_ref, vf_ref, gf_ref, qb_ref, kb_ref, vb_ref, gb_ref, of_ref, ob_ref, s_ref):
    C = GDN_CHUNK
    H = GDN_HEADS
    G = qf_ref.shape[0] // C
    in_refs = ((qf_ref, kf_ref, vf_ref, gf_ref), (qb_ref, kb_ref, vb_ref, gb_ref))
    o_refs = (of_ref, ob_ref)

    @pl.when(pl.program_id(1) == 0)
    def _():
        s_ref[...] = jnp.zeros_like(s_ref)

    ri = lax.broadcasted_iota(jnp.int32, (C, C), 0)
    ci = lax.broadcasted_iota(jnp.int32, (C, C), 1)
    eye = (ri == ci).astype(F32)
    incl = ((ci <= ri), (ci >= ri))
    strict = ((ci < ri), (ci > ri))
    rows = lambda g: slice(g * C, (g + 1) * C)
    cols = lambda h: slice(h * GDN_DK, (h + 1) * GDN_DK)
    chains = [(d, g, h) for d in range(2) for g in range(G) for h in range(H)]

    gbv = {(d, g): in_refs[d][3][rows(g), :] for d in range(2) for g in range(G)}
    gc_all = {dg: jnp.dot(incl[dg[0]].astype(F32), gbv[dg], precision=HI, preferred_element_type=F32)
              for dg in gbv}
    gc_t = {dg: gc_all[dg].T for dg in gbv}

    kk, gamma, rhs, qe, kdec, gend = {}, {}, {}, {}, {}, {}
    for ch in chains:
        d, g, h = ch
        c = d * H + h
        q, k, v = (in_refs[d][n][rows(g), cols(h)] for n in range(3))
        end = C - 1 if d == 0 else 0
        beta = gbv[d, g][:, 2 * H + c:2 * H + c + 1]
        gc_c = gc_all[d, g][:, c:c + 1]
        gc_r = gc_t[d, g][c:c + 1, :]
        g_tot = gc_all[d, g][end:end + 1, c:c + 1]
        gamma[ch] = jnp.where(incl[d], jnp.exp(jnp.where(incl[d], gc_c - gc_r, 0.0)), 0.0)
        e_c = jnp.exp(gc_c)
        kb = k * beta
        kk[ch] = _bdot_nt(jnp.concatenate([kb, q], axis=0), k)
        rhs[ch] = jnp.concatenate([v * beta, kb * e_c], axis=1).astype(BF16)
        qe[ch] = q * e_c
        kdec[ch] = (k * jnp.exp(g_tot - gc_c)).astype(BF16)
        gend[ch] = jnp.exp(g_tot)
    m = {ch: jnp.where(strict[ch[0]], kk[ch][:C] * gamma[ch], 0.0) for ch in chains}
    a_intra = {ch: (kk[ch][C:] * gamma[ch]).astype(BF16) for ch in chains}
    t = {ch: eye - m[ch] for ch in chains}
    pw = m
    for _ in range(5):
        pw = {ch: _bdot(pw[ch], pw[ch]) for ch in chains}
        t = {ch: t[ch] + _bdot(t[ch], pw[ch]) for ch in chains}
    uw = {ch: jnp.dot(t[ch].astype(BF16), rhs[ch], preferred_element_type=F32) for ch in chains}
    wq = {ch: jnp.concatenate([uw[ch][:, GDN_DV:], qe[ch]], axis=0).astype(BF16) for ch in chains}

    heads = [(d, h) for d in range(2) for h in range(H)]
    s = {dh: s_ref[dh[0] * H + dh[1]] for dh in heads}
    for j in range(G):
        at = lambda dh: (dh[0], j if dh[0] == 0 else G - 1 - j, dh[1])
        ws = {dh: jnp.dot(wq[at(dh)], s[dh].astype(BF16), preferred_element_type=F32) for dh in heads}
        v_new = {dh: uw[at(dh)][:, :GDN_DV] - ws[dh][:C] for dh in heads}
        for dh in heads:
            d, g, h = at(dh)
            o_refs[d][rows(g), cols(h)] = ws[dh][C:] + jnp.dot(a_intra[at(dh)], v_new[dh].astype(BF16),
                                                               preferred_element_type=F32)
        s = {dh: s[dh] * gend[at(dh)] + _bdot_tn(kdec[at(dh)], v_new[dh]) for dh in heads}
    for dh in heads:
        s_ref[dh[0] * H + dh[1]] = s[dh]


def gdn_scan(q, k, v, gb, n_ctx_chunks):
    B, Lt, NV = q.shape
    R = GDN_CHUNK * GDN_STEP_CHUNKS
    n = Lt // R
    n_ctx = n_ctx_chunks // GDN_STEP_CHUNKS
    n_lat = n - n_ctx
    assert n * R == Lt and n_ctx * GDN_STEP_CHUNKS == n_ctx_chunks

    def fwd(b, i):
        return (b, jnp.where(i < n_ctx, n_lat + i, i - n_ctx), 0)

    def bwd(b, i):
        return (b, n - 1 - i, 0)

    def specs(imap):
        return [pl.BlockSpec((None, R, NV), imap)] * 3 + [pl.BlockSpec((None, R, LANES), imap)]

    return pl.pallas_call(
        _gdn_scan_kernel,
        grid=(B, n),
        in_specs=specs(fwd) + specs(bwd),
        out_specs=[pl.BlockSpec((None, R, NV), fwd), pl.BlockSpec((None, R, NV), bwd)],
        out_shape=[jax.ShapeDtypeStruct((B, Lt, NV), F32)] * 2,
        scratch_shapes=[pltpu.VMEM((2 * GDN_HEADS, GDN_DK, GDN_DV), F32)],
        compiler_params=_cparams("parallel", "arbitrary"),
        name="gdn_scan",
    )(q, k, v, gb, q, k, v, gb)


MLA_QK = MLA_NOPE + MLA_ROPE
MLA_KVIN = MLA_KV_LORA + LANES


def _rope_partner(w_rope):
    nf = MLA_ROPE // 4
    parts = []
    for half in range(2):
        a = w_rope[..., half * 2 * nf:half * 2 * nf + nf]
        b = w_rope[..., half * 2 * nf + nf:(half + 1) * 2 * nf]
        parts += [-b, a]
    return jnp.concatenate(parts, axis=-1)


def _head_pad(nope, rope):
    pad = jnp.zeros(nope.shape[:-1] + (LANES - MLA_QK,), nope.dtype)
    out = jnp.concatenate([nope, rope, pad], axis=-1)
    return out.reshape(out.shape[:-2] + (MLA_HEADS * LANES,))


def mla_weights(w_uq, w_ukv):
    wq = w_uq.reshape(MLA_Q_LORA, MLA_HEADS, MLA_QK)
    qn, qr = wq[..., :MLA_NOPE], wq[..., MLA_NOPE:]
    wq2 = jnp.concatenate([_head_pad(qn, qr), _head_pad(jnp.zeros_like(qn), _rope_partner(qr))], axis=-1)
    wkv = w_ukv.reshape(MLA_KV_LORA, MLA_HEADS, MLA_NOPE + MLA_V)
    kn, vv = wkv[..., :MLA_NOPE], wkv[..., MLA_NOPE:]
    eye = jnp.broadcast_to(jnp.eye(MLA_ROPE, dtype=F32)[:, None, :], (MLA_ROPE, MLA_HEADS, MLA_ROPE))
    z_kn = jnp.zeros((MLA_ROPE, MLA_HEADS, MLA_NOPE), F32)
    z_rope = jnp.zeros((MLA_KV_LORA, MLA_HEADS, MLA_ROPE), F32)
    top = jnp.concatenate([_head_pad(kn, z_rope), _head_pad(jnp.zeros_like(kn), z_rope), _head_pad(vv, z_rope)],
                          axis=-1)
    mid = jnp.concatenate([_head_pad(z_kn, eye), _head_pad(z_kn, _rope_partner(eye)),
                           jnp.zeros((MLA_ROPE, MLA_HEADS * LANES), F32)], axis=-1)
    bot = jnp.zeros((MLA_KVIN - MLA_KV_LORA - MLA_ROPE, top.shape[1]), F32)
    return wq2.astype(BF16), jnp.concatenate([top, mid, bot], axis=0).astype(BF16)


def rope_tables(row, col):
    nf = MLA_ROPE // 4
    inv_freq = ROPE_THETA ** (-jnp.arange(nf, dtype=F32) / nf)
    ang = jnp.concatenate([row.astype(F32)[:, None] * inv_freq[None, :]] * 2
                          + [col.astype(F32)[:, None] * inv_freq[None, :]] * 2, axis=-1)
    n = ang.shape[0]
    pad = jnp.zeros((n, LANES - MLA_QK), F32)
    cos = jnp.concatenate([jnp.ones((n, MLA_NOPE), F32), jnp.cos(ang), pad], axis=-1)
    sin = jnp.concatenate([jnp.zeros((n, MLA_NOPE), F32), jnp.sin(ang), pad], axis=-1)
    return cos, sin


def _mla_proj_kernel(p_ref, gq_ref, gkv_ref, wq_ref, wk_ref, cos_ref, sin_ref, q_ref, k_ref, v_ref):
    HL = MLA_HEADS * LANES
    cos = jnp.concatenate([cos_ref[...]] * MLA_HEADS, axis=1)
    sin = jnp.concatenate([sin_ref[...]] * MLA_HEADS, axis=1)
    cq = p_ref[:, :MLA_Q_LORA]
    cqn = (cq * lax.rsqrt(jnp.mean(cq * cq, axis=-1, keepdims=True) + EPS) * gq_ref[...]).astype(BF16)
    ck = p_ref[:, MLA_Q_LORA:]
    lane = lax.broadcasted_iota(jnp.int32, ck.shape, 1)
    is_kv = lane < MLA_KV_LORA
    ms = jnp.sum(jnp.where(is_kv, ck * ck, 0.0), axis=-1, keepdims=True) * (1.0 / MLA_KV_LORA)
    ckn = jnp.where(is_kv, ck * lax.rsqrt(ms + EPS) * gkv_ref[...], ck).astype(BF16)
    qq = jnp.dot(cqn, wq_ref[...], preferred_element_type=F32)
    kk = jnp.dot(ckn, wk_ref[...], preferred_element_type=F32)
    q_ref[...] = ((qq[:, :HL] * cos + qq[:, HL:] * sin) * (MLA_QK ** -0.5 * math.log2(math.e))).astype(q_ref.dtype)
    k_ref[...] = (kk[:, :HL] * cos + kk[:, HL:2 * HL] * sin).astype(k_ref.dtype)
    vv = kk[:, 2 * HL:]
    vlane = lax.broadcasted_iota(jnp.int32, vv.shape, 1)
    v_ref[...] = jnp.where(vlane % LANES == MLA_V, 1.0, vv).astype(v_ref.dtype)


def mla_proj(p_mla, gq, gkv, wq2, wk2, cos, sin, tm):
    B, S, C = p_mla.shape
    tm = min(tm, S)
    HL = MLA_HEADS * LANES
    gkv = jnp.pad(gkv, ((0, 0), (0, MLA_KVIN - MLA_KV_LORA)))
    row = lambda n: pl.BlockSpec((None, tm, n), lambda b, i: (b, i, 0))
    full = lambda a: pl.BlockSpec(a.shape, lambda b, i: (0, 0))
    tab = pl.BlockSpec((tm, LANES), lambda b, i: (i, 0))
    return pl.pallas_call(
        _mla_proj_kernel,
        grid=(B, S // tm),
        in_specs=[row(C), full(gq), full(gkv), full(wq2), full(wk2), tab, tab],
        out_specs=[row(HL)] * 3,
        out_shape=[jax.ShapeDtypeStruct((B, S, HL), BF16)] * 3,
        compiler_params=_cparams("parallel", "parallel"),
        name="mla_proj",
    )(p_mla, gq, gkv, wq2, wk2, cos, sin)


ATT_SLAB = 32


def _mla_attn_kernel(q_ref, k_ref, v_ref, o_ref, *, tk):
    tq = q_ref.shape[0]
    Tk = k_ref.shape[0]
    n_full, rem = Tk // tk, Tk % tk
    heads = (slice(0, LANES), slice(LANES, 2 * LANES))
    qs = [q_ref[:, hs] for hs in heads]

    def step(carry, start, size):
        ss = [lax.dot_general(qs[h], k_ref[pl.ds(start, size), heads[h]], (((1,), (1,)), ((), ())),
                              preferred_element_type=F32) for h in range(2)]
        out = []
        for h in range(2):
            m, acc = carry[h]
            m_new = jnp.maximum(m, jnp.max(ss[h], axis=-1, keepdims=True))
            p = jnp.concatenate([jnp.exp2(ss[h][r:r + ATT_SLAB] - m_new[r:r + ATT_SLAB]).astype(BF16)
                                 for r in range(0, tq, ATT_SLAB)], axis=0)
            acc = acc * jnp.exp2(m - m_new) + jnp.dot(p, v_ref[pl.ds(start, size), heads[h]],
                                                      preferred_element_type=F32)
            out.append((m_new, acc))
        return tuple(out)

    carry = tuple((jnp.full((tq, 1), -jnp.inf, F32), jnp.zeros((tq, LANES), F32)) for _ in range(2))
    if n_full:
        carry = lax.fori_loop(0, n_full, lambda c, cr: step(cr, pl.multiple_of(c * tk, tk), tk), carry,
                              unroll=4 if n_full % 4 == 0 else 1)
    if rem:
        carry = step(carry, n_full * tk, rem)
    o0, o1 = [acc / acc[:, MLA_V:MLA_V + 1] for _, acc in carry]
    lane = lax.broadcasted_iota(jnp.int32, (tq, LANES), 1)
    o_ref[...] = jnp.where(lane < MLA_V, o0, pltpu.roll(o1, MLA_V, 1)).astype(o_ref.dtype)


def mla_attn(q, k, v, q_rows, k_rows, tq, tk):
    B = q.shape[0]
    (q0, S), (k0, Tk) = q_rows, k_rows
    tq = min(tq, S)
    assert q0 % tq == 0 and k0 % Tk == 0
    qb, kb = q0 // tq, k0 // Tk
    return pl.pallas_call(
        functools.partial(_mla_attn_kernel, tk=tk),
        grid=(B, MLA_HEADS // 2, S // tq),
        in_specs=[pl.BlockSpec((None, tq, 2 * LANES), lambda b, h, i: (b, qb + i, h)),
                  pl.BlockSpec((None, Tk, 2 * LANES), lambda b, h, i: (b, kb, h)),
                  pl.BlockSpec((None, Tk, 2 * LANES), lambda b, h, i: (b, kb, h))],
        out_specs=pl.BlockSpec((None, tq, LANES), lambda b, h, i: (b, i, h)),
        out_shape=jax.ShapeDtypeStruct((B, S, MLA_HEADS * MLA_V), BF16),
        compiler_params=_cparams("parallel", "parallel", "arbitrary"),
        name="mla_attn",
    )(q, k, v)


def _route(logits, rb):
    lane = lax.broadcasted_iota(jnp.int32, logits.shape, 1)
    neg = -jnp.inf
    scores = jax.nn.sigmoid(logits)
    sel = scores + rb

    def top2(masked):
        m1 = jnp.max(masked, axis=-1, keepdims=True)
        i1 = jnp.min(jnp.where(masked == m1, lane, LANES), axis=-1, keepdims=True)
        rest = jnp.where(lane == i1, neg, masked)
        m2 = jnp.max(rest, axis=-1, keepdims=True)
        i2 = jnp.min(jnp.where(rest == m2, lane, LANES), axis=-1, keepdims=True)
        return m1, i1, m2, i2

    best = None
    for gi in range(N_GROUPS):
        in_g = jnp.logical_and(lane >= gi * EXPERTS_PER_GROUP, lane < (gi + 1) * EXPERTS_PER_GROUP)
        m1, _, m2, _ = top2(jnp.where(in_g, sel, neg))
        gs = m1 + m2
        if best is None:
            best, grp = gs, jnp.zeros_like(gs, dtype=jnp.int32)
        else:
            better = gs > best
            grp = jnp.where(better, gi, grp)
            best = jnp.where(better, gs, best)
    lo = grp * EXPERTS_PER_GROUP
    in_grp = jnp.logical_and(lane >= lo, lane < lo + EXPERTS_PER_GROUP)
    _, i1, _, i2 = top2(jnp.where(in_grp, sel, neg))
    picked = jnp.where(jnp.logical_or(lane == i1, lane == i2), scores, 0.0)
    gate = picked / jnp.sum(picked, axis=-1, keepdims=True)
    return jnp.where(lane == LANES - 1, grp.astype(F32), gate)


MERGE_SPLIT = 2


def _merge_kernel(hyv_ref, of_ref, ob_ref, pg_ref, at_ref, x_ref, gt1_ref, sc2_ref, sh2_ref, gng_ref, n2g_ref,
                  whb_ref, wgd_ref, wml_ref, wo_ref, rw_ref, rb_ref, xo_ref, h2_ref, gate_ref):
    nv = GDN_HEADS * GDN_DV
    tm, D = x_ref.shape
    slabs = [slice(r, r + tm // MERGE_SPLIT) for r in range(0, tm, tm // MERGE_SPLIT)]

    def gdn_gate(rs):
        o = of_ref[rs, :] + ob_ref[rs, :]
        ys = []
        for h in range(GDN_HEADS):
            sl = slice(h * GDN_DV, (h + 1) * GDN_DV)
            oh = o[:, sl]
            on = oh * lax.rsqrt(jnp.mean(oh * oh, axis=-1, keepdims=True) + EPS) * gng_ref[...]
            ys.append(on * _silu(pg_ref[rs, sl]))
        return jnp.concatenate(ys, axis=1).astype(BF16)

    y_in = [gdn_gate(rs) for rs in slabs]
    branches = [(jnp.dot(hyv_ref[rs, :].astype(BF16), whb_ref[...], preferred_element_type=F32),
                 jnp.dot(y, wgd_ref[...], preferred_element_type=F32),
                 jnp.dot(at_ref[rs, :], wml_ref[...], preferred_element_type=F32)) for rs, y in zip(slabs, y_in)]
    merged = [(jax.nn.sigmoid(pg_ref[rs, nv:nv + D]) * y_hy
               + jax.nn.sigmoid(pg_ref[rs, nv + D:nv + 2 * D]) * y_gdn
               + jax.nn.sigmoid(pg_ref[rs, nv + 2 * D:]) * y_mla).astype(BF16)
              for rs, (y_hy, y_gdn, y_mla) in zip(slabs, branches)]
    mix = [jnp.dot(m, wo_ref[...], preferred_element_type=F32) for m in merged]
    h2s = []
    for rs, mx in zip(slabs, mix):
        xn = x_ref[rs, :] + gt1_ref[...] * mx
        xo_ref[rs, :] = xn
        y2 = xn * lax.rsqrt(jnp.mean(xn * xn, axis=-1, keepdims=True) + EPS) * n2g_ref[...]
        h2 = y2 * (1.0 + sc2_ref[...]) + sh2_ref[...]
        h2_ref[rs, :] = h2.astype(h2_ref.dtype)
        h2s.append(h2)
    logits = []
    for h2 in h2s:
        h_hi = h2.astype(BF16)
        h_lo = (h2 - h_hi.astype(F32)).astype(BF16)
        a = jnp.dot(h_hi, rw_ref[...], preferred_element_type=F32)
        logits.append(a[:, :LANES] + a[:, LANES:] + jnp.dot(h_lo, rw_ref[:, :LANES], preferred_element_type=F32))
    for rs, lg in zip(slabs, logits):
        gate_ref[rs, :] = _route(lg, rb_ref[...])


def merge_out(hyv, o_f, o_b, pg, attn, x, gt1, sc2, sh2, gdn_norm_g, norm2_g, w_hy, w_gdn, w_mla, w_out,
              router_w, router_b, tm, o_row0):
    B, S, D = x.shape
    tm = min(tm, S)
    assert o_row0 % tm == 0
    ob0 = o_row0 // tm
    row = lambda n: pl.BlockSpec((None, tm, n), lambda b, i: (b, i, 0))
    full = lambda a: pl.BlockSpec(a.shape, lambda b, i: (0, 0))
    nv = GDN_HEADS * GDN_DV
    o_spec = pl.BlockSpec((None, tm, nv), lambda b, i: (b, ob0 + i, 0))
    return pl.pallas_call(
        _merge_kernel,
        grid=(B, S // tm),
        in_specs=[row(HY_W), o_spec, o_spec, row(pg.shape[2]), row(MLA_HEADS * MLA_V), row(D),
                  _mod_spec(gt1, D), _mod_spec(sc2, D), _mod_spec(sh2, D),
                  full(gdn_norm_g), full(norm2_g), full(w_hy), full(w_gdn), full(w_mla), full(w_out),
                  full(router_w), full(router_b)],
        out_specs=[row(D), row(D), row(LANES)],
        out_shape=[jax.ShapeDtypeStruct((B, S, D), F32), jax.ShapeDtypeStruct((B, S, D), BF16),
                   jax.ShapeDtypeStruct((B, S, LANES), F32)],
        compiler_params=_cparams("parallel", "parallel"),
        name="merge_out",
    )(hyv, o_f, o_b, pg, attn, x, gt1, sc2, sh2, gdn_norm_g, norm2_g, w_hy, w_gdn, w_mla, w_out,
      router_w, router_b)


MOE_BLK = 128
MOE_STEP = 2


def _moe_kernel(h_ref, gate_ref, w1_ref, w3_ref, w2_ref, x_ref, gt2_ref, fg_ref, o_ref,
                xs_ref, gs_ref, y_ref, slot_ref, seg_ref, *, final_norm):
    e = pl.program_id(1)
    tm, D = h_ref.shape
    A = xs_ref.shape[0]

    @pl.when(e == 0)
    def _():
        gate = gate_ref[...]
        lane = lax.broadcasted_iota(jnp.int32, gate.shape, 1)
        grp = gate[:, LANES - 1:LANES]
        member = jnp.where(jnp.logical_and(lane.astype(F32) == grp, lane < N_GROUPS), 1.0, 0.0)
        ri = lax.broadcasted_iota(jnp.int32, (tm, tm), 0)
        ci = lax.broadcasted_iota(jnp.int32, (tm, tm), 1)
        before = jnp.where(ci < ri, 1.0, 0.0).astype(BF16)
        rank = jnp.dot(before, member.astype(BF16), preferred_element_type=F32)
        cnt = jnp.sum(member, axis=0, keepdims=True)
        blocks = jnp.floor((cnt + (MOE_BLK - 1)) * (1.0 / MOE_BLK))
        padded = blocks * MOE_BLK
        l1 = lane[0:1, :]
        p0, p1, p2 = padded[:, 0:1], padded[:, 1:2], padded[:, 2:3]
        start = jnp.where(l1 == 0, 0.0, jnp.where(l1 == 1, p0, jnp.where(l1 == 2, p0 + p1, p0 + p1 + p2)))
        slot = jnp.sum(member * (start + rank), axis=-1, keepdims=True)
        slot_b = jnp.broadcast_to(slot, (tm, LANES))
        slot_ref[...] = slot_b
        start_i, blocks_i = start.astype(jnp.int32), blocks.astype(jnp.int32)
        for g in range(N_GROUPS):
            seg_ref[g] = start_i[0, g]
            seg_ref[N_GROUPS + g] = blocks_i[0, g]
        slot_row = slot_b.T[0:1, :]
        pick = jnp.where(lax.broadcasted_iota(jnp.int32, (A, tm), 0).astype(F32) == slot_row, 1.0, 0.0).astype(BF16)
        g_hi = gate.astype(BF16)
        g_lo = (gate - g_hi.astype(F32)).astype(BF16)
        got = jnp.dot(pick, jnp.concatenate([h_ref[...], g_hi, g_lo], axis=1), preferred_element_type=F32)
        xs_ref[...] = got[:, :D].astype(BF16)
        gs_ref[...] = got[:, D:D + LANES] + got[:, D + LANES:]
        y_ref[...] = jnp.zeros_like(y_ref)

    g = e // (EXPERTS_PER_GROUP // MOE_STEP)
    first = seg_ref[g]
    lane_b = lax.broadcasted_iota(jnp.int32, (MOE_BLK, LANES), 1)

    def block(b, carry):
        rows = pl.ds(pl.multiple_of(first + b * MOE_BLK, MOE_BLK), MOE_BLK)
        xb = xs_ref[rows, :]
        gsb = gs_ref[rows, :]
        up = [(jnp.dot(xb, w1_ref[j], preferred_element_type=F32), jnp.dot(xb, w3_ref[j], preferred_element_type=F32))
              for j in range(MOE_STEP)]
        he = [(_silu(a) * b3).astype(BF16) for a, b3 in up]
        down = [jnp.dot(he[j], w2_ref[j], preferred_element_type=F32) for j in range(MOE_STEP)]
        acc = y_ref[rows, :]
        for j in range(MOE_STEP):
            ge = jnp.sum(jnp.where(lane_b == e * MOE_STEP + j, gsb, 0.0), axis=-1, keepdims=True)
            acc = acc + ge * down[j]
        y_ref[rows, :] = acc
        return carry

    lax.fori_loop(0, seg_ref[N_GROUPS + g], block, 0)

    @pl.when(e == pl.num_programs(1) - 1)
    def _():
        back = jnp.where(lax.broadcasted_iota(jnp.int32, (tm, A), 1).astype(F32) == slot_ref[:, 0:1], 1.0, 0.0)
        moe_out = jnp.dot(back.astype(BF16), y_ref[...].astype(BF16), preferred_element_type=F32)
        xn = x_ref[...] + gt2_ref[...] * moe_out
        if final_norm:
            xn = xn * lax.rsqrt(jnp.mean(xn * xn, axis=-1, keepdims=True) + EPS) * fg_ref[...]
        o_ref[...] = xn


def moe(h2, gate, w1, w3, w2, x, gt2, final_g, S, tm, final_norm):
    T, D = x.shape
    E, _, FF = w1.shape
    if gt2.shape[0] == 1:
        tm = min(tm, T)
        gt_spec = pl.BlockSpec((None, 1, D), lambda i, e: (0, 0, 0))
    else:
        tm = min(tm, S)
        per_b = S // tm
        gt_spec = pl.BlockSpec((None, 1, D), lambda i, e: (i // per_b, 0, 0))
    A = tm + N_GROUPS * MOE_BLK
    return pl.pallas_call(
        functools.partial(_moe_kernel, final_norm=final_norm),
        grid=(T // tm, E // MOE_STEP),
        in_specs=[pl.BlockSpec((tm, D), lambda i, e: (i, 0)),
                  pl.BlockSpec((tm, LANES), lambda i, e: (i, 0)),
                  pl.BlockSpec((MOE_STEP, D, FF), lambda i, e: (e, 0, 0)),
                  pl.BlockSpec((MOE_STEP, D, FF), lambda i, e: (e, 0, 0)),
                  pl.BlockSpec((MOE_STEP, FF, D), lambda i, e: (e, 0, 0)),
                  pl.BlockSpec((tm, D), lambda i, e: (i, 0)),
                  gt_spec,
                  pl.BlockSpec((1, D), lambda i, e: (0, 0))],
        out_specs=pl.BlockSpec((tm, D), lambda i, e: (i, 0)),
        out_shape=jax.ShapeDtypeStruct((T, D), F32),
        scratch_shapes=[pltpu.VMEM((A, D), BF16), pltpu.VMEM((A, LANES), F32), pltpu.VMEM((A, D), F32),
                        pltpu.VMEM((tm, LANES), F32), pltpu.SMEM((2 * N_GROUPS,), jnp.int32)],
        compiler_params=_cparams("parallel", "arbitrary"),
        name="moe",
    )(h2, gate, w1, w3, w2, x, gt2, final_g)


IN_SIZES = (3 * HY_W, GDN_QKV, GDN_HEADS * GDN_DV, 2 * GDN_HEADS, 2 * GDN_HEADS, MLA_Q_LORA, MLA_KV_LORA, MLA_ROPE)


def _split_w_in(w_in):
    D = w_in.shape[0]
    parts, off = [], 0
    for n in IN_SIZES:
        parts.append(w_in[:, off:off + n])
        off += n
    hy, qkv, z, a, b, cq, ckv, kr = parts
    gate = w_in[:, off:]
    zeros = lambda n: jnp.zeros((D, n), w_in.dtype)
    w_hy = hy
    w_gdn = jnp.concatenate([qkv, a, b, zeros(LANES - 4 * GDN_HEADS)], axis=1)
    w_mla = jnp.concatenate([cq, ckv, kr, zeros(LANES - MLA_ROPE)], axis=1)
    w_gate = jnp.concatenate([z, gate], axis=1)
    return [w.astype(BF16) for w in (w_hy, w_gdn, w_mla, w_gate)]


def _layer(x, cx, mod, mod_c, lw, tabs, router_w, router_b, final_g, update_ctx, last):
    B, S, D = x.shape
    Lc = cx.shape[1]
    sh1, sc1, gt1, sh2, sc2, gt2 = [m[:, None, :] for m in jnp.split(mod, 6, axis=-1)]
    csh1, csc1, cgt1, csh2, csc2, cgt2 = [m[:, None, :] for m in jnp.split(mod_c, 6, axis=-1)]
    n1g = lw['norm1_g'][None, :]
    w_hy, w_gdn, w_mla, w_gate = _split_w_in(lw['w_in'])

    Lt = S + Lc
    TM, TMC = 512, 256

    def project(w, joint):
        if joint:
            return in_proj_joint(x, cx, n1g, sc1, sh1, csc1, csh1, w, TM, TMC)
        return in_proj(x, n1g, sc1, sh1, w, TM), in_proj(cx, n1g, csc1, csh1, w, TMC)

    p_hy, c_hy = project(w_hy, False)
    p_gate, c_gate = project(w_gate, False)
    pc_gdn = project(w_gdn, True)
    pc_mla = project(w_mla, True)

    o_f, o_b = gdn_scan(*gdn_pre(pc_gdn, lw['gdn_conv_w'], lw['gdn_a_log'], lw['gdn_dt_bias'], TMC, S),
                        Lc // GDN_CHUNK)

    wq2, wk2 = mla_weights(lw['mla_w_uq'], lw['mla_w_ukv'])
    gq, gkv = lw['mla_q_norm_g'][None, :], lw['mla_kv_norm_g'][None, :]
    q_a, k_a, v_a = mla_proj(pc_mla, gq, gkv, wq2, wk2, tabs[0], tabs[1], TMC)
    attn_l = mla_attn(q_a, k_a, v_a, (0, S), (0, Lt), 512, 1024)

    def hyena(p, L):
        filt = hy_filter(L, lw['hy_f_w1'], lw['hy_f_b1'], lw['hy_f_w2'], lw['hy_f_b2'], lw['hy_f_w3'],
                         lw['hy_f_freq'], lw['hy_decay'])
        x0, u = hy_pre(p, lw['hy_conv_w'], lw['hy_conv_b'][None, :], 512)
        return hyena_long_conv(x0, u, filt, lw['hy_bias'])

    hyv_l = hyena(p_hy, S)

    wb = lambda name: lw[name].astype(BF16)
    rw = jnp.pad(router_w, ((0, 0), (0, LANES - N_EXPERTS)))
    rw_hi = rw.astype(BF16)
    rw = jnp.concatenate([rw_hi, (rw - rw_hi.astype(F32)).astype(BF16)], axis=1)
    rb = jnp.pad(router_b[None, :], ((0, 0), (0, LANES - N_EXPERTS)))
    n2g, gng = lw['norm2_g'][None, :], lw['gdn_norm_g'][None, :]
    w1, w3, w2 = wb('moe_w1'), wb('moe_w3'), wb('moe_w2')

    def finish(xx, hyv, o_row0, pg, attn, gt1_, sc2_, sh2_, gt2_, tm, tm_moe, fin):
        Bx, Sx, _ = xx.shape
        xn, h2, gate = merge_out(hyv, o_f, o_b, pg, attn, xx, gt1_, sc2_, sh2_, gng, n2g, wb('hy_out'), wb('gdn_out'),
                                 wb('mla_out'), wb('w_out'), rw, rb, tm, o_row0)
        out = moe(h2.reshape(Bx * Sx, D), gate.reshape(Bx * Sx, LANES), w1, w3, w2, xn.reshape(Bx * Sx, D),
                  gt2_, final_g, Sx, tm_moe, fin)
        return out.reshape(Bx, Sx, D)

    x_new = finish(x, hyv_l, 0, p_gate, attn_l, gt1, sc2, sh2, gt2, 512, 1024, last)
    if update_ctx:
        hyv_c = hyena(c_hy, Lc)
        attn_c = mla_attn(q_a, k_a, v_a, (S, Lc), (S, Lc), 256, 1024)
        cx = finish(cx, hyv_c, S, c_gate, attn_c, cgt1, csc2, csh2, cgt2, 256, 1024, False)
    return x_new, cx


def kernel(x, c, ctx, c_ctx, w_ada, b_ada, norm1_g, norm2_g, w_in, hy_conv_w, hy_conv_b, hy_f_w1, hy_f_b1, hy_f_w2, hy_f_b2, hy_f_w3, hy_f_freq, hy_decay, hy_bias, hy_out, gdn_conv_w, gdn_a_log, gdn_dt_bias, gdn_norm_g, gdn_out, mla_q_norm_g, mla_w_uq, mla_kv_norm_g, mla_w_ukv, mla_out, w_out, moe_w1, moe_w3, moe_w2, router_w, router_b, final_norm_g):
    per_layer = dict(norm1_g=norm1_g, norm2_g=norm2_g, w_in=w_in, hy_conv_w=hy_conv_w, hy_conv_b=hy_conv_b,
                     hy_f_w1=hy_f_w1, hy_f_b1=hy_f_b1, hy_f_w2=hy_f_w2, hy_f_b2=hy_f_b2, hy_f_w3=hy_f_w3,
                     hy_f_freq=hy_f_freq, hy_decay=hy_decay, hy_bias=hy_bias, hy_out=hy_out,
                     gdn_conv_w=gdn_conv_w, gdn_a_log=gdn_a_log, gdn_dt_bias=gdn_dt_bias, gdn_norm_g=gdn_norm_g,
                     gdn_out=gdn_out, mla_q_norm_g=mla_q_norm_g, mla_w_uq=mla_w_uq, mla_kv_norm_g=mla_kv_norm_g,
                     mla_w_ukv=mla_w_ukv, mla_out=mla_out, w_out=w_out, moe_w1=moe_w1, moe_w3=moe_w3, moe_w2=moe_w2)
    B, S, D = x.shape
    Lc = ctx.shape[1]
    depth = w_ada.shape[0]
    rows = S // GRID_W
    row = jnp.repeat(jnp.arange(rows, dtype=jnp.int32), GRID_W)
    col = jnp.tile(jnp.arange(GRID_W, dtype=jnp.int32), rows)
    zero = jnp.zeros((Lc,), jnp.int32)
    tabs = rope_tables(jnp.concatenate([row, zero]), jnp.concatenate([col, zero]))
    cc = jnp.concatenate([c, c_ctx[None, :], jnp.zeros((2 * SUBLANES - B - 1, D), F32)], axis=0)
    final_g = final_norm_g[None, :]
    cx = ctx
    for l in range(depth):
        lw = {k: v[l] for k, v in per_layer.items()}
        mods = ada_mod(cc, w_ada[l], b_ada[l][None, :])
        x, cx = _layer(x, cx, mods[:B], mods[B:B + 1], lw, tabs, router_w, router_b, final_g,
                       l < depth - 1, l == depth - 1)
    return x
```

```python
import functools
import math

import jax
import jax.numpy as jnp
from jax import lax
from jax.experimental import pallas as pl
from jax.experimental.pallas import tpu as pltpu

F32 = jnp.float32
BF16 = jnp.bfloat16
HI = lax.Precision.HIGHEST
EPS = 1e-6

GRID_W = 64
HY_W = 512
HY_EMB = 33
HY_BANDS = (HY_EMB - 1) // 2
HY_MOD_SHIFT = 0.05
GDN_HEADS = 4
GDN_DK = 128
GDN_DV = 128
GDN_CHUNK = 64
MLA_HEADS = 8
MLA_NOPE = 64
MLA_ROPE = 32
MLA_V = 64
MLA_Q_LORA = 768
MLA_KV_LORA = 256
ROPE_THETA = 10000.0
N_EXPERTS = 16
N_GROUPS = 4
EXPERTS_PER_GROUP = N_EXPERTS // N_GROUPS
EXPERT_FF = 512
LANES = 128
SUBLANES = 8
VMEM_LIMIT = 56 * 1024 * 1024


def _cparams(*sem):
    return pltpu.CompilerParams(dimension_semantics=sem, vmem_limit_bytes=VMEM_LIMIT)


def _silu(x):
    return x * jax.nn.sigmoid(x)


def _ada_kernel(c_ref, w_ref, b_ref, o_ref):
    a = _silu(c_ref[...])
    o_ref[...] = jnp.dot(a, w_ref[...], precision=HI, preferred_element_type=F32) + b_ref[...]


def ada_mod(cc, w, b):
    R, D = cc.shape
    N = w.shape[1]
    tn = 1536
    return pl.pallas_call(
        _ada_kernel,
        grid=(N // tn,),
        in_specs=[pl.BlockSpec((R, D), lambda j: (0, 0)),
                  pl.BlockSpec((D, tn), lambda j: (0, j)),
                  pl.BlockSpec((1, tn), lambda j: (0, j))],
        out_specs=pl.BlockSpec((R, tn), lambda j: (0, j)),
        out_shape=jax.ShapeDtypeStruct((R, N), F32),
        compiler_params=_cparams("parallel"),
        name="ada_mod",
    )(cc, w, b)


def _norm_mod_matmul(x_ref, g, sc, sh, w_ref, o_ref):
    tm = x_ref.shape[0]
    hs = []
    for rs in ((slice(0, tm // 2), slice(tm // 2, tm)) if tm >= 4 * LANES else (slice(0, tm),)):
        x = x_ref[rs, :]
        y = x * lax.rsqrt(jnp.mean(x * x, axis=-1, keepdims=True) + EPS) * g
        hs.append((rs, (y * (1.0 + sc) + sh).astype(BF16)))
    for rs, h in hs:
        o_ref[rs, :] = jnp.dot(h, w_ref[...], preferred_element_type=F32).astype(o_ref.dtype)


def _inproj_kernel(x_ref, g_ref, sc_ref, sh_ref, w_ref, o_ref):
    _norm_mod_matmul(x_ref, g_ref[...], sc_ref[...], sh_ref[...], w_ref, o_ref)


def _mod_spec(m, D):
    if m.shape[0] == 1:
        return pl.BlockSpec((None, 1, D), lambda b, i: (0, 0, 0))
    return pl.BlockSpec((None, 1, D), lambda b, i: (b, 0, 0))


def in_proj(x, g, sc, sh, w, tm):
    B, S, D = x.shape
    N = w.shape[1]
    tm = min(tm, S)
    return pl.pallas_call(
        _inproj_kernel,
        grid=(B, S // tm),
        in_specs=[pl.BlockSpec((None, tm, D), lambda b, i: (b, i, 0)),
                  pl.BlockSpec((1, D), lambda b, i: (0, 0)),
                  _mod_spec(sc, D), _mod_spec(sh, D),
                  pl.BlockSpec((D, N), lambda b, i: (0, 0))],
        out_specs=pl.BlockSpec((None, tm, N), lambda b, i: (b, i, 0)),
        out_shape=jax.ShapeDtypeStruct((B, S, N), F32),
        compiler_params=_cparams("parallel", "parallel"),
        name="in_proj",
    )(x, g, sc, sh, w)


def _inproj_head_kernel(x_ref, g_ref, sc_ref, sh_ref, w_ref, o_ref, *, n_lat):
    i = pl.program_id(1)

    @pl.when(i < n_lat)
    def _():
        _inproj_kernel(x_ref, g_ref, sc_ref, sh_ref, w_ref, o_ref)

    @pl.when(i == n_lat)
    def _():
        o_ref[...] = jnp.zeros_like(o_ref)


def _inproj_tail_kernel(x_ref, g_ref, sc_ref, sh_ref, w_ref, dst_ref, o_ref):
    del dst_ref
    _inproj_kernel(x_ref, g_ref, sc_ref, sh_ref, w_ref, o_ref)


def in_proj_joint(x, cx, g, sc, sh, csc, csh, w, tm, tmc):
    B, S, D = x.shape
    Lc = cx.shape[1]
    N = w.shape[1]
    assert S % tm == 0 and Lc % tmc == 0 and S % tmc == 0 and Lc <= tm
    n_lat = S // tm
    specs = lambda s_, h_, imap: [pl.BlockSpec((None, imap[0], D), imap[1]), pl.BlockSpec((1, D), lambda b, i: (0, 0)),
                                  _mod_spec(s_, D), _mod_spec(h_, D), pl.BlockSpec((D, N), lambda b, i: (0, 0))]
    head = pl.pallas_call(
        functools.partial(_inproj_head_kernel, n_lat=n_lat),
        grid=(B, n_lat + 1),
        in_specs=specs(sc, sh, (tm, lambda b, i: (b, jnp.minimum(i, n_lat - 1), 0))),
        out_specs=pl.BlockSpec((None, tm, N), lambda b, i: (b, i, 0)),
        out_shape=jax.ShapeDtypeStruct((B, S + Lc, N), F32),
        compiler_params=_cparams("parallel", "parallel"),
        name="in_proj",
    )(x, g, sc, sh, w)
    off = S // tmc
    return pl.pallas_call(
        _inproj_tail_kernel,
        grid=(B, Lc // tmc),
        in_specs=specs(csc, csh, (tmc, lambda b, i: (b, i, 0))) + [pl.BlockSpec(memory_space=pl.ANY)],
        out_specs=pl.BlockSpec((None, tmc, N), lambda b, i: (b, off + i, 0)),
        out_shape=jax.ShapeDtypeStruct((B, S + Lc, N), F32),
        input_output_aliases={5: 0},
        compiler_params=_cparams("parallel", "parallel"),
        name="in_proj",
    )(cx, g, csc, csh, w, head)


def _halo_specs(tm, S, C, col_block=0):
    nb8 = tm // SUBLANES
    last8 = S // SUBLANES - 1
    main = pl.BlockSpec((None, tm, C), lambda b, i: (b, i, col_block))
    prev = pl.BlockSpec((None, SUBLANES, C), lambda b, i: (b, jnp.maximum(i * nb8 - 1, 0), col_block))
    nxt = pl.BlockSpec((None, SUBLANES, C), lambda b, i: (b, jnp.minimum((i + 1) * nb8, last8), col_block))
    return main, prev, nxt


def _conv3(x, prev8, next8, w, first, last):
    tm = x.shape[0]
    row = lax.broadcasted_iota(jnp.int32, x.shape, 0)
    p_row = jnp.where(first, 0.0, prev8[SUBLANES - 1:SUBLANES, :])
    n_row = jnp.where(last, 0.0, next8[0:1, :])
    x_prev = jnp.where(row == 0, p_row, pltpu.roll(x, 1, 0))
    x_next = jnp.where(row == tm - 1, n_row, pltpu.roll(x, tm - 1, 0))
    return x_prev * w[0:1, :] + x * w[1:2, :] + x_next * w[2:3, :]


def _hy_pre_kernel(p_ref, pp_ref, pn_ref, w_ref, b_ref, x0_ref, u_ref):
    i = pl.program_id(1)
    y = _conv3(p_ref[...], pp_ref[...], pn_ref[...], w_ref[...], i == 0, i == pl.num_programs(1) - 1)
    y = y + b_ref[...]
    x0_ref[...] = y[:, :HY_W]
    u_ref[...] = y[:, HY_W:2 * HY_W] * y[:, 2 * HY_W:]


def hy_pre(p_hy, conv_w, conv_b, tm):
    B, S, C = p_hy.shape
    tm = min(tm, S)
    main, prev, nxt = _halo_specs(tm, S, C)
    o_spec = pl.BlockSpec((None, tm, HY_W), lambda b, i: (b, i, 0))
    return pl.pallas_call(
        _hy_pre_kernel,
        grid=(B, S // tm),
        in_specs=[main, prev, nxt,
                  pl.BlockSpec((3, C), lambda b, i: (0, 0)),
                  pl.BlockSpec((1, C), lambda b, i: (0, 0))],
        out_specs=[o_spec, o_spec],
        out_shape=[jax.ShapeDtypeStruct((B, S, HY_W), F32)] * 2,
        compiler_params=_cparams("parallel", "parallel"),
        name="hy_pre",
    )(p_hy, p_hy, p_hy, conv_w, conv_b)


def _hy_filter_kernel(z_ref, w1_ref, b1_ref, w2_ref, b2_ref, w3_ref, fq_ref, dc_ref, o_ref):
    fq = fq_ref[...]
    h = jnp.sin(fq * (jnp.dot(z_ref[...], w1_ref[...], precision=HI, preferred_element_type=F32) + b1_ref[...]))
    h = jnp.sin(fq * (jnp.dot(h, w2_ref[...], precision=HI, preferred_element_type=F32) + b2_ref[...]))
    h = jnp.dot(h, w3_ref[...], precision=HI, preferred_element_type=F32)
    taps = h * (jnp.exp(-z_ref[:, 0:1] * jnp.abs(dc_ref[...])) + HY_MOD_SHIFT)
    C = o_ref.shape[1]
    is_fwd = z_ref[:, LANES - 2:LANES - 1] > 0.5
    o_ref[...] = jnp.where(is_fwd, taps[:, :C], taps[:, C:]) * z_ref[:, LANES - 1:LANES]


def hy_filter(L, w1, b1, w2, b2, w3, freq, decay):
    t = jnp.linspace(0.0, 1.0, L, dtype=F32)[:, None]
    w = (2.0 * math.pi / L) * jnp.arange(L, dtype=F32)[:, None]
    f = jnp.linspace(1e-4, HY_BANDS - 1, HY_BANDS, dtype=F32)[None, :]
    z = jnp.concatenate([t, jnp.cos(f * w), -jnp.sin(f * w)], axis=-1)
    emb_pad = LANES - HY_EMB
    lag = jnp.concatenate([jnp.arange(L), jnp.zeros((1,), jnp.int32), jnp.arange(L - 1, 0, -1)])
    r = jnp.arange(2 * L)
    flags = jnp.stack([(r < L).astype(F32), (r != L).astype(F32)], axis=1)
    z = jnp.concatenate([z[lag], jnp.zeros((2 * L, emb_pad - 2), F32), flags], axis=1)
    w1 = jnp.pad(w1, ((0, emb_pad), (0, 0)))
    tl = min(2 * L, 512)
    C = w3.shape[1] // 2
    full = lambda a: pl.BlockSpec(a.shape, lambda i: (0, 0))
    b1, b2, freq, decay = b1[None, :], b2[None, :], freq[None, :], decay[None, :]
    return pl.pallas_call(
        _hy_filter_kernel,
        grid=(2 * L // tl,),
        in_specs=[pl.BlockSpec((tl, LANES), lambda i: (i, 0)),
                  full(w1), full(b1), full(w2), full(b2), full(w3), full(freq), full(decay)],
        out_specs=pl.BlockSpec((tl, C), lambda i: (i, 0)),
        out_shape=jax.ShapeDtypeStruct((2 * L, C), F32),
        compiler_params=_cparams("parallel"),
        name="hy_filter",
    )(z, w1, b1, w2, b2, w3, freq, decay)


def _dft_tables(N1, N2):
    N = N1 * N2
    two_pi = 2.0 * math.pi

    def cs(num, den):
        ang = (two_pi / den) * (num % den).astype(F32)
        return jnp.cos(ang), jnp.sin(ang)

    a1 = jnp.arange(N1, dtype=jnp.int32)
    a2 = jnp.arange(N2, dtype=jnp.int32)
    c1, s1 = cs(a1[:, None] * a1[None, :], N1)
    c2, s2 = cs(a2[:, None] * a2[None, :], N2)
    ct, st = cs(a2[:, None] * a1[None, :], N)

    def stack(re, im):
        return jnp.concatenate([jnp.concatenate([re, -im], axis=-1),
                                jnp.concatenate([im, re], axis=-1)], axis=-2)

    tr = ct[:, :, None] * c1[None] - st[:, :, None] * s1[None]
    ti = -(ct[:, :, None] * s1[None] + st[:, :, None] * c1[None])
    h = N1 // 2
    m1_data = stack(tr[:, :, :h], ti[:, :, :h])
    m1_real = jnp.concatenate([tr, ti], axis=-2)
    m2 = stack(c2, -s2)
    ctk, stk = ct.T, st.T
    gr = ctk[:, :, None] * c2.T[None] - stk[:, :, None] * s2.T[None]
    gi = ctk[:, :, None] * s2.T[None] + stk[:, :, None] * c2.T[None]
    m2inv = stack(gr, gi)
    er, ei = c1.T[:h] / N, s1.T[:h] / N
    m3 = stack(er, ei)
    return [_hi_lo_rows(m) for m in (m1_data, m1_real, m2, m2inv, m3)]


def _hi_lo_rows(m):
    hi = m.astype(BF16)
    lo = (m - hi.astype(F32)).astype(BF16)
    return jnp.concatenate([hi, lo], axis=-2)


def _dot3(m2, x):
    M = m2.shape[0] // 2
    x_hi = x.astype(BF16)
    x_lo = (x - x_hi.astype(F32)).astype(BF16)
    a = jnp.dot(m2, x_hi, preferred_element_type=F32)
    return a[:M] + a[M:] + jnp.dot(m2[:M], x_lo, preferred_element_type=F32)


def _hy_pass_a_kernel(u_ref, m_ref, o_ref):
    n1 = o_ref.shape[1]
    for j in range(SUBLANES):
        xj = jnp.concatenate([u_ref[0, :, j, :], u_ref[1, :, j, :]], axis=0)
        a = _dot3(m_ref[j], xj)
        o_ref[0, :, j, :] = a[:n1]
        o_ref[1, :, j, :] = a[n1:]


def _hy_pass_a(u4, m1, n_pairs):
    _, h, N2, C = u4.shape
    N1 = 2 * h
    return pl.pallas_call(
        _hy_pass_a_kernel,
        grid=(n_pairs, N2 // SUBLANES),
        in_specs=[pl.BlockSpec((2, h, SUBLANES, C), lambda p, j: (p, 0, j, 0)),
                  pl.BlockSpec((SUBLANES, 4 * N1, N1), lambda p, j: (j, 0, 0))],
        out_specs=pl.BlockSpec((2, N1, SUBLANES, C), lambda p, j: (0, 0, j, p)),
        out_shape=jax.ShapeDtypeStruct((2, N1, N2, n_pairs * C), F32),
        compiler_params=_cparams("parallel", "parallel"),
        name="hy_pass_a",
    )(u4, m1)


def _hy_spec_kernel(a_ref, m2_ref, o_ref):
    n2 = a_ref.shape[2]
    for j in range(SUBLANES):
        a = jnp.concatenate([a_ref[0, j], a_ref[1, j]], axis=0)
        x = _dot3(m2_ref[...], a)
        o_ref[0, j] = x[:n2]
        o_ref[1, j] = x[n2:]


def _hy_spectrum(a, m2):
    _, N1, N2, C = a.shape
    spec = pl.BlockSpec((2, SUBLANES, N2, C), lambda k: (0, k, 0, 0))
    return pl.pallas_call(
        _hy_spec_kernel,
        grid=(N1 // SUBLANES,),
        in_specs=[spec, pl.BlockSpec((4 * N2, 2 * N2), lambda k: (0, 0))],
        out_specs=spec,
        out_shape=jax.ShapeDtypeStruct(a.shape, F32),
        compiler_params=_cparams("parallel"),
        name="hy_spectrum",
    )(a, m2)


def _hy_pass_b_kernel(a_ref, k_ref, m2_ref, mi_ref, o_ref):
    n2 = a_ref.shape[2]
    for j in range(SUBLANES):
        a = jnp.concatenate([a_ref[0, j], a_ref[1, j]], axis=0)
        x = _dot3(m2_ref[...], a)
        xr, xi = x[:n2], x[n2:]
        kr, ki = k_ref[0, j], k_ref[1, j]
        y = jnp.concatenate([xr * kr - xi * ki, xr * ki + xi * kr], axis=0)
        b = _dot3(mi_ref[j], y)
        o_ref[0, :, j, :] = b[:n2]
        o_ref[1, :, j, :] = b[n2:]


def _hy_pass_b(a, kf, m2, m2inv, n_pairs):
    _, N1, N2, PC = a.shape
    C = PC // n_pairs
    return pl.pallas_call(
        _hy_pass_b_kernel,
        grid=(n_pairs, N1 // SUBLANES),
        in_specs=[pl.BlockSpec((2, SUBLANES, N2, C), lambda p, k: (0, k, 0, p)),
                  pl.BlockSpec((2, SUBLANES, N2, C), lambda p, k: (0, k, 0, 0)),
                  pl.BlockSpec((4 * N2, 2 * N2), lambda p, k: (0, 0)),
                  pl.BlockSpec((SUBLANES, 4 * N2, 2 * N2), lambda p, k: (k, 0, 0))],
        out_specs=pl.BlockSpec((2, N2, SUBLANES, C), lambda p, k: (0, 0, k, p)),
        out_shape=jax.ShapeDtypeStruct((2, N2, N1, PC), F32),
        compiler_params=_cparams("parallel", "parallel"),
        name="hy_pass_b",
    )(a, kf, m2, m2inv)


def _hy_pass_c_kernel(b_ref, u_ref, x0_ref, bias_ref, m3_ref, o_ref):
    h = o_ref.shape[1]
    bias = bias_ref[...]
    for j in range(SUBLANES):
        bb = jnp.concatenate([b_ref[0, j], b_ref[1, j]], axis=0)
        y = _dot3(m3_ref[...], bb)
        for r in range(2):
            o_ref[r, :, j, :] = x0_ref[r, :, j, :] * (y[r * h:(r + 1) * h] + bias * u_ref[r, :, j, :])


def _hy_pass_c(bq, u4, x04, bias, m3, n_pairs):
    _, N2, N1, PC = bq.shape
    C = PC // n_pairs
    h = N1 // 2
    io = pl.BlockSpec((2, h, SUBLANES, C), lambda p, j: (p, 0, j, 0))
    return pl.pallas_call(
        _hy_pass_c_kernel,
        grid=(n_pairs, N2 // SUBLANES),
        in_specs=[pl.BlockSpec((2, SUBLANES, N1, C), lambda p, j: (0, j, 0, p)),
                  io, io,
                  pl.BlockSpec((1, C), lambda p, j: (0, 0)),
                  pl.BlockSpec((2 * N1, 2 * N1), lambda p, j: (0, 0))],
        out_specs=io,
        out_shape=jax.ShapeDtypeStruct(u4.shape, F32),
        compiler_params=_cparams("parallel", "parallel"),
        name="hy_pass_c",
    )(bq, u4, x04, bias, m3)


def hyena_long_conv(x0, u, kbuf, bias):
    B, L, C = u.shape
    N2 = min(128, L // 32)
    N1 = 2 * L // N2
    n_pairs = B // 2
    m1_data, m1_real, m2, m2inv, m3 = _dft_tables(N1, N2)
    kf = _hy_spectrum(_hy_pass_a(kbuf.reshape(2, N1 // 2, N2, C), m1_real, 1), m2)
    u4 = u.reshape(B, N1 // 2, N2, C)
    a = _hy_pass_a(u4, m1_data, n_pairs)
    bq = _hy_pass_b(a, kf, m2, m2inv, n_pairs)
    y = _hy_pass_c(bq, u4, x0.reshape(u4.shape), bias[None, :], m3, n_pairs)
    return y.reshape(B, L, C)


GDN_QKV = GDN_HEADS * (2 * GDN_DK + GDN_DV)


def _gdn_pre_kernel(p_ref, pp_ref, pn_ref, w_ref, alog_ref, dtb_ref, q_ref, k_ref, v_ref, gb_ref, *, n_first):
    i = pl.program_id(1)
    C = GDN_QKV
    first = jnp.logical_or(i == 0, i == n_first)
    last = jnp.logical_or(i == n_first - 1, i == pl.num_programs(1) - 1)
    y = _conv3(p_ref[:, :C], pp_ref[:, :C], pn_ref[:, :C], w_ref[...], first, last)
    y = _silu(y)
    nk = GDN_HEADS * GDN_DK
    for h in range(GDN_HEADS):
        sl = slice(h * GDN_DK, (h + 1) * GDN_DK)
        qh = y[:, sl]
        kh = y[:, nk + h * GDN_DK:nk + (h + 1) * GDN_DK]
        q_ref[:, sl] = qh * (lax.rsqrt(jnp.sum(qh * qh, axis=-1, keepdims=True) + EPS) * (GDN_DK ** -0.5))
        k_ref[:, sl] = kh * lax.rsqrt(jnp.sum(kh * kh, axis=-1, keepdims=True) + EPS)
    v_ref[...] = y[:, 2 * nk:]
    s = p_ref[:, C:]
    lane = lax.broadcasted_iota(jnp.int32, s.shape, 1)
    xa = s + dtb_ref[...]
    softplus = jnp.maximum(xa, 0.0) + jnp.log1p(jnp.exp(-jnp.abs(xa)))
    g = -jnp.exp(alog_ref[...]) * softplus
    gb_ref[...] = jnp.where(lane < 2 * GDN_HEADS, g, jnp.where(lane < 4 * GDN_HEADS, jax.nn.sigmoid(s), 0.0))


def gdn_pre(p_gdn, conv_w, a_log, dt_bias, tm, first_rows):
    B, S, C = p_gdn.shape
    assert first_rows % tm == 0 and S % tm == 0
    main, prev, nxt = _halo_specs(tm, S, C)
    pad = LANES - 2 * GDN_HEADS
    alog = jnp.pad(a_log.reshape(1, -1), ((0, 0), (0, pad)))
    dtb = jnp.pad(dt_bias.reshape(1, -1), ((0, 0), (0, pad)))
    nv = GDN_HEADS * GDN_DV
    o_spec = pl.BlockSpec((None, tm, nv), lambda b, i: (b, i, 0))
    return pl.pallas_call(
        functools.partial(_gdn_pre_kernel, n_first=first_rows // tm),
        grid=(B, S // tm),
        in_specs=[main, prev, nxt,
                  pl.BlockSpec((3, GDN_QKV), lambda b, i: (0, 0)),
                  pl.BlockSpec((1, LANES), lambda b, i: (0, 0)),
                  pl.BlockSpec((1, LANES), lambda b, i: (0, 0))],
        out_specs=[o_spec, o_spec, o_spec, pl.BlockSpec((None, tm, LANES), lambda b, i: (b, i, 0))],
        out_shape=[jax.ShapeDtypeStruct((B, S, nv), F32)] * 3 + [jax.ShapeDtypeStruct((B, S, LANES), F32)],
        compiler_params=_cparams("parallel", "parallel"),
        name="gdn_pre",
    )(p_gdn, p_gdn, p_gdn, conv_w, alog, dtb)


def _bdot(a, b):
    return jnp.dot(a.astype(BF16), b.astype(BF16), preferred_element_type=F32)


def _bdot_nt(a, b):
    return lax.dot_general(a.astype(BF16), b.astype(BF16), (((1,), (1,)), ((), ())), preferred_element_type=F32)


def _bdot_tn(a, b):
    return lax.dot_general(a.astype(BF16), b.astype(BF16), (((0,), (0,)), ((), ())), preferred_element_type=F32)


GDN_STEP_CHUNKS = 4


def _gdn_scan_kernel(qf_ref, kf_ref, vf_ref, gf_ref, qb_ref, kb_ref, vb_ref, gb_ref, of_ref, ob_ref, s_ref):
    C = GDN_CHUNK
    H = GDN_HEADS
    G = qf_ref.shape[0] // C
    in_refs = ((qf_ref, kf_ref, vf_ref, gf_ref), (qb_ref, kb_ref, vb_ref, gb_ref))
    o_refs = (of_ref, ob_ref)

    @pl.when(pl.program_id(1) == 0)
    def _():
        s_ref[...] = jnp.zeros_like(s_ref)

    ri = lax.broadcasted_iota(jnp.int32, (C, C), 0)
    ci = lax.broadcasted_iota(jnp.int32, (C, C), 1)
    eye = (ri == ci).astype(F32)
    incl = ((ci <= ri), (ci >= ri))
    strict = ((ci < ri), (ci > ri))
    rows = lambda g: slice(g * C, (g + 1) * C)
    cols = lambda h: slice(h * GDN_DK, (h + 1) * GDN_DK)
    chains = [(d, g, h) for d in range(2) for g in range(G) for h in range(H)]

    gbv = {(d, g): in_refs[d][3][rows(g), :] for d in range(2) for g in range(G)}
    gc_all = {dg: jnp.dot(incl[dg[0]].astype(F32), gbv[dg], precision=HI, preferred_element_type=F32)
              for dg in gbv}
    gc_t = {dg: gc_all[dg].T for dg in gbv}

    kk, gamma, rhs, qe, kdec, gend = {}, {}, {}, {}, {}, {}
    for ch in chains:
        d, g, h = ch
        c = d * H + h
        q, k, v = (in_refs[d][n][rows(g), cols(h)] for n in range(3))
        end = C - 1 if d == 0 else 0
        beta = gbv[d, g][:, 2 * H + c:2 * H + c + 1]
        gc_c = gc_all[d, g][:, c:c + 1]
        gc_r = gc_t[d, g][c:c + 1, :]
        g_tot = gc_all[d, g][end:end + 1, c:c + 1]
        gamma[ch] = jnp.where(incl[d], jnp.exp(jnp.where(incl[d], gc_c - gc_r, 0.0)), 0.0)
        e_c = jnp.exp(gc_c)
        kb = k * beta
        kk[ch] = _bdot_nt(jnp.concatenate([kb, q], axis=0), k)
        rhs[ch] = jnp.concatenate([v * beta, kb * e_c], axis=1).astype(BF16)
        qe[ch] = q * e_c
        kdec[ch] = (k * jnp.exp(g_tot - gc_c)).astype(BF16)
        gend[ch] = jnp.exp(g_tot)
    m = {ch: jnp.where(strict[ch[0]], kk[ch][:C] * gamma[ch], 0.0) for ch in chains}
    a_intra = {ch: (kk[ch][C:] * gamma[ch]).astype(BF16) for ch in chains}
    t = {ch: eye - m[ch] for ch in chains}
    pw = m
    for _ in range(5):
        pw = {ch: _bdot(pw[ch], pw[ch]) for ch in chains}
        t = {ch: t[ch] + _bdot(t[ch], pw[ch]) for ch in chains}
    uw = {ch: jnp.dot(t[ch].astype(BF16), rhs[ch], preferred_element_type=F32) for ch in chains}
    wq = {ch: jnp.concatenate([uw[ch][:, GDN_DV:], qe[ch]], axis=0).astype(BF16) for ch in chains}

    heads = [(d, h) for d in range(2) for h in range(H)]
    s = {dh: s_ref[dh[0] * H + dh[1]] for dh in heads}
    for j in range(G):
        at = lambda dh: (dh[0], j if dh[0] == 0 else G - 1 - j, dh[1])
        ws = {dh: jnp.dot(wq[at(dh)], s[dh].astype(BF16), preferred_element_type=F32) for dh in heads}
        v_new = {dh: uw[at(dh)][:, :GDN_DV] - ws[dh][:C] for dh in heads}
        for dh in heads:
            d, g, h = at(dh)
            o_refs[d][rows(g), cols(h)] = ws[dh][C:] + jnp.dot(a_intra[at(dh)], v_new[dh].astype(BF16),
                                                               preferred_element_type=F32)
        s = {dh: s[dh] * gend[at(dh)] + _bdot_tn(kdec[at(dh)], v_new[dh]) for dh in heads}
    for dh in heads:
        s_ref[dh[0] * H + dh[1]] = s[dh]


def gdn_scan(q, k, v, gb, n_ctx_chunks):
    B, Lt, NV = q.shape
    R = GDN_CHUNK * GDN_STEP_CHUNKS
    n = Lt // R
    n_ctx = n_ctx_chunks // GDN_STEP_CHUNKS
    n_lat = n - n_ctx
    assert n * R == Lt and n_ctx * GDN_STEP_CHUNKS == n_ctx_chunks

    def fwd(b, i):
        return (b, jnp.where(i < n_ctx, n_lat + i, i - n_ctx), 0)

    def bwd(b, i):
        return (b, n - 1 - i, 0)

    def specs(imap):
        return [pl.BlockSpec((None, R, NV), imap)] * 3 + [pl.BlockSpec((None, R, LANES), imap)]

    return pl.pallas_call(
        _gdn_scan_kernel,
        grid=(B, n),
        in_specs=specs(fwd) + specs(bwd),
        out_specs=[pl.BlockSpec((None, R, NV), fwd), pl.BlockSpec((None, R, NV), bwd)],
        out_shape=[jax.ShapeDtypeStruct((B, Lt, NV), F32)] * 2,
        scratch_shapes=[pltpu.VMEM((2 * GDN_HEADS, GDN_DK, GDN_DV), F32)],
        compiler_params=_cparams("parallel", "arbitrary"),
        name="gdn_scan",
    )(q, k, v, gb, q, k, v, gb)


MLA_QK = MLA_NOPE + MLA_ROPE
MLA_KVIN = MLA_KV_LORA + LANES


def _rope_partner(w_rope):
    nf = MLA_ROPE // 4
    parts = []
    for half in range(2):
        a = w_rope[..., half * 2 * nf:half * 2 * nf + nf]
        b = w_rope[..., half * 2 * nf + nf:(half + 1) * 2 * nf]
        parts += [-b, a]
    return jnp.concatenate(parts, axis=-1)


def _head_pad(nope, rope):
    pad = jnp.zeros(nope.shape[:-1] + (LANES - MLA_QK,), nope.dtype)
    out = jnp.concatenate([nope, rope, pad], axis=-1)
    return out.reshape(out.shape[:-2] + (MLA_HEADS * LANES,))


def mla_weights(w_uq, w_ukv):
    wq = w_uq.reshape(MLA_Q_LORA, MLA_HEADS, MLA_QK)
    qn, qr = wq[..., :MLA_NOPE], wq[..., MLA_NOPE:]
    wq2 = jnp.concatenate([_head_pad(qn, qr), _head_pad(jnp.zeros_like(qn), _rope_partner(qr))], axis=-1)
    wkv = w_ukv.reshape(MLA_KV_LORA, MLA_HEADS, MLA_NOPE + MLA_V)
    kn, vv = wkv[..., :MLA_NOPE], wkv[..., MLA_NOPE:]
    eye = jnp.broadcast_to(jnp.eye(MLA_ROPE, dtype=F32)[:, None, :], (MLA_ROPE, MLA_HEADS, MLA_ROPE))
    z_kn = jnp.zeros((MLA_ROPE, MLA_HEADS, MLA_NOPE), F32)
    z_rope = jnp.zeros((MLA_KV_LORA, MLA_HEADS, MLA_ROPE), F32)
    top = jnp.concatenate([_head_pad(kn, z_rope), _head_pad(jnp.zeros_like(kn), z_rope), _head_pad(vv, z_rope)],
                          axis=-1)
    mid = jnp.concatenate([_head_pad(z_kn, eye), _head_pad(z_kn, _rope_partner(eye)),
                           jnp.zeros((MLA_ROPE, MLA_HEADS * LANES), F32)], axis=-1)
    bot = jnp.zeros((MLA_KVIN - MLA_KV_LORA - MLA_ROPE, top.shape[1]), F32)
    return wq2.astype(BF16), jnp.concatenate([top, mid, bot], axis=0).astype(BF16)


def rope_tables(row, col):
    nf = MLA_ROPE // 4
    inv_freq = ROPE_THETA ** (-jnp.arange(nf, dtype=F32) / nf)
    ang = jnp.concatenate([row.astype(F32)[:, None] * inv_freq[None, :]] * 2
                          + [col.astype(F32)[:, None] * inv_freq[None, :]] * 2, axis=-1)
    n = ang.shape[0]
    pad = jnp.zeros((n, LANES - MLA_QK), F32)
    cos = jnp.concatenate([jnp.ones((n, MLA_NOPE), F32), jnp.cos(ang), pad], axis=-1)
    sin = jnp.concatenate([jnp.zeros((n, MLA_NOPE), F32), jnp.sin(ang), pad], axis=-1)
    return cos, sin


def _mla_proj_kernel(p_ref, gq_ref, gkv_ref, wq_ref, wk_ref, cos_ref, sin_ref, q_ref, k_ref, v_ref):
    HL = MLA_HEADS * LANES
    cos = jnp.concatenate([cos_ref[...]] * MLA_HEADS, axis=1)
    sin = jnp.concatenate([sin_ref[...]] * MLA_HEADS, axis=1)
    cq = p_ref[:, :MLA_Q_LORA]
    cqn = (cq * lax.rsqrt(jnp.mean(cq * cq, axis=-1, keepdims=True) + EPS) * gq_ref[...]).astype(BF16)
    ck = p_ref[:, MLA_Q_LORA:]
    lane = lax.broadcasted_iota(jnp.int32, ck.shape, 1)
    is_kv = lane < MLA_KV_LORA
    ms = jnp.sum(jnp.where(is_kv, ck * ck, 0.0), axis=-1, keepdims=True) * (1.0 / MLA_KV_LORA)
    ckn = jnp.where(is_kv, ck * lax.rsqrt(ms + EPS) * gkv_ref[...], ck).astype(BF16)
    qq = jnp.dot(cqn, wq_ref[...], preferred_element_type=F32)
    kk = jnp.dot(ckn, wk_ref[...], preferred_element_type=F32)
    q_ref[...] = ((qq[:, :HL] * cos + qq[:, HL:] * sin) * (MLA_QK ** -0.5 * math.log2(math.e))).astype(q_ref.dtype)
    k_ref[...] = (kk[:, :HL] * cos + kk[:, HL:2 * HL] * sin).astype(k_ref.dtype)
    vv = kk[:, 2 * HL:]
    vlane = lax.broadcasted_iota(jnp.int32, vv.shape, 1)
    v_ref[...] = jnp.where(vlane % LANES == MLA_V, 1.0, vv).astype(v_ref.dtype)


def mla_proj(p_mla, gq, gkv, wq2, wk2, cos, sin, tm):
    B, S, C = p_mla.shape
    tm = min(tm, S)
    HL = MLA_HEADS * LANES
    gkv = jnp.pad(gkv, ((0, 0), (0, MLA_KVIN - MLA_KV_LORA)))
    row = lambda n: pl.BlockSpec((None, tm, n), lambda b, i: (b, i, 0))
    full = lambda a: pl.BlockSpec(a.shape, lambda b, i: (0, 0))
    tab = pl.BlockSpec((tm, LANES), lambda b, i: (i, 0))
    return pl.pallas_call(
        _mla_proj_kernel,
        grid=(B, S // tm),
        in_specs=[row(C), full(gq), full(gkv), full(wq2), full(wk2), tab, tab],
        out_specs=[row(HL)] * 3,
        out_shape=[jax.ShapeDtypeStruct((B, S, HL), BF16)] * 3,
        compiler_params=_cparams("parallel", "parallel"),
        name="mla_proj",
    )(p_mla, gq, gkv, wq2, wk2, cos, sin)


ATT_SLAB = 32


def _mla_attn_kernel(q_ref, k_ref, v_ref, o_ref, *, tk):
    tq = q_ref.shape[0]
    Tk = k_ref.shape[0]
    n_full, rem = Tk // tk, Tk % tk
    heads = (slice(0, LANES), slice(LANES, 2 * LANES))
    qs = [q_ref[:, hs] for hs in heads]

    def step(carry, start, size):
        ss = [lax.dot_general(qs[h], k_ref[pl.ds(start, size), heads[h]], (((1,), (1,)), ((), ())),
                              preferred_element_type=F32) for h in range(2)]
        out = []
        for h in range(2):
            m, acc = carry[h]
            m_new = jnp.maximum(m, jnp.max(ss[h], axis=-1, keepdims=True))
            p = jnp.concatenate([jnp.exp2(ss[h][r:r + ATT_SLAB] - m_new[r:r + ATT_SLAB]).astype(BF16)
                                 for r in range(0, tq, ATT_SLAB)], axis=0)
            acc = acc * jnp.exp2(m - m_new) + jnp.dot(p, v_ref[pl.ds(start, size), heads[h]],
                                                      preferred_element_type=F32)
            out.append((m_new, acc))
        return tuple(out)

    carry = tuple((jnp.full((tq, 1), -jnp.inf, F32), jnp.zeros((tq, LANES), F32)) for _ in range(2))
    if n_full:
        carry = lax.fori_loop(0, n_full, lambda c, cr: step(cr, pl.multiple_of(c * tk, tk), tk), carry,
                              unroll=8 if n_full % 8 == 0 else 1)
    if rem:
        carry = step(carry, n_full * tk, rem)
    o0, o1 = [acc / acc[:, MLA_V:MLA_V + 1] for _, acc in carry]
    lane = lax.broadcasted_iota(jnp.int32, (tq, LANES), 1)
    o_ref[...] = jnp.where(lane < MLA_V, o0, pltpu.roll(o1, MLA_V, 1)).astype(o_ref.dtype)


def mla_attn(q, k, v, q_rows, k_rows, tq, tk):
    B = q.shape[0]
    (q0, S), (k0, Tk) = q_rows, k_rows
    tq = min(tq, S)
    assert q0 % tq == 0 and k0 % Tk == 0
    qb, kb = q0 // tq, k0 // Tk
    return pl.pallas_call(
        functools.partial(_mla_attn_kernel, tk=tk),
        grid=(B, MLA_HEADS // 2, S // tq),
        in_specs=[pl.BlockSpec((None, tq, 2 * LANES), lambda b, h, i: (b, qb + i, h)),
                  pl.BlockSpec((None, Tk, 2 * LANES), lambda b, h, i: (b, kb, h)),
                  pl.BlockSpec((None, Tk, 2 * LANES), lambda b, h, i: (b, kb, h))],
        out_specs=pl.BlockSpec((None, tq, LANES), lambda b, h, i: (b, i, h)),
        out_shape=jax.ShapeDtypeStruct((B, S, MLA_HEADS * MLA_V), BF16),
        compiler_params=_cparams("parallel", "parallel", "arbitrary"),
        name="mla_attn",
    )(q, k, v)


def _route(logits, rb):
    lane = lax.broadcasted_iota(jnp.int32, logits.shape, 1)
    neg = -jnp.inf
    scores = jax.nn.sigmoid(logits)
    sel = scores + rb

    def top2(masked):
        m1 = jnp.max(masked, axis=-1, keepdims=True)
        i1 = jnp.min(jnp.where(masked == m1, lane, LANES), axis=-1, keepdims=True)
        rest = jnp.where(lane == i1, neg, masked)
        m2 = jnp.max(rest, axis=-1, keepdims=True)
        i2 = jnp.min(jnp.where(rest == m2, lane, LANES), axis=-1, keepdims=True)
        return m1, i1, m2, i2

    best = None
    for gi in range(N_GROUPS):
        in_g = jnp.logical_and(lane >= gi * EXPERTS_PER_GROUP, lane < (gi + 1) * EXPERTS_PER_GROUP)
        m1, _, m2, _ = top2(jnp.where(in_g, sel, neg))
        gs = m1 + m2
        if best is None:
            best, grp = gs, jnp.zeros_like(gs, dtype=jnp.int32)
        else:
            better = gs > best
            grp = jnp.where(better, gi, grp)
            best = jnp.where(better, gs, best)
    lo = grp * EXPERTS_PER_GROUP
    in_grp = jnp.logical_and(lane >= lo, lane < lo + EXPERTS_PER_GROUP)
    _, i1, _, i2 = top2(jnp.where(in_grp, sel, neg))
    picked = jnp.where(jnp.logical_or(lane == i1, lane == i2), scores, 0.0)
    gate = picked / jnp.sum(picked, axis=-1, keepdims=True)
    return jnp.where(lane == LANES - 1, grp.astype(F32), gate)


MERGE_SPLIT = 2


def _merge_kernel(hyv_ref, of_ref, ob_ref, pg_ref, at_ref, x_ref, gt1_ref, sc2_ref, sh2_ref, gng_ref, n2g_ref,
                  whb_ref, wgd_ref, wml_ref, wo_ref, rw_ref, rb_ref, xo_ref, h2_ref, gate_ref):
    nv = GDN_HEADS * GDN_DV
    tm, D = x_ref.shape
    slabs = [slice(r, r + tm // MERGE_SPLIT) for r in range(0, tm, tm // MERGE_SPLIT)]

    def gdn_gate(rs):
        o = of_ref[rs, :] + ob_ref[rs, :]
        ys = []
        for h in range(GDN_HEADS):
            sl = slice(h * GDN_DV, (h + 1) * GDN_DV)
            oh = o[:, sl]
            on = oh * lax.rsqrt(jnp.mean(oh * oh, axis=-1, keepdims=True) + EPS) * gng_ref[...]
            ys.append(on * _silu(pg_ref[rs, sl]))
        return jnp.concatenate(ys, axis=1).astype(BF16)

    y_in = [gdn_gate(rs) for rs in slabs]
    branches = [(jnp.dot(hyv_ref[rs, :].astype(BF16), whb_ref[...], preferred_element_type=F32),
                 jnp.dot(y, wgd_ref[...], preferred_element_type=F32),
                 jnp.dot(at_ref[rs, :], wml_ref[...], preferred_element_type=F32)) for rs, y in zip(slabs, y_in)]
    merged = [(jax.nn.sigmoid(pg_ref[rs, nv:nv + D]) * y_hy
               + jax.nn.sigmoid(pg_ref[rs, nv + D:nv + 2 * D]) * y_gdn
               + jax.nn.sigmoid(pg_ref[rs, nv + 2 * D:]) * y_mla).astype(BF16)
              for rs, (y_hy, y_gdn, y_mla) in zip(slabs, branches)]
    mix = [jnp.dot(m, wo_ref[...], preferred_element_type=F32) for m in merged]
    h2s = []
    for rs, mx in zip(slabs, mix):
        xn = x_ref[rs, :] + gt1_ref[...] * mx
        xo_ref[rs, :] = xn
        y2 = xn * lax.rsqrt(jnp.mean(xn * xn, axis=-1, keepdims=True) + EPS) * n2g_ref[...]
        h2 = y2 * (1.0 + sc2_ref[...]) + sh2_ref[...]
        h2_ref[rs, :] = h2.astype(h2_ref.dtype)
        h2s.append(h2)
    logits = []
    for h2 in h2s:
        h_hi = h2.astype(BF16)
        h_lo = (h2 - h_hi.astype(F32)).astype(BF16)
        a = jnp.dot(h_hi, rw_ref[...], preferred_element_type=F32)
        logits.append(a[:, :LANES] + a[:, LANES:] + jnp.dot(h_lo, rw_ref[:, :LANES], preferred_element_type=F32))
    for rs, lg in zip(slabs, logits):
        gate_ref[rs, :] = _route(lg, rb_ref[...])


def merge_out(hyv, o_f, o_b, pg, attn, x, gt1, sc2, sh2, gdn_norm_g, norm2_g, w_hy, w_gdn, w_mla, w_out,
              router_w, router_b, tm, o_row0):
    B, S, D = x.shape
    tm = min(tm, S)
    assert o_row0 % tm == 0
    ob0 = o_row0 // tm
    row = lambda n: pl.BlockSpec((None, tm, n), lambda b, i: (b, i, 0))
    full = lambda a: pl.BlockSpec(a.shape, lambda b, i: (0, 0))
    nv = GDN_HEADS * GDN_DV
    o_spec = pl.BlockSpec((None, tm, nv), lambda b, i: (b, ob0 + i, 0))
    return pl.pallas_call(
        _merge_kernel,
        grid=(B, S // tm),
        in_specs=[row(HY_W), o_spec, o_spec, row(pg.shape[2]), row(MLA_HEADS * MLA_V), row(D),
                  _mod_spec(gt1, D), _mod_spec(sc2, D), _mod_spec(sh2, D),
                  full(gdn_norm_g), full(norm2_g), full(w_hy), full(w_gdn), full(w_mla), full(w_out),
                  full(router_w), full(router_b)],
        out_specs=[row(D), row(D), row(LANES)],
        out_shape=[jax.ShapeDtypeStruct((B, S, D), F32), jax.ShapeDtypeStruct((B, S, D), BF16),
                   jax.ShapeDtypeStruct((B, S, LANES), F32)],
        compiler_params=_cparams("parallel", "parallel"),
        name="merge_out",
    )(hyv, o_f, o_b, pg, attn, x, gt1, sc2, sh2, gdn_norm_g, norm2_g, w_hy, w_gdn, w_mla, w_out,
      router_w, router_b)


MOE_BLK = 128
MOE_STEP = 2


def _moe_kernel(h_ref, gate_ref, w1_ref, w3_ref, w2_ref, x_ref, gt2_ref, fg_ref, o_ref,
                xs_ref, gs_ref, y_ref, slot_ref, seg_ref, *, final_norm):
    e = pl.program_id(1)
    tm, D = h_ref.shape
    A = xs_ref.shape[0]

    @pl.when(e == 0)
    def _():
        gate = gate_ref[...]
        lane = lax.broadcasted_iota(jnp.int32, gate.shape, 1)
        grp = gate[:, LANES - 1:LANES]
        member = jnp.where(jnp.logical_and(lane.astype(F32) == grp, lane < N_GROUPS), 1.0, 0.0)
        ri = lax.broadcasted_iota(jnp.int32, (tm, tm), 0)
        ci = lax.broadcasted_iota(jnp.int32, (tm, tm), 1)
        before = jnp.where(ci < ri, 1.0, 0.0).astype(BF16)
        rank = jnp.dot(before, member.astype(BF16), preferred_element_type=F32)
        cnt = jnp.sum(member, axis=0, keepdims=True)
        blocks = jnp.floor((cnt + (MOE_BLK - 1)) * (1.0 / MOE_BLK))
        padded = blocks * MOE_BLK
        l1 = lane[0:1, :]
        p0, p1, p2 = padded[:, 0:1], padded[:, 1:2], padded[:, 2:3]
        start = jnp.where(l1 == 0, 0.0, jnp.where(l1 == 1, p0, jnp.where(l1 == 2, p0 + p1, p0 + p1 + p2)))
        slot = jnp.sum(member * (start + rank), axis=-1, keepdims=True)
        slot_b = jnp.broadcast_to(slot, (tm, LANES))
        slot_ref[...] = slot_b
        start_i, blocks_i = start.astype(jnp.int32), blocks.astype(jnp.int32)
        for g in range(N_GROUPS):
            seg_ref[g] = start_i[0, g]
            seg_ref[N_GROUPS + g] = blocks_i[0, g]
        slot_row = slot_b.T[0:1, :]
        pick = jnp.where(lax.broadcasted_iota(jnp.int32, (A, tm), 0).astype(F32) == slot_row, 1.0, 0.0).astype(BF16)
        g_hi = gate.astype(BF16)
        g_lo = (gate - g_hi.astype(F32)).astype(BF16)
        got = jnp.dot(pick, jnp.concatenate([h_ref[...], g_hi, g_lo], axis=1), preferred_element_type=F32)
        xs_ref[...] = got[:, :D].astype(BF16)
        gs_ref[...] = got[:, D:D + LANES] + got[:, D + LANES:]
        y_ref[...] = jnp.zeros_like(y_ref)

    g = e // (EXPERTS_PER_GROUP // MOE_STEP)
    first = seg_ref[g]
    lane_b = lax.broadcasted_iota(jnp.int32, (MOE_BLK, LANES), 1)

    def block(b, carry):
        rows = pl.ds(pl.multiple_of(first + b * MOE_BLK, MOE_BLK), MOE_BLK)
        xb = xs_ref[rows, :]
        gsb = gs_ref[rows, :]
        up = [(jnp.dot(xb, w1_ref[j], preferred_element_type=F32), jnp.dot(xb, w3_ref[j], preferred_element_type=F32))
              for j in range(MOE_STEP)]
        he = [(_silu(a) * b3).astype(BF16) for a, b3 in up]
        down = [jnp.dot(he[j], w2_ref[j], preferred_element_type=F32) for j in range(MOE_STEP)]
        acc = y_ref[rows, :]
        for j in range(MOE_STEP):
            ge = jnp.sum(jnp.where(lane_b == e * MOE_STEP + j, gsb, 0.0), axis=-1, keepdims=True)
            acc = acc + ge * down[j]
        y_ref[rows, :] = acc
        return carry

    lax.fori_loop(0, seg_ref[N_GROUPS + g], block, 0)

    @pl.when(e == pl.num_programs(1) - 1)
    def _():
        back = jnp.where(lax.broadcasted_iota(jnp.int32, (tm, A), 1).astype(F32) == slot_ref[:, 0:1], 1.0, 0.0)
        moe_out = jnp.dot(back.astype(BF16), y_ref[...].astype(BF16), preferred_element_type=F32)
        xn = x_ref[...] + gt2_ref[...] * moe_out
        if final_norm:
            xn = xn * lax.rsqrt(jnp.mean(xn * xn, axis=-1, keepdims=True) + EPS) * fg_ref[...]
        o_ref[...] = xn


def moe(h2, gate, w1, w3, w2, x, gt2, final_g, S, tm, final_norm):
    T, D = x.shape
    E, _, FF = w1.shape
    if gt2.shape[0] == 1:
        tm = min(tm, T)
        gt_spec = pl.BlockSpec((None, 1, D), lambda i, e: (0, 0, 0))
    else:
        tm = min(tm, S)
        per_b = S // tm
        gt_spec = pl.BlockSpec((None, 1, D), lambda i, e: (i // per_b, 0, 0))
    A = tm + N_GROUPS * MOE_BLK
    return pl.pallas_call(
        functools.partial(_moe_kernel, final_norm=final_norm),
        grid=(T // tm, E // MOE_STEP),
        in_specs=[pl.BlockSpec((tm, D), lambda i, e: (i, 0)),
                  pl.BlockSpec((tm, LANES), lambda i, e: (i, 0)),
                  pl.BlockSpec((MOE_STEP, D, FF), lambda i, e: (e, 0, 0)),
                  pl.BlockSpec((MOE_STEP, D, FF), lambda i, e: (e, 0, 0)),
                  pl.BlockSpec((MOE_STEP, FF, D), lambda i, e: (e, 0, 0)),
                  pl.BlockSpec((tm, D), lambda i, e: (i, 0)),
                  gt_spec,
                  pl.BlockSpec((1, D), lambda i, e: (0, 0))],
        out_specs=pl.BlockSpec((tm, D), lambda i, e: (i, 0)),
        out_shape=jax.ShapeDtypeStruct((T, D), F32),
        scratch_shapes=[pltpu.VMEM((A, D), BF16), pltpu.VMEM((A, LANES), F32), pltpu.VMEM((A, D), F32),
                        pltpu.VMEM((tm, LANES), F32), pltpu.SMEM((2 * N_GROUPS,), jnp.int32)],
        compiler_params=_cparams("parallel", "arbitrary"),
        name="moe",
    )(h2, gate, w1, w3, w2, x, gt2, final_g)


IN_SIZES = (3 * HY_W, GDN_QKV, GDN_HEADS * GDN_DV, 2 * GDN_HEADS, 2 * GDN_HEADS, MLA_Q_LORA, MLA_KV_LORA, MLA_ROPE)


def _split_w_in(w_in):
    D = w_in.shape[0]
    parts, off = [], 0
    for n in IN_SIZES:
        parts.append(w_in[:, off:off + n])
        off += n
    hy, qkv, z, a, b, cq, ckv, kr = parts
    gate = w_in[:, off:]
    zeros = lambda n: jnp.zeros((D, n), w_in.dtype)
    w_hy = hy
    w_gdn = jnp.concatenate([qkv, a, b, zeros(LANES - 4 * GDN_HEADS)], axis=1)
    w_mla = jnp.concatenate([cq, ckv, kr, zeros(LANES - MLA_ROPE)], axis=1)
    w_gate = jnp.concatenate([z, gate], axis=1)
    return [w.astype(BF16) for w in (w_hy, w_gdn, w_mla, w_gate)]


def _layer(x, cx, mod, mod_c, lw, tabs, router_w, router_b, final_g, update_ctx, last):
    B, S, D = x.shape
    Lc = cx.shape[1]
    sh1, sc1, gt1, sh2, sc2, gt2 = [m[:, None, :] for m in jnp.split(mod, 6, axis=-1)]
    csh1, csc1, cgt1, csh2, csc2, cgt2 = [m[:, None, :] for m in jnp.split(mod_c, 6, axis=-1)]
    n1g = lw['norm1_g'][None, :]
    w_hy, w_gdn, w_mla, w_gate = _split_w_in(lw['w_in'])

    Lt = S + Lc
    TM, TMC = 512, 256

    def project(w, joint):
        if joint:
            return in_proj_joint(x, cx, n1g, sc1, sh1, csc1, csh1, w, TM, TMC)
        return in_proj(x, n1g, sc1, sh1, w, TM), in_proj(cx, n1g, csc1, csh1, w, TMC)

    p_hy, c_hy = project(w_hy, False)
    p_gate, c_gate = project(w_gate, False)
    pc_gdn = project(w_gdn, True)
    pc_mla = project(w_mla, True)

    o_f, o_b = gdn_scan(*gdn_pre(pc_gdn, lw['gdn_conv_w'], lw['gdn_a_log'], lw['gdn_dt_bias'], TMC, S),
                        Lc // GDN_CHUNK)

    wq2, wk2 = mla_weights(lw['mla_w_uq'], lw['mla_w_ukv'])
    gq, gkv = lw['mla_q_norm_g'][None, :], lw['mla_kv_norm_g'][None, :]
    q_a, k_a, v_a = mla_proj(pc_mla, gq, gkv, wq2, wk2, tabs[0], tabs[1], TMC)
    attn_l = mla_attn(q_a, k_a, v_a, (0, S), (0, Lt), 512, 1024)

    def hyena(p, L):
        filt = hy_filter(L, lw['hy_f_w1'], lw['hy_f_b1'], lw['hy_f_w2'], lw['hy_f_b2'], lw['hy_f_w3'],
                         lw['hy_f_freq'], lw['hy_decay'])
        x0, u = hy_pre(p, lw['hy_conv_w'], lw['hy_conv_b'][None, :], 512)
        return hyena_long_conv(x0, u, filt, lw['hy_bias'])

    hyv_l = hyena(p_hy, S)

    wb = lambda name: lw[name].astype(BF16)
    rw = jnp.pad(router_w, ((0, 0), (0, LANES - N_EXPERTS)))
    rw_hi = rw.astype(BF16)
    rw = jnp.concatenate([rw_hi, (rw - rw_hi.astype(F32)).astype(BF16)], axis=1)
    rb = jnp.pad(router_b[None, :], ((0, 0), (0, LANES - N_EXPERTS)))
    n2g, gng = lw['norm2_g'][None, :], lw['gdn_norm_g'][None, :]
    w1, w3, w2 = wb('moe_w1'), wb('moe_w3'), wb('moe_w2')

    def finish(xx, hyv, o_row0, pg, attn, gt1_, sc2_, sh2_, gt2_, tm, tm_moe, fin):
        Bx, Sx, _ = xx.shape
        xn, h2, gate = merge_out(hyv, o_f, o_b, pg, attn, xx, gt1_, sc2_, sh2_, gng, n2g, wb('hy_out'), wb('gdn_out'),
                                 wb('mla_out'), wb('w_out'), rw, rb, tm, o_row0)
        out = moe(h2.reshape(Bx * Sx, D), gate.reshape(Bx * Sx, LANES), w1, w3, w2, xn.reshape(Bx * Sx, D),
                  gt2_, final_g, Sx, tm_moe, fin)
        return out.reshape(Bx, Sx, D)

    x_new = finish(x, hyv_l, 0, p_gate, attn_l, gt1, sc2, sh2, gt2, 512, 1024, last)
    if update_ctx:
        hyv_c = hyena(c_hy, Lc)
        attn_c = mla_attn(q_a, k_a, v_a, (S, Lc), (S, Lc), 256, 1024)
        cx = finish(cx, hyv_c, S, c_gate, attn_c, cgt1, csc2, csh2, cgt2, 256, 1024, False)
    return x_new, cx


def kernel(x, c, ctx, c_ctx, w_ada, b_ada, norm1_g, norm2_g, w_in, hy_conv_w, hy_conv_b, hy_f_w1, hy_f_b1, hy_f_w2, hy_f_b2, hy_f_w3, hy_f_freq, hy_decay, hy_bias, hy_out, gdn_conv_w, gdn_a_log, gdn_dt_bias, gdn_norm_g, gdn_out, mla_q_norm_g, mla_w_uq, mla_kv_norm_g, mla_w_ukv, mla_out, w_out, moe_w1, moe_w3, moe_w2, router_w, router_b, final_norm_g):
    per_layer = dict(norm1_g=norm1_g, norm2_g=norm2_g, w_in=w_in, hy_conv_w=hy_conv_w, hy_conv_b=hy_conv_b,
                     hy_f_w1=hy_f_w1, hy_f_b1=hy_f_b1, hy_f_w2=hy_f_w2, hy_f_b2=hy_f_b2, hy_f_w3=hy_f_w3,
                     hy_f_freq=hy_f_freq, hy_decay=hy_decay, hy_bias=hy_bias, hy_out=hy_out,
                     gdn_conv_w=gdn_conv_w, gdn_a_log=gdn_a_log, gdn_dt_bias=gdn_dt_bias, gdn_norm_g=gdn_norm_g,
                     gdn_out=gdn_out, mla_q_norm_g=mla_q_norm_g, mla_w_uq=mla_w_uq, mla_kv_norm_g=mla_kv_norm_g,
                     mla_w_ukv=mla_w_ukv, mla_out=mla_out, w_out=w_out, moe_w1=moe_w1, moe_w3=moe_w3, moe_w2=moe_w2)
    B, S, D = x.shape
    Lc = ctx.shape[1]
    depth = w_ada.shape[0]
    rows = S // GRID_W
    row = jnp.repeat(jnp.arange(rows, dtype=jnp.int32), GRID_W)
    col = jnp.tile(jnp.arange(GRID_W, dtype=jnp.int32), rows)
    zero = jnp.zeros((Lc,), jnp.int32)
    tabs = rope_tables(jnp.concatenate([row, zero]), jnp.concatenate([col, zero]))
    cc = jnp.concatenate([c, c_ctx[None, :], jnp.zeros((2 * SUBLANES - B - 1, D), F32)], axis=0)
    final_g = final_norm_g[None, :]
    cx = ctx
    for l in range(depth):
        lw = {k: v[l] for k, v in per_layer.items()}
        mods = ada_mod(cc, w_ada[l], b_ada[l][None, :])
        x, cx = _layer(x, cx, mods[:B], mods[B:B + 1], lw, tabs, router_w, router_b, final_g,
                       l < depth - 1, l == depth - 1)
    return x
```
